```python
import math
import jax, jax.numpy as jnp
from jax import lax
import numpy as np

D_MODEL = 1024
BATCH = 8
SEQ = 4096
DEPTH = 2
DEC_BATCH = 128
DEC_SEQ = 1
PAST_LEN = 16384
PAGE_SIZE = 128

HEAD_DIM = 64
N_HEADS = 8
N_KV_HEADS = 2
GQA_GROUP = N_HEADS // N_KV_HEADS
ATTN_WIDTH = N_HEADS * HEAD_DIM
KV_WIDTH = N_KV_HEADS * HEAD_DIM
WINDOW = 128
BLOCK = WINDOW
POOL_SIZES = (2, 4, 8, 16)
POOL_GROUP_WIDTH = 128
POOL_WIDTH = len(POOL_SIZES) * POOL_GROUP_WIDTH
POOL_STATE = max(POOL_SIZES) - 1
MIX_WIDTH = ATTN_WIDTH + POOL_WIDTH
IN_WIDTH = ATTN_WIDTH + 2 * KV_WIDTH + POOL_WIDTH
D_FF = 2816
N_BUCKETS = 32
MAX_DISTANCE = 128
N_SUBLAYERS = 3
N_MOD = 3 * N_SUBLAYERS
EPS = 1e-6

kernel_name = "hybrid_swa_pool_macaron_adaln_step"


def rms_norm(x, g):
    xf = x.astype(jnp.float32)
    y = xf * lax.rsqrt(jnp.mean(xf * xf, axis=-1, keepdims=True) + EPS)
    return (y * g.astype(jnp.float32)).astype(x.dtype)


def swiglu(h, wg, wu, wd):
    return (jax.nn.silu(h @ wg) * (h @ wu)) @ wd


def t5_bucket(dist):
    n = jnp.maximum(dist, 0)
    max_exact = N_BUCKETS // 2
    nf = jnp.maximum(n, 1).astype(jnp.float32)
    large = max_exact + (jnp.log(nf / max_exact) / math.log(MAX_DISTANCE / max_exact)
                         * (N_BUCKETS - max_exact)).astype(jnp.int32)
    large = jnp.minimum(large, N_BUCKETS - 1)
    return jnp.where(n < max_exact, n, large)


def window_attention(q, k, v, key_valid, offset, sinks, rel_bias):
    B, N, Q = q.shape[:3]
    K = k.shape[2]
    qg = q.reshape(B, N, Q, N_KV_HEADS, GQA_GROUP, HEAD_DIM)
    s = jnp.einsum("bnqkgd,bnskd->bnkgqs", qg, k,
                   preferred_element_type=jnp.float32) * (HEAD_DIM ** -0.5)
    dist = jnp.arange(Q)[:, None] + offset - jnp.arange(K)[None, :]
    bias = rel_bias[t5_bucket(dist)].astype(jnp.float32)
    bias = bias.transpose(2, 0, 1).reshape(N_KV_HEADS, GQA_GROUP, Q, K)
    valid = ((dist >= 0) & (dist <= WINDOW))[None] & key_valid[:, None, :]
    s = jnp.where(valid[None, :, None, None], s + bias, -jnp.inf)
    sink = sinks.astype(jnp.float32).reshape(N_KV_HEADS, GQA_GROUP, 1, 1)
    m = jnp.maximum(jnp.max(s, axis=-1, keepdims=True), sink)
    p = jnp.exp(s - m)
    denom = jnp.sum(p, axis=-1, keepdims=True) + jnp.exp(sink - m)
    o = jnp.einsum("bnkgqs,bnskd->bnqkgd", (p / denom).astype(v.dtype), v)
    return o.reshape(B, N, Q, ATTN_WIDTH)


def prompt_attention(q, k, v, sinks, rel_bias):
    B, S = q.shape[:2]
    nb = S // BLOCK
    qb = q.reshape(B, nb, BLOCK, N_HEADS, HEAD_DIM)
    kb = k.reshape(B, nb, BLOCK, N_KV_HEADS, HEAD_DIM)
    vb = v.reshape(B, nb, BLOCK, N_KV_HEADS, HEAD_DIM)

    def with_prev(xb):
        prev = jnp.pad(xb, ((0, 0), (1, 0), (0, 0), (0, 0), (0, 0)))[:, :-1]
        return jnp.concatenate([prev, xb], axis=2)

    key_valid = jnp.concatenate(
        [jnp.broadcast_to(jnp.arange(nb)[:, None] > 0, (nb, BLOCK)),
         jnp.ones((nb, BLOCK), dtype=bool)], axis=1)
    o = window_attention(qb, with_prev(kb), with_prev(vb), key_valid, BLOCK, sinks, rel_bias)
    return o.reshape(B, S, ATTN_WIDTH)


def multiscale_pool(u_ext, pos0, n_prev, pool_w, pool_scale):
    L = u_ext.shape[1]
    uf = u_ext.astype(jnp.float32)
    cs = jnp.cumsum(uf, axis=1)
    pos = pos0 + jnp.arange(L)
    outs = []
    for g, w in enumerate(POOL_SIZES):
        lo, hi = g * POOL_GROUP_WIDTH, (g + 1) * POOL_GROUP_WIDTH
        cg = cs[..., lo:hi]
        lagged = jnp.pad(cg, ((0, 0), (w, 0), (0, 0)))[:, :L]
        count = jnp.minimum(pos + 1, w).astype(jnp.float32)[None, :, None]
        pooled = (cg - lagged) / count - uf[..., lo:hi]
        pooled = pooled[:, n_prev:].astype(u_ext.dtype)
        outs.append(pooled @ pool_w[g])
    return jnp.concatenate(outs, axis=-1) * pool_scale


def split_projection(z):
    return jnp.split(z, [ATTN_WIDTH, ATTN_WIDTH + KV_WIDTH, ATTN_WIDTH + 2 * KV_WIDTH], axis=-1)


def run_trunk(x, c, mixer, w_ada, b_ada, norm_gain, w_in, w_out,
              ffn1_wg, ffn1_wu, ffn1_wd, ffn2_wg, ffn2_wu, ffn2_wd, final_gain):
    states = []
    for l in range(DEPTH):
        mod = (jax.nn.silu(c) @ w_ada[l] + b_ada[l]).reshape(c.shape[0], 1, N_MOD, D_MODEL)

        def modulate(h, i):
            return rms_norm(h, norm_gain[l, i]) * (1 + mod[:, :, 3 * i + 1]) + mod[:, :, 3 * i]

        def gate(i):
            return mod[:, :, 3 * i + 2]

        x = x + 0.5 * gate(0) * swiglu(modulate(x, 0), ffn1_wg[l], ffn1_wu[l], ffn1_wd[l])
        z = modulate(x, 1) @ w_in[l]
        mixed, st = mixer(l, z)
        x = x + gate(1) * (mixed @ w_out[l])
        x = x + 0.5 * gate(2) * swiglu(modulate(x, 2), ffn2_wg[l], ffn2_wu[l], ffn2_wd[l])
        states.append(st)
    return rms_norm(x, final_gain), states


def setup_inputs(seed: int = 0) -> dict:
    key = jax.random.key(seed)
    ks = jax.random.split(key, 24)
    f32 = jnp.float32
    nrm = lambda k, shape, s: jax.random.normal(k, shape, f32) * s
    return {
        "x_prompt": nrm(ks[0], (BATCH, SEQ, D_MODEL), 1.0),
        "x_sample": nrm(ks[1], (DEC_BATCH, DEC_SEQ, D_MODEL), 1.0),
        "c_prompt": nrm(ks[2], (BATCH, D_MODEL), 1.0),
        "c_sample": nrm(ks[3], (DEC_BATCH, D_MODEL), 1.0),
        "cache_k": nrm(ks[4], (DEPTH, DEC_BATCH, WINDOW, N_KV_HEADS, HEAD_DIM), 1.0),
        "cache_v": nrm(ks[5], (DEPTH, DEC_BATCH, WINDOW, N_KV_HEADS, HEAD_DIM), 1.0),
        "state_pool": nrm(ks[6], (DEPTH, DEC_BATCH, POOL_STATE, POOL_WIDTH), 1.0),
        "w_ada": nrm(ks[7], (DEPTH, D_MODEL, N_MOD * D_MODEL), D_MODEL ** -0.5),
        "b_ada": nrm(ks[8], (DEPTH, N_MOD * D_MODEL), 0.02),
        "norm_gain": 1.0 + nrm(ks[9], (DEPTH, N_SUBLAYERS, D_MODEL), 0.02),
        "w_in": nrm(ks[10], (DEPTH, D_MODEL, IN_WIDTH), D_MODEL ** -0.5),
        "w_out": nrm(ks[11], (DEPTH, MIX_WIDTH, D_MODEL), MIX_WIDTH ** -0.5),
        "sinks": nrm(ks[12], (DEPTH, N_HEADS), 1.0),
        "rel_bias": nrm(ks[13], (N_BUCKETS, N_HEADS), 0.5),
        "pool_w": nrm(ks[14], (DEPTH, len(POOL_SIZES), POOL_GROUP_WIDTH, POOL_GROUP_WIDTH),
                      POOL_GROUP_WIDTH ** -0.5),
        "pool_scale": 1.0 + nrm(ks[15], (DEPTH, POOL_WIDTH), 0.1),
        "ffn1_wg": nrm(ks[16], (DEPTH, D_MODEL, D_FF), D_MODEL ** -0.5),
        "ffn1_wu": nrm(ks[17], (DEPTH, D_MODEL, D_FF), D_MODEL ** -0.5),
        "ffn1_wd": nrm(ks[18], (DEPTH, D_FF, D_MODEL), D_FF ** -0.5),
        "ffn2_wg": nrm(ks[19], (DEPTH, D_MODEL, D_FF), D_MODEL ** -0.5),
        "ffn2_wu": nrm(ks[20], (DEPTH, D_MODEL, D_FF), D_MODEL ** -0.5),
        "ffn2_wd": nrm(ks[21], (DEPTH, D_FF, D_MODEL), D_FF ** -0.5),
        "final_gain": 1.0 + nrm(ks[22], (D_MODEL,), 0.02),
    }


def reference(x_prompt, x_sample, c_prompt, c_sample, cache_k, cache_v, state_pool,
              w_ada, b_ada, norm_gain, w_in, w_out, sinks, rel_bias, pool_w, pool_scale,
              ffn1_wg, ffn1_wu, ffn1_wd, ffn2_wg, ffn2_wu, ffn2_wd, final_gain):

    def prompt_mixer(l, z):
        B, S = z.shape[:2]
        q, k, v, u = split_projection(z)
        k = k.reshape(B, S, N_KV_HEADS, HEAD_DIM)
        v = v.reshape(B, S, N_KV_HEADS, HEAD_DIM)
        attn = prompt_attention(q.reshape(B, S, N_HEADS, HEAD_DIM), k, v, sinks[l], rel_bias)
        pool = multiscale_pool(u, 0, 0, pool_w[l], pool_scale[l])
        return (jnp.concatenate([attn, pool], axis=-1),
                (k[:, -WINDOW:], v[:, -WINDOW:], u[:, -POOL_STATE:]))

    def sample_mixer(l, z):
        B, T = z.shape[:2]
        q, k, v, u = split_projection(z)
        k_all = jnp.concatenate([cache_k[l], k.reshape(B, T, N_KV_HEADS, HEAD_DIM)], axis=1)
        v_all = jnp.concatenate([cache_v[l], v.reshape(B, T, N_KV_HEADS, HEAD_DIM)], axis=1)
        key_valid = jnp.ones((1, WINDOW + T), dtype=bool)
        attn = window_attention(q.reshape(B, 1, T, N_HEADS, HEAD_DIM), k_all[:, None],
                                v_all[:, None], key_valid, WINDOW, sinks[l], rel_bias)[:, 0]
        u_ext = jnp.concatenate([state_pool[l], u], axis=1)
        pool = multiscale_pool(u_ext, PAST_LEN - POOL_STATE, POOL_STATE, pool_w[l], pool_scale[l])
        return (jnp.concatenate([attn, pool], axis=-1),
                (k_all[:, -WINDOW:], v_all[:, -WINDOW:], u_ext[:, -POOL_STATE:]))

    y_prompt, st_p = run_trunk(x_prompt, c_prompt, prompt_mixer, w_ada, b_ada, norm_gain, w_in,
                               w_out, ffn1_wg, ffn1_wu, ffn1_wd, ffn2_wg, ffn2_wu, ffn2_wd,
                               final_gain)
    y_sample, st_s = run_trunk(x_sample, c_sample, sample_mixer, w_ada, b_ada, norm_gain, w_in,
                               w_out, ffn1_wg, ffn1_wu, ffn1_wd, ffn2_wg, ffn2_wu, ffn2_wd,
                               final_gain)
    new_k_prompt = jnp.stack([s[0] for s in st_p])
    new_v_prompt = jnp.stack([s[1] for s in st_p])
    new_pool_prompt = jnp.stack([s[2] for s in st_p])
    new_k_sample = jnp.stack([s[0] for s in st_s])
    new_v_sample = jnp.stack([s[1] for s in st_s])
    new_pool_sample = jnp.stack([s[2] for s in st_s])
    return (y_prompt, y_sample, new_k_prompt, new_v_prompt, new_pool_prompt,
            new_k_sample, new_v_sample, new_pool_sample)
```

```python
import functools
import math

import numpy as np
import jax
import jax.numpy as jnp
from jax import lax
from jax.experimental import pallas as pl
from jax.experimental.pallas import tpu as pltpu

D_MODEL = 1024
DEPTH = 2
HEAD_DIM = 64
N_HEADS = 8
N_KV_HEADS = 2
ATTN_WIDTH = N_HEADS * HEAD_DIM
KV_WIDTH = N_KV_HEADS * HEAD_DIM
WINDOW = 128
POOL_SIZES = (2, 4, 8, 16)
POOL_GROUP_WIDTH = 128
POOL_WIDTH = len(POOL_SIZES) * POOL_GROUP_WIDTH
POOL_STATE = max(POOL_SIZES) - 1
IN_WIDTH = ATTN_WIDTH + 2 * KV_WIDTH + POOL_WIDTH
D_FF = 2816
N_BUCKETS = 32
MAX_DISTANCE = 128
N_MOD = 9
EPS = 1e-6

LANES = 128
FF_CHUNK = 256
N_FF_CHUNKS = D_FF // FF_CHUNK
TM_PROMPT = 512
HIST = 16
VMEM_LIMIT = 56 * 1024 * 1024

F32 = jnp.float32
BF16 = jnp.bfloat16
NEG_INF = float("-inf")


def _q_perm():
    idx = np.empty(ATTN_WIDTH, np.int32)
    for c in range(4):
        for half in range(2):
            for d in range(HEAD_DIM):
                idx[c * LANES + half * HEAD_DIM + d] = (c + 4 * half) * HEAD_DIM + d
    return idx


def _head_of_row():
    return np.array([(r // 2) + 4 * (r % 2) for r in range(N_HEADS)], np.int32)


def _t5_bucket_np(dist):
    n = np.maximum(dist, 0)
    max_exact = N_BUCKETS // 2
    nf = np.maximum(n, 1).astype(np.float32)
    large = max_exact + (np.log(nf / max_exact) / math.log(MAX_DISTANCE / max_exact)
                         * (N_BUCKETS - max_exact)).astype(np.int32)
    large = np.minimum(large, N_BUCKETS - 1)
    return np.where(n < max_exact, n, large).astype(np.int32)


def _bucket_table():
    dist = np.arange(WINDOW)[:, None] + WINDOW - np.arange(2 * WINDOW)[None, :]
    return _t5_bucket_np(dist)


def _full(shape):
    return pl.BlockSpec(shape, lambda *_: (0,) * len(shape))


def _resident(shape):
    return pl.BlockSpec(shape, lambda *_: (0,) * len(shape), pipeline_mode=pl.Buffered(1))


def _rms_mod(x, gain, shift, scale):
    y = x * lax.rsqrt(jnp.mean(x * x, axis=-1, keepdims=True) + EPS)
    return (y * gain) * (1.0 + scale) + shift


def _swiglu(h_ref, act_ref, wgu_ref, wd_ref):
    for c in range(N_FF_CHUNKS):
        gu = jnp.dot(h_ref[...], wgu_ref[:, c * 2 * FF_CHUNK:(c + 1) * 2 * FF_CHUNK],
                     preferred_element_type=F32)
        g = gu[:, :FF_CHUNK]
        u = gu[:, FF_CHUNK:]
        act_ref[:, c * FF_CHUNK:(c + 1) * FF_CHUNK] = ((g / (1.0 + jnp.exp(-g))) * u).astype(BF16)
    return jnp.dot(act_ref[...], wd_ref[...], preferred_element_type=F32)


def _mod(mod_ref, k):
    return mod_ref[:, k * D_MODEL:(k + 1) * D_MODEL]


def _mod_kernel(c_ref, w_ref, b_ref, o_ref):
    c = c_ref[...]
    a = (c / (1.0 + jnp.exp(-c))).astype(BF16)
    o_ref[...] = jnp.dot(a, w_ref[...].astype(BF16), preferred_element_type=F32) + b_ref[...]


def _modulation(c_all, w_ada, b_ada):
    rows = c_all.shape[0]
    n_chunk = 1024
    n_steps = N_MOD * D_MODEL // n_chunk
    return pl.pallas_call(
        _mod_kernel,
        out_shape=jax.ShapeDtypeStruct((DEPTH, rows, N_MOD * D_MODEL), F32),
        grid=(DEPTH, n_steps),
        in_specs=[
            pl.BlockSpec((rows, D_MODEL), lambda l, j: (0, 0)),
            pl.BlockSpec((None, D_MODEL, n_chunk), lambda l, j: (l, 0, j)),
            pl.BlockSpec((None, 1, n_chunk), lambda l, j: (l, 0, j)),
        ],
        out_specs=pl.BlockSpec((None, rows, n_chunk), lambda l, j: (l, 0, j)),
        compiler_params=pltpu.CompilerParams(
            dimension_semantics=("arbitrary", "arbitrary"), vmem_limit_bytes=VMEM_LIMIT),
        name="adaln_modulation",
    )(c_all, w_ada, b_ada.reshape(DEPTH, 1, N_MOD * D_MODEL))


def _bias_kernel(rb_ref, bucket_ref, o_ref):
    heads = _head_of_row()
    bucket = bucket_ref[...]
    eq = [bucket == b for b in range(N_BUCKETS)]
    for r in range(N_HEADS):
        acc = jnp.zeros((WINDOW, 2 * WINDOW), F32)
        for b in range(N_BUCKETS):
            acc = jnp.where(eq[b], rb_ref[b, int(heads[r])], acc)
        o_ref[r * WINDOW:(r + 1) * WINDOW, :] = acc


def _bias_table(rel_bias):
    return pl.pallas_call(
        _bias_kernel,
        out_shape=jax.ShapeDtypeStruct((N_HEADS * WINDOW, 2 * WINDOW), F32),
        in_specs=[pl.BlockSpec(memory_space=pltpu.SMEM),
                  pl.BlockSpec(memory_space=pltpu.VMEM)],
        out_specs=pl.BlockSpec(memory_space=pltpu.VMEM),
        name="rel_bias_table",
    )(rel_bias, jnp.asarray(_bucket_table()))


def _ka_kernel(x_ref, mod_ref, gain_ref, wgu_ref, wd_ref, win_ref,
               x1_ref, q_ref, k_ref, v_ref, u_ref, h_ref, act_ref):
    x = x_ref[...]
    h_ref[...] = _rms_mod(x, gain_ref[0:1, :], _mod(mod_ref, 0), _mod(mod_ref, 1)).astype(BF16)
    y = _swiglu(h_ref, act_ref, wgu_ref, wd_ref)
    x1 = x + 0.5 * _mod(mod_ref, 2) * y
    x1_ref[...] = x1
    hm = _rms_mod(x1, gain_ref[1:2, :], _mod(mod_ref, 3), _mod(mod_ref, 4)).astype(BF16)
    z = jnp.dot(hm, win_ref[...], preferred_element_type=F32)
    q_ref[...] = z[:, :ATTN_WIDTH].astype(BF16)
    k_ref[...] = z[:, ATTN_WIDTH:ATTN_WIDTH + KV_WIDTH]
    v_ref[...] = z[:, ATTN_WIDTH + KV_WIDTH:ATTN_WIDTH + 2 * KV_WIDTH]
    u_ref[...] = z[:, ATTN_WIDTH + 2 * KV_WIDTH:]


def _ka_call(x, mod, gain, wgu, wd, win, tm, name):
    g, r, _ = x.shape
    mr = mod.shape[1]
    steps = r // tm
    row = lambda w: pl.BlockSpec((None, tm, w), lambda b, t: (b, t, 0))
    mod_spec = (pl.BlockSpec((None, 1, N_MOD * D_MODEL), lambda b, t: (b, 0, 0)) if mr == 1 else
                pl.BlockSpec((None, tm, N_MOD * D_MODEL), lambda b, t: (b, t, 0)))
    return pl.pallas_call(
        _ka_kernel,
        out_shape=(jax.ShapeDtypeStruct((g, r, D_MODEL), F32),
                   jax.ShapeDtypeStruct((g, r, ATTN_WIDTH), BF16),
                   jax.ShapeDtypeStruct((g, r, KV_WIDTH), F32),
                   jax.ShapeDtypeStruct((g, r, KV_WIDTH), F32),
                   jax.ShapeDtypeStruct((g, r, POOL_WIDTH), F32)),
        grid=(g, steps),
        in_specs=[row(D_MODEL), mod_spec, _resident((3, D_MODEL)),
                  _resident((D_MODEL, 2 * D_FF)), _resident((D_FF, D_MODEL)),
                  _resident((D_MODEL, IN_WIDTH))],
        out_specs=(row(D_MODEL), row(ATTN_WIDTH), row(KV_WIDTH), row(KV_WIDTH), row(POOL_WIDTH)),
        scratch_shapes=[pltpu.VMEM((tm, D_MODEL), BF16), pltpu.VMEM((tm, D_FF), BF16)],
        compiler_params=pltpu.CompilerParams(
            dimension_semantics=("arbitrary", "arbitrary"), vmem_limit_bytes=VMEM_LIMIT),
        name=name,
    )(x, mod, gain, wgu, wd, win)


def _attn_block(qb, kk, vv, bias_ref, sink_ref, valid):
    lo = lax.broadcasted_iota(jnp.int32, (WINDOW, LANES), 1) < HEAD_DIM
    zero = jnp.zeros((WINDOW, LANES), BF16)
    pieces = []
    for c in range(4):
        qc = qb[:, c * LANES:(c + 1) * LANES]
        pieces += [jnp.where(lo, qc, zero), jnp.where(lo, zero, qc)]
    qs = jnp.concatenate(pieces, axis=0)
    s = lax.dot_general(qs, kk, (((1,), (1,)), ((), ())), preferred_element_type=F32)
    ps = []
    for r in range(N_HEADS):
        rows = slice(r * WINDOW, (r + 1) * WINDOW)
        sr = jnp.where(valid, s[rows] + bias_ref[rows, :], NEG_INF)
        sink = sink_ref[rows, :]
        m = jnp.maximum(jnp.max(sr, axis=-1, keepdims=True), sink)
        p = jnp.exp(sr - m)
        denom = jnp.sum(p, axis=-1, keepdims=True) + jnp.exp(sink - m)
        ps.append((p / denom).astype(BF16))
    o = jnp.dot(jnp.concatenate(ps, axis=0), vv, preferred_element_type=F32)
    cols = [jnp.where(lo, o[(2 * c) * WINDOW:(2 * c + 1) * WINDOW], o[(2 * c + 1) * WINDOW:(2 * c + 2) * WINDOW])
            for c in range(4)]
    return jnp.concatenate(cols, axis=1)


def _tail(x1, mixed_ref, mod_ref, gain_ref, wout_ref, wgu_ref, wd_ref, fgain_ref, h_ref, act_ref, final):
    y = jnp.dot(mixed_ref[...], wout_ref[...], preferred_element_type=F32)
    x2 = x1 + _mod(mod_ref, 5) * y
    h_ref[...] = _rms_mod(x2, gain_ref[2:3, :], _mod(mod_ref, 6), _mod(mod_ref, 7)).astype(BF16)
    x3 = x2 + 0.5 * _mod(mod_ref, 8) * _swiglu(h_ref, act_ref, wgu_ref, wd_ref)
    if final:
        x3 = x3 * lax.rsqrt(jnp.mean(x3 * x3, axis=-1, keepdims=True) + EPS) * fgain_ref[...]
    return x3


def _kb_kernel(x1_ref, q_ref, k_ref, v_ref, kp_ref, vp_ref, u_ref, up_ref, mod_ref, gain_ref,
               bias_ref, sink_ref, poolw_ref, pscale_ref, wout_ref, wgu_ref, wd_ref, fgain_ref,
               o_ref, mixed_ref, h_ref, act_ref, uext_ref, *, final):
    tm = x1_ref.shape[0]
    t = pl.program_id(1)
    first = t == 0

    kext = jnp.concatenate([kp_ref[...], k_ref[...]], axis=0).astype(BF16)
    vext = jnp.concatenate([vp_ref[...], v_ref[...]], axis=0).astype(BF16)
    qi = lax.broadcasted_iota(jnp.int32, (WINDOW, 2 * WINDOW), 0)
    kj = lax.broadcasted_iota(jnp.int32, (WINDOW, 2 * WINDOW), 1)
    dist = qi + WINDOW - kj
    band = (dist >= 0) & (dist <= WINDOW)
    band0 = band & (kj >= jnp.where(first, WINDOW, 0))
    for j in range(tm // WINDOW):
        rows = slice(j * WINDOW, (j + 1) * WINDOW)
        keys = slice(j * WINDOW, (j + 2) * WINDOW)
        a = _attn_block(q_ref[rows, :], kext[keys], vext[keys], bias_ref, sink_ref, band0 if j == 0 else band)
        mixed_ref[rows, 0:ATTN_WIDTH] = a.astype(BF16)

    uext_ref[0:HIST, :] = jnp.where(first, 0.0, up_ref[...])
    uext_ref[HIST:, :] = u_ref[...]
    pos = t * tm + lax.broadcasted_iota(jnp.int32, (tm, POOL_GROUP_WIDTH), 0)
    for g, w in enumerate(POOL_SIZES):
        cols = slice(g * POOL_GROUP_WIDTH, (g + 1) * POOL_GROUP_WIDTH)
        ug = uext_ref[HIST:, cols]
        acc = ug
        for d in range(1, w):
            acc = acc + uext_ref[HIST - d:HIST - d + tm, cols]
        count = jnp.minimum(pos + 1, w).astype(F32)
        pooled = (acc / count - ug).astype(BF16)
        pg = jnp.dot(pooled, poolw_ref[g], preferred_element_type=F32) * pscale_ref[:, cols]
        mixed_ref[:, ATTN_WIDTH + g * POOL_GROUP_WIDTH:ATTN_WIDTH + (g + 1) * POOL_GROUP_WIDTH] = pg.astype(BF16)

    o_ref[...] = _tail(x1_ref[...], mixed_ref, mod_ref, gain_ref, wout_ref, wgu_ref, wd_ref, fgain_ref,
                       h_ref, act_ref, final)


def _kb_call(x1, q, k, v, u, mod, gain, bias, sink, poolw, pscale, wout, wgu, wd, fgain, final, name):
    g, r, _ = x1.shape
    tm = TM_PROMPT
    nb = tm // WINDOW
    nh = tm // HIST
    row = lambda w: pl.BlockSpec((None, tm, w), lambda b, t: (b, t, 0))
    prev_kv = pl.BlockSpec((None, WINDOW, KV_WIDTH), lambda b, t: (b, jnp.maximum(t * nb - 1, 0), 0))
    prev_u = pl.BlockSpec((None, HIST, POOL_WIDTH), lambda b, t: (b, jnp.maximum(t * nh - 1, 0), 0))
    return pl.pallas_call(
        functools.partial(_kb_kernel, final=final),
        out_shape=jax.ShapeDtypeStruct((g, r, D_MODEL), F32),
        grid=(g, r // tm),
        in_specs=[row(D_MODEL), row(ATTN_WIDTH), row(KV_WIDTH), row(KV_WIDTH), prev_kv, prev_kv,
                  row(POOL_WIDTH), prev_u,
                  pl.BlockSpec((None, 1, N_MOD * D_MODEL), lambda b, t: (b, 0, 0)),
                  _resident((3, D_MODEL)),
                  _resident((N_HEADS * WINDOW, 2 * WINDOW)), _resident((N_HEADS * WINDOW, 1)),
                  _resident((len(POOL_SIZES), POOL_GROUP_WIDTH, POOL_GROUP_WIDTH)), _resident((1, POOL_WIDTH)),
                  _resident((D_MODEL, D_MODEL)), _resident((D_MODEL, 2 * D_FF)), _resident((D_FF, D_MODEL)),
                  _resident((1, D_MODEL))],
        out_specs=row(D_MODEL),
        scratch_shapes=[pltpu.VMEM((tm, D_MODEL), BF16), pltpu.VMEM((tm, D_MODEL), BF16),
                        pltpu.VMEM((tm, D_FF), BF16), pltpu.VMEM((HIST + tm, POOL_WIDTH), F32)],
        compiler_params=pltpu.CompilerParams(
            dimension_semantics=("arbitrary", "arbitrary"), vmem_limit_bytes=VMEM_LIMIT),
        name=name,
    )(x1, q, k, v, k, v, u, u, mod, gain, bias, sink, poolw, pscale, wout, wgu, wd, fgain)


def _ks_kernel(qs_ref, kn_ref, vn_ref, ck_ref, cv_ref, bias_ref, sink_ref, o_ref):
    bt = qs_ref.shape[0]
    bias_c = bias_ref[:, 0:WINDOW]
    bias_n = bias_ref[:, WINDOW:WINDOW + 1]
    sink = sink_ref[...]
    for b in range(bt):
        qs = qs_ref[b]
        kc = ck_ref[b].astype(BF16)
        vc = cv_ref[b].astype(BF16)
        kn = kn_ref[b].astype(BF16).astype(F32)
        vn = vn_ref[b].astype(BF16).astype(F32)
        s_c = lax.dot_general(qs, kc, (((1,), (1,)), ((), ())), preferred_element_type=F32) + bias_c
        s_n = jnp.sum(qs.astype(F32) * kn, axis=-1, keepdims=True) + bias_n
        m = jnp.maximum(jnp.maximum(jnp.max(s_c, axis=-1, keepdims=True), s_n), sink)
        p_c = jnp.exp(s_c - m)
        p_n = jnp.exp(s_n - m)
        denom = jnp.sum(p_c, axis=-1, keepdims=True) + p_n + jnp.exp(sink - m)
        o = jnp.dot((p_c / denom).astype(BF16), vc, preferred_element_type=F32)
        o_ref[b] = o + (p_n / denom).astype(BF16).astype(F32) * vn


def _ks_call(qs, kn, vn, cache_k, cache_v, layer, bias_s, sink_s):
    nb = qs.shape[0]
    bt = 8
    per_b = lambda a, c: pl.BlockSpec((bt, a, c), lambda i: (i, 0, 0))
    cache = pl.BlockSpec((None, bt, WINDOW, KV_WIDTH), lambda i: (layer, i, 0, 0))
    return pl.pallas_call(
        _ks_kernel,
        out_shape=jax.ShapeDtypeStruct((nb, N_HEADS, KV_WIDTH), F32),
        grid=(nb // bt,),
        in_specs=[per_b(N_HEADS, KV_WIDTH), per_b(1, KV_WIDTH), per_b(1, KV_WIDTH), cache, cache,
                  _full((N_HEADS, 2 * WINDOW)), _full((N_HEADS, 1))],
        out_specs=per_b(N_HEADS, KV_WIDTH),
        compiler_params=pltpu.CompilerParams(dimension_semantics=("arbitrary",), vmem_limit_bytes=VMEM_LIMIT),
        name="sample_attention",
    )(qs, kn, vn, cache_k, cache_v, bias_s, sink_s)


def _kc_kernel(x1_ref, attn_ref, u_ref, hist_ref, mod_ref, gain_ref, poolw_ref, pscale_ref,
               wout_ref, wgu_ref, wd_ref, fgain_ref, o_ref, mixed_ref, h_ref, act_ref, *, final):
    mixed_ref[:, 0:ATTN_WIDTH] = attn_ref[...].astype(BF16)
    for g, w in enumerate(POOL_SIZES):
        ug = u_ref[:, g * POOL_GROUP_WIDTH:(g + 1) * POOL_GROUP_WIDTH]
        acc = ug
        for d in range(1, w):
            r = POOL_STATE - d
            acc = acc + hist_ref[:, r * POOL_WIDTH + g * POOL_GROUP_WIDTH:r * POOL_WIDTH + (g + 1) * POOL_GROUP_WIDTH]
        pooled = (acc / float(w) - ug).astype(BF16)
        pg = (jnp.dot(pooled, poolw_ref[g], preferred_element_type=F32)
              * pscale_ref[:, g * POOL_GROUP_WIDTH:(g + 1) * POOL_GROUP_WIDTH])
        mixed_ref[:, ATTN_WIDTH + g * POOL_GROUP_WIDTH:ATTN_WIDTH + (g + 1) * POOL_GROUP_WIDTH] = pg.astype(BF16)
    o_ref[...] = _tail(x1_ref[...], mixed_ref, mod_ref, gain_ref, wout_ref, wgu_ref, wd_ref, fgain_ref,
                       h_ref, act_ref, final)


def _kc_call(x1, attn, u, hist, mod, gain, poolw, pscale, wout, wgu, wd, fgain, final, name):
    r = x1.shape[0]
    return pl.pallas_call(
        functools.partial(_kc_kernel, final=final),
        out_shape=jax.ShapeDtypeStruct((r, D_MODEL), F32),
        grid=(1,),
        in_specs=[_full((r, D_MODEL)), _full((r, ATTN_WIDTH)), _full((r, POOL_WIDTH)),
                  _full((r, POOL_STATE * POOL_WIDTH)), _full((r, N_MOD * D_MODEL)), _full((3, D_MODEL)),
                  _full((len(POOL_SIZES), POOL_GROUP_WIDTH, POOL_GROUP_WIDTH)), _full((1, POOL_WIDTH)),
                  _resident((D_MODEL, D_MODEL)), _resident((D_MODEL, 2 * D_FF)), _resident((D_FF, D_MODEL)),
                  _full((1, D_MODEL))],
        out_specs=_full((r, D_MODEL)),
        scratch_shapes=[pltpu.VMEM((r, D_MODEL), BF16), pltpu.VMEM((r, D_MODEL), BF16),
                        pltpu.VMEM((r, D_FF), BF16)],
        compiler_params=pltpu.CompilerParams(dimension_semantics=("arbitrary",), vmem_limit_bytes=VMEM_LIMIT),
        name=name,
    )(x1, attn, u, hist, mod, gain, poolw, pscale, wout, wgu, wd, fgain)


def _interleave_gate_up(wg, wu):
    g = wg.reshape(D_MODEL, N_FF_CHUNKS, FF_CHUNK)
    u = wu.reshape(D_MODEL, N_FF_CHUNKS, FF_CHUNK)
    return jnp.concatenate([g, u], axis=2).reshape(D_MODEL, 2 * D_FF).astype(BF16)


def kernel(x_prompt, x_sample, c_prompt, c_sample, cache_k, cache_v, state_pool, w_ada, b_ada, norm_gain,
           w_in, w_out, sinks, rel_bias, pool_w, pool_scale, ffn1_wg, ffn1_wu, ffn1_wd, ffn2_wg, ffn2_wu,
           ffn2_wd, final_gain):
    n_p, seq, _ = x_prompt.shape
    n_s = x_sample.shape[0]
    perm = _q_perm()
    heads = _head_of_row()

    mod_all = _modulation(jnp.concatenate([c_prompt, c_sample], axis=0), w_ada, b_ada)
    bias = _bias_table(rel_bias)
    bias_s = bias.reshape(N_HEADS, WINDOW, 2 * WINDOW)[:, 0, :]
    fgain = final_gain.reshape(1, D_MODEL)

    xp = x_prompt
    xs = x_sample.reshape(1, n_s, D_MODEL)
    ck = cache_k.reshape(DEPTH, n_s, WINDOW, KV_WIDTH)
    cv = cache_v.reshape(DEPTH, n_s, WINDOW, KV_WIDTH)
    half_mask = jnp.asarray((np.arange(KV_WIDTH)[None, :] // HEAD_DIM == np.arange(2)[:, None]), BF16)

    new_kp, new_vp, new_pp, new_ks, new_vs, new_ps = [], [], [], [], [], []
    for l in range(DEPTH):
        final = l == DEPTH - 1
        wq = w_in[l][:, :ATTN_WIDTH][:, perm] * (HEAD_DIM ** -0.5)
        win = jnp.concatenate([wq, w_in[l][:, ATTN_WIDTH:]], axis=1).astype(BF16)
        wout = jnp.concatenate([w_out[l][:ATTN_WIDTH][perm], w_out[l][ATTN_WIDTH:]], axis=0).astype(BF16)
        wgu1 = _interleave_gate_up(ffn1_wg[l], ffn1_wu[l])
        wgu2 = _interleave_gate_up(ffn2_wg[l], ffn2_wu[l])
        wd1 = ffn1_wd[l].astype(BF16)
        wd2 = ffn2_wd[l].astype(BF16)
        poolw = pool_w[l].astype(BF16)
        pscale = pool_scale[l].reshape(1, POOL_WIDTH)
        gain = norm_gain[l]
        sink_rows = sinks[l][heads]
        sink_p = jnp.repeat(sink_rows, WINDOW).reshape(N_HEADS * WINDOW, 1)
        sink_s = sink_rows.reshape(N_HEADS, 1)
        mod_p = mod_all[l, :n_p].reshape(n_p, 1, N_MOD * D_MODEL)
        mod_s = mod_all[l, n_p:].reshape(1, n_s, N_MOD * D_MODEL)

        x1, q, k, v, u = _ka_call(xp, mod_p, gain, wgu1, wd1, win, TM_PROMPT, f"prompt_ffn1_inproj_l{l}")
        xp = _kb_call(x1, q, k, v, u, mod_p, gain, bias, sink_p, poolw, pscale, wout, wgu2, wd2, fgain,
                      final, f"prompt_mixer_ffn2_l{l}")
        new_kp.append(k[:, seq - WINDOW:].reshape(n_p, WINDOW, N_KV_HEADS, HEAD_DIM))
        new_vp.append(v[:, seq - WINDOW:].reshape(n_p, WINDOW, N_KV_HEADS, HEAD_DIM))
        new_pp.append(u[:, seq - POOL_STATE:])

        x1s, qs, ks, vs, us = _ka_call(xs, mod_s, gain, wgu1, wd1, win, n_s, f"sample_ffn1_inproj_l{l}")
        x1s, qs, ks, vs, us = x1s[0], qs[0], ks[0], vs[0], us[0]
        qrows = (qs.reshape(n_s, 4, 1, KV_WIDTH) * half_mask[None, None]).reshape(n_s, N_HEADS, KV_WIDTH)
        o = _ks_call(qrows, ks.reshape(n_s, 1, KV_WIDTH), vs.reshape(n_s, 1, KV_WIDTH), ck, cv, l, bias_s, sink_s)
        o5 = o.reshape(n_s, 4, 2, 2, HEAD_DIM)
        attn_s = jnp.stack([o5[:, :, 0, 0], o5[:, :, 1, 1]], axis=2).reshape(n_s, ATTN_WIDTH)
        hist = state_pool[l].reshape(n_s, POOL_STATE * POOL_WIDTH)
        xs = _kc_call(x1s, attn_s, us, hist, mod_s[0], gain, poolw, pscale, wout, wgu2, wd2, fgain,
                      final, f"sample_mixer_ffn2_l{l}").reshape(1, n_s, D_MODEL)
        new_ks.append(jnp.concatenate([ck[l][:, 1:], ks[:, None, :]], axis=1)
                      .reshape(n_s, WINDOW, N_KV_HEADS, HEAD_DIM))
        new_vs.append(jnp.concatenate([cv[l][:, 1:], vs[:, None, :]], axis=1)
                      .reshape(n_s, WINDOW, N_KV_HEADS, HEAD_DIM))
        new_ps.append(jnp.concatenate([state_pool[l][:, 1:], us[:, None, :]], axis=1))

    return (xp, xs.reshape(n_s, 1, D_MODEL), jnp.stack(new_kp), jnp.stack(new_vp), jnp.stack(new_pp),
            jnp.stack(new_ks), jnp.stack(new_vs), jnp.stack(new_ps))
```

```python
import functools
import math

import numpy as np
import jax
import jax.numpy as jnp
from jax import lax
from jax.experimental import pallas as pl
from jax.experimental.pallas import tpu as pltpu

D_MODEL = 1024
DEPTH = 2
HEAD_DIM = 64
N_HEADS = 8
N_KV_HEADS = 2
ATTN_WIDTH = N_HEADS * HEAD_DIM
KV_WIDTH = N_KV_HEADS * HEAD_DIM
WINDOW = 128
POOL_SIZES = (2, 4, 8, 16)
POOL_GROUP_WIDTH = 128
POOL_WIDTH = len(POOL_SIZES) * POOL_GROUP_WIDTH
POOL_STATE = max(POOL_SIZES) - 1
IN_WIDTH = ATTN_WIDTH + 2 * KV_WIDTH + POOL_WIDTH
D_FF = 2816
N_BUCKETS = 32
MAX_DISTANCE = 128
N_MOD = 9
MOD_WIDTH = N_MOD * D_MODEL
EPS = 1e-6

LANES = 128
SUBLANES = 8
MXU_COLS = 256
FF_CHUNK = 2 * MXU_COLS
FF_CHUNKS = tuple((c, min(FF_CHUNK, D_FF - c)) for c in range(0, D_FF, FF_CHUNK))
TM_PROMPT = 512
HIST = 16
VMEM_LIMIT = 56 * 1024 * 1024

F32 = jnp.float32
BF16 = jnp.bfloat16
NEG_INF = float("-inf")


def _q_perm():
    idx = np.empty(ATTN_WIDTH, np.int32)
    for c in range(4):
        for half in range(2):
            for d in range(HEAD_DIM):
                idx[c * LANES + half * HEAD_DIM + d] = (c + 4 * half) * HEAD_DIM + d
    return idx


def _head_of_row(r):
    return (r // 2) + 4 * (r % 2)


def _t5_bucket_np(dist):
    n = np.maximum(dist, 0)
    max_exact = N_BUCKETS // 2
    nf = np.maximum(n, 1).astype(np.float32)
    large = max_exact + (np.log(nf / max_exact) / math.log(MAX_DISTANCE / max_exact)
                         * (N_BUCKETS - max_exact)).astype(np.int32)
    large = np.minimum(large, N_BUCKETS - 1)
    return np.where(n < max_exact, n, large).astype(np.int32)


def _bucket_table():
    dist = np.arange(WINDOW)[:, None] + WINDOW - np.arange(2 * WINDOW)[None, :]
    return _t5_bucket_np(dist)


def _full(shape):
    return pl.BlockSpec(shape, lambda *_: (0,) * len(shape))


def _layer_resident(shape, layer):
    return pl.BlockSpec((None,) + tuple(shape), lambda *_: (layer,) + (0,) * len(shape),
                        pipeline_mode=pl.Buffered(1))


_SMEM = pl.BlockSpec(memory_space=pltpu.SMEM)


def _rms_mod(x, gain, shift, scale):
    y = x * lax.rsqrt(jnp.mean(x * x, axis=-1, keepdims=True) + EPS)
    return (y * gain) * (1.0 + scale) + shift


def _swiglu(h_ref, act_ref, wg_ref, wu_ref, wd_ref, side_work=()):
    for i, (c0, cw) in enumerate(FF_CHUNKS):
        g = jnp.dot(h_ref[...], wg_ref[:, c0:c0 + cw], preferred_element_type=F32)
        u = jnp.dot(h_ref[...], wu_ref[:, c0:c0 + cw], preferred_element_type=F32)
        act_ref[:, c0:c0 + cw] = ((g / (1.0 + jnp.exp(-g))) * u).astype(BF16)
        if i < len(side_work):
            side_work[i]()
    assert len(side_work) <= len(FF_CHUNKS)
    return jnp.dot(act_ref[...], wd_ref[...], preferred_element_type=F32)


def _mod_getter(mod_ref, row=None):
    rows = slice(None) if row is None else pl.ds(row, 1)
    return lambda k: mod_ref[rows, k * D_MODEL:(k + 1) * D_MODEL]


def _mod_kernel(c_ref, w_ref, b_ref, o_ref):
    c = c_ref[...]
    a = (c / (1.0 + jnp.exp(-c))).astype(BF16)
    o_ref[...] = jnp.dot(a, w_ref[...].astype(BF16), preferred_element_type=F32) + b_ref[...]


def _modulation(c_all, w_ada, b_ada):
    rows = c_all.shape[0]
    n_chunk = 1024
    return pl.pallas_call(
        _mod_kernel,
        out_shape=jax.ShapeDtypeStruct((DEPTH, rows, MOD_WIDTH), F32),
        grid=(DEPTH, MOD_WIDTH // n_chunk),
        in_specs=[
            pl.BlockSpec((rows, D_MODEL), lambda l, j: (0, 0)),
            pl.BlockSpec((None, D_MODEL, n_chunk), lambda l, j: (l, 0, j)),
            pl.BlockSpec((None, 1, n_chunk), lambda l, j: (l, 0, j)),
        ],
        out_specs=pl.BlockSpec((None, rows, n_chunk), lambda l, j: (l, 0, j)),
        compiler_params=pltpu.CompilerParams(
            dimension_semantics=("arbitrary", "arbitrary"), vmem_limit_bytes=VMEM_LIMIT),
        name="adaln_modulation",
    )(c_all, w_ada, b_ada.reshape(DEPTH, 1, MOD_WIDTH))


def _bias_kernel(rb_ref, bucket_ref, o_ref):
    bucket = bucket_ref[...]
    eq = [bucket == b for b in range(N_BUCKETS)]
    for r in range(N_HEADS):
        acc = jnp.zeros((WINDOW, 2 * WINDOW), F32)
        for b in range(N_BUCKETS):
            acc = jnp.where(eq[b], rb_ref[b, _head_of_row(r)], acc)
        o_ref[r * WINDOW:(r + 1) * WINDOW, :] = acc


def _bias_table(rel_bias):
    return pl.pallas_call(
        _bias_kernel,
        out_shape=jax.ShapeDtypeStruct((N_HEADS * WINDOW, 2 * WINDOW), F32),
        in_specs=[_SMEM, pl.BlockSpec(memory_space=pltpu.VMEM)],
        out_specs=pl.BlockSpec(memory_space=pltpu.VMEM),
        name="rel_bias_table",
    )(rel_bias, jnp.asarray(_bucket_table()))


def _ka_kernel(x_ref, mod_ref, gain_ref, wg_ref, wu_ref, wd_ref, win_ref,
               x1_ref, q_ref, k_ref, v_ref, u_ref, h_ref, act_ref, *, layer, per_row_mod):
    mod = _mod_getter(mod_ref, None if per_row_mod else pl.program_id(0))
    x = x_ref[...]
    h_ref[...] = _rms_mod(x, gain_ref[layer, 0:1, :], mod(0), mod(1)).astype(BF16)
    y = _swiglu(h_ref, act_ref, wg_ref, wu_ref, wd_ref)
    x1 = x + 0.5 * mod(2) * y
    x1_ref[...] = x1
    hm = _rms_mod(x1, gain_ref[layer, 1:2, :], mod(3), mod(4)).astype(BF16)
    z = jnp.dot(hm, win_ref[...], preferred_element_type=F32)
    q_ref[...] = z[:, :ATTN_WIDTH].astype(BF16)
    k_ref[...] = z[:, ATTN_WIDTH:ATTN_WIDTH + KV_WIDTH]
    v_ref[...] = z[:, ATTN_WIDTH + KV_WIDTH:ATTN_WIDTH + 2 * KV_WIDTH]
    u_ref[...] = z[:, ATTN_WIDTH + 2 * KV_WIDTH:]


def _ka_call(x, mod_all, mod_spec, per_row_mod, gain, wg, wu, wd, win, layer, tm, name):
    g, r, _ = x.shape
    row = lambda w: pl.BlockSpec((None, tm, w), lambda b, t: (b, t, 0))
    return pl.pallas_call(
        functools.partial(_ka_kernel, layer=layer, per_row_mod=per_row_mod),
        out_shape=(jax.ShapeDtypeStruct((g, r, D_MODEL), F32),
                   jax.ShapeDtypeStruct((g, r, ATTN_WIDTH), BF16),
                   jax.ShapeDtypeStruct((g, r, KV_WIDTH), F32),
                   jax.ShapeDtypeStruct((g, r, KV_WIDTH), F32),
                   jax.ShapeDtypeStruct((g, r, POOL_WIDTH), F32)),
        grid=(g, r // tm),
        in_specs=[row(D_MODEL), mod_spec, _full((DEPTH, 3, D_MODEL)),
                  _layer_resident((D_MODEL, D_FF), layer), _layer_resident((D_MODEL, D_FF), layer),
                  _layer_resident((D_FF, D_MODEL), layer), _layer_resident((D_MODEL, IN_WIDTH), layer)],
        out_specs=(row(D_MODEL), row(ATTN_WIDTH), row(KV_WIDTH), row(KV_WIDTH), row(POOL_WIDTH)),
        scratch_shapes=[pltpu.VMEM((tm, D_MODEL), BF16), pltpu.VMEM((tm, D_FF), BF16)],
        compiler_params=pltpu.CompilerParams(
            dimension_semantics=("arbitrary", "arbitrary"), vmem_limit_bytes=VMEM_LIMIT),
        name=name,
    )(x, mod_all, gain, wg, wu, wd, win)


def _attn_block(qb, kk, vv, bias_ref, sinks_ref, layer, valid):
    lo = lax.broadcasted_iota(jnp.int32, (WINDOW, LANES), 1) < HEAD_DIM
    zero = jnp.zeros((WINDOW, LANES), BF16)
    pieces = []
    for c in range(4):
        qc = qb[:, c * LANES:(c + 1) * LANES]
        pieces += [jnp.where(lo, qc, zero), jnp.where(lo, zero, qc)]
    qs = jnp.concatenate(pieces, axis=0)
    s = lax.dot_general(qs, kk, (((1,), (1,)), ((), ())), preferred_element_type=F32)
    ps = []
    for r in range(N_HEADS):
        rows = slice(r * WINDOW, (r + 1) * WINDOW)
        sr = jnp.where(valid, s[rows] + bias_ref[rows, :], NEG_INF)
        sink = sinks_ref[layer, _head_of_row(r)]
        m = jnp.maximum(jnp.max(sr, axis=-1, keepdims=True), sink)
        p = jnp.exp(sr - m)
        denom = jnp.sum(p, axis=-1, keepdims=True) + jnp.exp(sink - m)
        ps.append((p / denom).astype(BF16))
    o = jnp.dot(jnp.concatenate(ps, axis=0), vv, preferred_element_type=F32)
    cols = [jnp.where(lo, o[(2 * c) * WINDOW:(2 * c + 1) * WINDOW], o[(2 * c + 1) * WINDOW:(2 * c + 2) * WINDOW])
            for c in range(4)]
    return jnp.concatenate(cols, axis=1)


def _mix_prompt_steps(q_ref, k_ref, v_ref, kp_ref, vp_ref, u_ref, up_ref, bias_ref, sinks_ref, poolw_ref,
                      pscale_ref, mixed_ref, uext_ref, layer, seq_tile):
    tm = q_ref.shape[0]
    first = seq_tile == 0

    def attn_step(j):
        kk = (kp_ref[...] if j == 0 else k_ref[(j - 1) * WINDOW:j * WINDOW, :])
        vv = (vp_ref[...] if j == 0 else v_ref[(j - 1) * WINDOW:j * WINDOW, :])
        rows = slice(j * WINDOW, (j + 1) * WINDOW)
        kk = jnp.concatenate([kk, k_ref[rows, :]], axis=0).astype(BF16)
        vv = jnp.concatenate([vv, v_ref[rows, :]], axis=0).astype(BF16)
        qi = lax.broadcasted_iota(jnp.int32, (WINDOW, 2 * WINDOW), 0)
        kj = lax.broadcasted_iota(jnp.int32, (WINDOW, 2 * WINDOW), 1)
        dist = qi + WINDOW - kj
        valid = (dist >= 0) & (dist <= WINDOW)
        if j == 0:
            valid = valid & (kj >= jnp.where(first, WINDOW, 0))
        a = _attn_block(q_ref[rows, :], kk, vv, bias_ref, sinks_ref, layer, valid)
        mixed_ref[rows, 0:ATTN_WIDTH] = a.astype(BF16)

    def pool_step():
        uext_ref[0:HIST, :] = jnp.where(first, 0.0, up_ref[...])
        uext_ref[HIST:, :] = u_ref[...]
        pos = seq_tile * tm + lax.broadcasted_iota(jnp.int32, (tm, POOL_GROUP_WIDTH), 0)
        for g, w in enumerate(POOL_SIZES):
            cols = slice(g * POOL_GROUP_WIDTH, (g + 1) * POOL_GROUP_WIDTH)
            ug = uext_ref[HIST:, cols]
            acc = ug
            for d in range(1, w):
                acc = acc + uext_ref[HIST - d:HIST - d + tm, cols]
            count = jnp.minimum(pos + 1, w).astype(F32)
            pooled = (acc / count - ug).astype(BF16)
            pg = jnp.dot(pooled, poolw_ref[g], preferred_element_type=F32) * pscale_ref[layer:layer + 1, cols]
            mixed_ref[:, ATTN_WIDTH + g * POOL_GROUP_WIDTH:ATTN_WIDTH + (g + 1) * POOL_GROUP_WIDTH] = pg.astype(BF16)

    return [functools.partial(attn_step, j) for j in range(tm // WINDOW)] + [pool_step]


def _tail(x1, y, mod, gain_ref, wg_ref, wu_ref, wd_ref, fgain_ref, h_ref, act_ref, layer, final, side_work=()):
    x2 = x1 + mod(5) * y
    h_ref[...] = _rms_mod(x2, gain_ref[layer, 2:3, :], mod(6), mod(7)).astype(BF16)
    x3 = x2 + 0.5 * mod(8) * _swiglu(h_ref, act_ref, wg_ref, wu_ref, wd_ref, side_work)
    if final:
        x3 = x3 * lax.rsqrt(jnp.mean(x3 * x3, axis=-1, keepdims=True) + EPS) * fgain_ref[...]
    return x3


def _kb_kernel(sinks_ref, x1_ref, q_ref, k_ref, v_ref, kp_ref, vp_ref, u_ref, up_ref, mod_ref, gain_ref,
               bias_ref, poolw_ref, pscale_ref, wout_ref, wg_ref, wu_ref, wd_ref, fgain_ref,
               o_ref, mixed_ref, h_ref, act_ref, uext_ref, *, layer, final, tiles_per_seq):
    s = pl.program_id(0)
    n_tiles = pl.num_programs(0) - 1
    mix_tile = jnp.minimum(s, n_tiles - 1)
    tail_tile = jnp.maximum(s - 1, 0)

    @pl.when(s == 0)
    def _():
        mixed_ref[...] = jnp.zeros_like(mixed_ref)

    y = jnp.dot(mixed_ref[...], wout_ref[...], preferred_element_type=F32)
    mod = _mod_getter(mod_ref, tail_tile // tiles_per_seq)
    mix_steps = _mix_prompt_steps(q_ref, k_ref, v_ref, kp_ref, vp_ref, u_ref, up_ref, bias_ref, sinks_ref,
                                  poolw_ref, pscale_ref, mixed_ref, uext_ref, layer, mix_tile % tiles_per_seq)
    o_ref[...] = _tail(x1_ref[...], y, mod, gain_ref, wg_ref, wu_ref, wd_ref, fgain_ref, h_ref, act_ref,
                       layer, final, side_work=mix_steps)


def _kb_call(sinks, x1, q, k, v, u, mod_all, mod_spec, gain, bias, poolw, pscale, wout, wg, wu, wd, fgain,
             layer, final, name):
    g, r, _ = x1.shape
    tm = TM_PROMPT
    tps = r // tm
    n_tiles = g * tps
    nb = tm // WINDOW
    nh = tm // HIST
    mix = lambda s: jnp.minimum(s, n_tiles - 1)
    tail = lambda s: jnp.maximum(s - 1, 0)
    mix_row = lambda w: pl.BlockSpec((None, tm, w), lambda s: (mix(s) // tps, mix(s) % tps, 0))
    tail_row = lambda w: pl.BlockSpec((None, tm, w), lambda s: (tail(s) // tps, tail(s) % tps, 0))
    prev_kv = pl.BlockSpec((None, WINDOW, KV_WIDTH),
                           lambda s: (mix(s) // tps, jnp.maximum((mix(s) % tps) * nb - 1, 0), 0))
    prev_u = pl.BlockSpec((None, HIST, POOL_WIDTH),
                          lambda s: (mix(s) // tps, jnp.maximum((mix(s) % tps) * nh - 1, 0), 0))
    return pl.pallas_call(
        functools.partial(_kb_kernel, layer=layer, final=final, tiles_per_seq=tps),
        out_shape=jax.ShapeDtypeStruct((g, r, D_MODEL), F32),
        grid=(n_tiles + 1,),
        in_specs=[_SMEM, tail_row(D_MODEL), mix_row(ATTN_WIDTH), mix_row(KV_WIDTH), mix_row(KV_WIDTH),
                  prev_kv, prev_kv, mix_row(POOL_WIDTH), prev_u, mod_spec, _full((DEPTH, 3, D_MODEL)),
                  _full((N_HEADS * WINDOW, 2 * WINDOW)),
                  _layer_resident((len(POOL_SIZES), POOL_GROUP_WIDTH, POOL_GROUP_WIDTH), layer),
                  _full((DEPTH, POOL_WIDTH)),
                  _layer_resident((D_MODEL, D_MODEL), layer), _layer_resident((D_MODEL, D_FF), layer),
                  _layer_resident((D_MODEL, D_FF), layer), _layer_resident((D_FF, D_MODEL), layer),
                  _full((1, D_MODEL))],
        out_specs=tail_row(D_MODEL),
        scratch_shapes=[pltpu.VMEM((tm, D_MODEL), BF16), pltpu.VMEM((tm, D_MODEL), BF16),
                        pltpu.VMEM((tm, D_FF), BF16), pltpu.VMEM((HIST + tm, POOL_WIDTH), F32)],
        compiler_params=pltpu.CompilerParams(
            dimension_semantics=("arbitrary",), vmem_limit_bytes=VMEM_LIMIT),
        name=name,
    )(sinks, x1, q, k, v, k, v, u, u, mod_all, gain, bias, poolw, pscale, wout, wg, wu, wd, fgain)


def _ks_kernel(sinks_ref, qs_ref, kn_ref, vn_ref, ck_ref, cv_ref, bias_ref, o_ref, *, layer):
    bt = qs_ref.shape[0]
    bias_c = bias_ref[:, 0:WINDOW]
    bias_n = bias_ref[:, WINDOW:WINDOW + 1]
    head = lax.broadcasted_iota(jnp.int32, (N_HEADS, 1), 0)
    sink = jnp.zeros((N_HEADS, 1), F32)
    for r in range(N_HEADS):
        sink = jnp.where(head == r, sinks_ref[layer, _head_of_row(r)], sink)
    for b in range(bt):
        qs = qs_ref[b]
        kc = ck_ref[b].astype(BF16)
        vc = cv_ref[b].astype(BF16)
        kn = kn_ref[b].astype(BF16).astype(F32)
        vn = vn_ref[b].astype(BF16).astype(F32)
        s_c = lax.dot_general(qs, kc, (((1,), (1,)), ((), ())), preferred_element_type=F32) + bias_c
        s_n = jnp.sum(qs.astype(F32) * kn, axis=-1, keepdims=True) + bias_n
        m = jnp.maximum(jnp.maximum(jnp.max(s_c, axis=-1, keepdims=True), s_n), sink)
        p_c = jnp.exp(s_c - m)
        p_n = jnp.exp(s_n - m)
        denom = jnp.sum(p_c, axis=-1, keepdims=True) + p_n + jnp.exp(sink - m)
        o = jnp.dot((p_c / denom).astype(BF16), vc, preferred_element_type=F32)
        o_ref[b] = o + (p_n / denom).astype(BF16).astype(F32) * vn


def _ks_call(sinks, qs, kn, vn, cache_k, cache_v, layer, bias_s):
    nb = qs.shape[0]
    bt = 8
    per_b = lambda a, c: pl.BlockSpec((bt, a, c), lambda i: (i, 0, 0))
    cache = pl.BlockSpec((None, bt, WINDOW, KV_WIDTH), lambda i: (layer, i, 0, 0))
    return pl.pallas_call(
        functools.partial(_ks_kernel, layer=layer),
        out_shape=jax.ShapeDtypeStruct((nb, N_HEADS, KV_WIDTH), F32),
        grid=(nb // bt,),
        in_specs=[_SMEM, per_b(N_HEADS, KV_WIDTH), per_b(1, KV_WIDTH), per_b(1, KV_WIDTH), cache, cache,
                  _full((N_HEADS, 2 * WINDOW))],
        out_specs=per_b(N_HEADS, KV_WIDTH),
        compiler_params=pltpu.CompilerParams(dimension_semantics=("arbitrary",), vmem_limit_bytes=VMEM_LIMIT),
        name="sample_attention",
    )(sinks, qs, kn, vn, cache_k, cache_v, bias_s)


def _kc_kernel(x1_ref, attn_ref, u_ref, hist_ref, mod_ref, gain_ref, poolw_ref, pscale_ref,
               wout_ref, wg_ref, wu_ref, wd_ref, fgain_ref, o_ref, mixed_ref, h_ref, act_ref, *, layer, final):
    mixed_ref[:, 0:ATTN_WIDTH] = attn_ref[...].astype(BF16)
    for g, w in enumerate(POOL_SIZES):
        cols = slice(g * POOL_GROUP_WIDTH, (g + 1) * POOL_GROUP_WIDTH)
        ug = u_ref[:, cols]
        acc = ug
        for d in range(1, w):
            r = POOL_STATE - d
            acc = acc + hist_ref[:, r * POOL_WIDTH + g * POOL_GROUP_WIDTH:r * POOL_WIDTH + (g + 1) * POOL_GROUP_WIDTH]
        pooled = (acc / float(w) - ug).astype(BF16)
        pg = jnp.dot(pooled, poolw_ref[g], preferred_element_type=F32) * pscale_ref[layer:layer + 1, cols]
        mixed_ref[:, ATTN_WIDTH + g * POOL_GROUP_WIDTH:ATTN_WIDTH + (g + 1) * POOL_GROUP_WIDTH] = pg.astype(BF16)
    y = jnp.dot(mixed_ref[...], wout_ref[...], preferred_element_type=F32)
    o_ref[...] = _tail(x1_ref[...], y, _mod_getter(mod_ref), gain_ref, wg_ref, wu_ref, wd_ref, fgain_ref,
                       h_ref, act_ref, layer, final)


def _kc_call(x1, attn, u, hist, mod_all, mod_spec, gain, poolw, pscale, wout, wg, wu, wd, fgain, layer, final, name):
    r = x1.shape[0]
    return pl.pallas_call(
        functools.partial(_kc_kernel, layer=layer, final=final),
        out_shape=jax.ShapeDtypeStruct((r, D_MODEL), F32),
        grid=(1,),
        in_specs=[_full((r, D_MODEL)), _full((r, ATTN_WIDTH)), _full((r, POOL_WIDTH)),
                  _full((r, POOL_STATE * POOL_WIDTH)), mod_spec, _full((DEPTH, 3, D_MODEL)),
                  _layer_resident((len(POOL_SIZES), POOL_GROUP_WIDTH, POOL_GROUP_WIDTH), layer),
                  _full((DEPTH, POOL_WIDTH)),
                  _layer_resident((D_MODEL, D_MODEL), layer), _layer_resident((D_MODEL, D_FF), layer),
                  _layer_resident((D_MODEL, D_FF), layer), _layer_resident((D_FF, D_MODEL), layer),
                  _full((1, D_MODEL))],
        out_specs=_full((r, D_MODEL)),
        scratch_shapes=[pltpu.VMEM((r, D_MODEL), BF16), pltpu.VMEM((r, D_MODEL), BF16),
                        pltpu.VMEM((r, D_FF), BF16)],
        compiler_params=pltpu.CompilerParams(dimension_semantics=("arbitrary",), vmem_limit_bytes=VMEM_LIMIT),
        name=name,
    )(x1, attn, u, hist, mod_all, gain, poolw, pscale, wout, wg, wu, wd, fgain)


def kernel(x_prompt, x_sample, c_prompt, c_sample, cache_k, cache_v, state_pool, w_ada, b_ada, norm_gain,
           w_in, w_out, sinks, rel_bias, pool_w, pool_scale, ffn1_wg, ffn1_wu, ffn1_wd, ffn2_wg, ffn2_wu,
           ffn2_wd, final_gain):
    n_p, seq, _ = x_prompt.shape
    n_s = x_sample.shape[0]
    perm = _q_perm()

    mod_all = _modulation(jnp.concatenate([c_sample, c_prompt], axis=0), w_ada, b_ada)
    bias = _bias_table(rel_bias)
    bias_s = bias.reshape(N_HEADS, WINDOW, 2 * WINDOW)[:, 0, :]
    fgain = final_gain.reshape(1, D_MODEL)

    win = jnp.concatenate([w_in[:, :, :ATTN_WIDTH][:, :, perm] * (HEAD_DIM ** -0.5), w_in[:, :, ATTN_WIDTH:]],
                          axis=2).astype(BF16)
    wout = jnp.concatenate([w_out[:, :ATTN_WIDTH][:, perm], w_out[:, ATTN_WIDTH:]], axis=1).astype(BF16)
    wg1, wu1, wd1 = ffn1_wg.astype(BF16), ffn1_wu.astype(BF16), ffn1_wd.astype(BF16)
    wg2, wu2, wd2 = ffn2_wg.astype(BF16), ffn2_wu.astype(BF16), ffn2_wd.astype(BF16)
    poolw = pool_w.astype(BF16)

    xp = x_prompt
    xs = x_sample.reshape(1, n_s, D_MODEL)
    ck = cache_k.reshape(DEPTH, n_s, WINDOW, KV_WIDTH)
    cv = cache_v.reshape(DEPTH, n_s, WINDOW, KV_WIDTH)
    half_mask = jnp.asarray((np.arange(KV_WIDTH)[None, :] // HEAD_DIM == np.arange(2)[:, None]), BF16)

    new_kp, new_vp, new_pp, new_ks, new_vs, new_ps = [], [], [], [], [], []
    for l in range(DEPTH):
        final = l == DEPTH - 1
        mod_p = pl.BlockSpec((None, n_p, MOD_WIDTH), lambda *_, l=l: (l, n_s // n_p, 0))
        mod_s = pl.BlockSpec((None, n_s, MOD_WIDTH), lambda *_, l=l: (l, 0, 0))

        x1, q, k, v, u = _ka_call(xp, mod_all, mod_p, False, norm_gain, wg1, wu1, wd1, win, l, TM_PROMPT,
                                  f"prompt_ffn1_inproj_l{l}")
        xp = _kb_call(sinks, x1, q, k, v, u, mod_all, mod_p, norm_gain, bias, poolw, pool_scale, wout,
                      wg2, wu2, wd2, fgain, l, final, f"prompt_mixer_ffn2_l{l}")
        new_kp.append(k[:, seq - WINDOW:].reshape(n_p, WINDOW, N_KV_HEADS, HEAD_DIM))
        new_vp.append(v[:, seq - WINDOW:].reshape(n_p, WINDOW, N_KV_HEADS, HEAD_DIM))
        new_pp.append(u[:, seq - POOL_STATE:])

        x1s, qs, ks, vs, us = _ka_call(xs, mod_all, mod_s, True, norm_gain, wg1, wu1, wd1, win, l, n_s,
                                       f"sample_ffn1_inproj_l{l}")
        x1s, qs, ks, vs, us = x1s[0], qs[0], ks[0], vs[0], us[0]
        qrows = (qs.reshape(n_s, 4, 1, KV_WIDTH) * half_mask[None, None]).reshape(n_s, N_HEADS, KV_WIDTH)
        o = _ks_call(sinks, qrows, ks.reshape(n_s, 1, KV_WIDTH), vs.reshape(n_s, 1, KV_WIDTH), ck, cv, l, bias_s)
        o5 = o.reshape(n_s, 4, 2, 2, HEAD_DIM)
        attn_s = jnp.stack([o5[:, :, 0, 0], o5[:, :, 1, 1]], axis=2).reshape(n_s, ATTN_WIDTH)
        hist = state_pool[l].reshape(n_s, POOL_STATE * POOL_WIDTH)
        xs = _kc_call(x1s, attn_s, us, hist, mod_all, mod_s, norm_gain, poolw, pool_scale, wout, wg2, wu2, wd2,
                      fgain, l, final, f"sample_mixer_ffn2_l{l}").reshape(1, n_s, D_MODEL)
        new_ks.append(jnp.concatenate([ck[l][:, 1:], ks[:, None, :]], axis=1)
                      .reshape(n_s, WINDOW, N_KV_HEADS, HEAD_DIM))
        new_vs.append(jnp.concatenate([cv[l][:, 1:], vs[:, None, :]], axis=1)
                      .reshape(n_s, WINDOW, N_KV_HEADS, HEAD_DIM))
        new_ps.append(jnp.concatenate([state_pool[l][:, 1:], us[:, None, :]], axis=1))

    return (xp, xs.reshape(n_s, 1, D_MODEL), jnp.stack(new_kp), jnp.stack(new_vp), jnp.stack(new_pp),
            jnp.stack(new_ks), jnp.stack(new_vs), jnp.stack(new_ps))
```

```python
import functools
import math

import numpy as np
import jax
import jax.numpy as jnp
from jax import lax
from jax.experimental import pallas as pl
from jax.experimental.pallas import tpu as pltpu

D_MODEL = 1024
DEPTH = 2
HEAD_DIM = 64
N_HEADS = 8
N_KV_HEADS = 2
ATTN_WIDTH = N_HEADS * HEAD_DIM
KV_WIDTH = N_KV_HEADS * HEAD_DIM
WINDOW = 128
POOL_SIZES = (2, 4, 8, 16)
POOL_GROUP_WIDTH = 128
POOL_WIDTH = len(POOL_SIZES) * POOL_GROUP_WIDTH
POOL_STATE = max(POOL_SIZES) - 1
IN_WIDTH = ATTN_WIDTH + 2 * KV_WIDTH + POOL_WIDTH
D_FF = 2816
N_BUCKETS = 32
MAX_DISTANCE = 128
N_MOD = 9
MOD_WIDTH = N_MOD * D_MODEL
EPS = 1e-6

LANES = 128
SUBLANES = 8
MXU_COLS = 256
FF_CHUNK = 2 * MXU_COLS
FF_CHUNKS = tuple((c, min(FF_CHUNK, D_FF - c)) for c in range(0, D_FF, FF_CHUNK))
TM_PROMPT = 512
HIST = 16
VMEM_LIMIT = 56 * 1024 * 1024

F32 = jnp.float32
BF16 = jnp.bfloat16
NEG_INF = float("-inf")


def _head_of_row(r):
    return (r // 2) + 4 * (r % 2)


def _t5_bucket_np(dist):
    n = np.maximum(dist, 0)
    max_exact = N_BUCKETS // 2
    nf = np.maximum(n, 1).astype(np.float32)
    large = max_exact + (np.log(nf / max_exact) / math.log(MAX_DISTANCE / max_exact)
                         * (N_BUCKETS - max_exact)).astype(np.int32)
    large = np.minimum(large, N_BUCKETS - 1)
    return np.where(n < max_exact, n, large).astype(np.int32)


def _bucket_table():
    dist = np.arange(WINDOW)[:, None] + WINDOW - np.arange(2 * WINDOW)[None, :]
    return _t5_bucket_np(dist)


def _full(shape):
    return pl.BlockSpec(shape, lambda *_: (0,) * len(shape))


def _layer_resident(shape, layer):
    return pl.BlockSpec((None,) + tuple(shape), lambda *_: (layer,) + (0,) * len(shape),
                        pipeline_mode=pl.Buffered(1))


_SMEM = pl.BlockSpec(memory_space=pltpu.SMEM)


def _rms_mod(x, gain, shift, scale):
    y = x * lax.rsqrt(jnp.mean(x * x, axis=-1, keepdims=True) + EPS)
    return (y * gain) * (1.0 + scale) + shift


def _swiglu(h_ref, act_ref, wg_ref, wu_ref, wd_ref, side_work=(), before_down=None):
    assert len(side_work) <= len(FF_CHUNKS)
    for i, (c0, cw) in enumerate(FF_CHUNKS):
        g = jnp.dot(h_ref[...], wg_ref[:, c0:c0 + cw], preferred_element_type=F32)
        u = jnp.dot(h_ref[...], wu_ref[:, c0:c0 + cw], preferred_element_type=F32)
        act_ref[:, c0:c0 + cw] = ((g / (1.0 + jnp.exp(-g))) * u).astype(BF16)
        if i < len(side_work):
            side_work[i]()
    if before_down is not None:
        before_down()
    return jnp.dot(act_ref[...], wd_ref[...], preferred_element_type=F32)


def _mod_getter(mod_ref, row=None):
    rows = slice(None) if row is None else pl.ds(row, 1)
    return lambda k: mod_ref[rows, k * D_MODEL:(k + 1) * D_MODEL]


def _mod_kernel(c_ref, w_ref, b_ref, o_ref):
    c = c_ref[...]
    a = (c / (1.0 + jnp.exp(-c))).astype(BF16)
    o_ref[...] = jnp.dot(a, w_ref[...].astype(BF16), preferred_element_type=F32) + b_ref[...]


def _modulation(c_all, w_ada, b_ada):
    rows = c_all.shape[0]
    n_chunk = 1024
    return pl.pallas_call(
        _mod_kernel,
        out_shape=jax.ShapeDtypeStruct((DEPTH, rows, MOD_WIDTH), F32),
        grid=(DEPTH, MOD_WIDTH // n_chunk),
        in_specs=[
            pl.BlockSpec((rows, D_MODEL), lambda l, j: (0, 0)),
            pl.BlockSpec((None, D_MODEL, n_chunk), lambda l, j: (l, 0, j)),
            pl.BlockSpec((None, 1, n_chunk), lambda l, j: (l, 0, j)),
        ],
        out_specs=pl.BlockSpec((None, rows, n_chunk), lambda l, j: (l, 0, j)),
        compiler_params=pltpu.CompilerParams(
            dimension_semantics=("arbitrary", "arbitrary"), vmem_limit_bytes=VMEM_LIMIT),
        name="adaln_modulation",
    )(c_all, w_ada, b_ada.reshape(DEPTH, 1, MOD_WIDTH))


def _bias_kernel(rb_ref, bucket_ref, o_ref):
    bucket = bucket_ref[...]
    eq = [bucket == b for b in range(N_BUCKETS)]
    for r in range(N_HEADS):
        acc = jnp.zeros((WINDOW, 2 * WINDOW), F32)
        for b in range(N_BUCKETS):
            acc = jnp.where(eq[b], rb_ref[b, _head_of_row(r)], acc)
        o_ref[r * WINDOW:(r + 1) * WINDOW, :] = acc


def _bias_table(rel_bias):
    return pl.pallas_call(
        _bias_kernel,
        out_shape=jax.ShapeDtypeStruct((N_HEADS * WINDOW, 2 * WINDOW), F32),
        in_specs=[_SMEM, pl.BlockSpec(memory_space=pltpu.VMEM)],
        out_specs=pl.BlockSpec(memory_space=pltpu.VMEM),
        name="rel_bias_table",
    )(rel_bias, jnp.asarray(_bucket_table()))


def _project_q(hm, win_ref, q_ref):
    q_ref[...] = jnp.dot(hm, win_ref[:, :ATTN_WIDTH], preferred_element_type=F32).astype(BF16)


def _project_kvu(hm, win_ref, k_ref, v_ref, u_ref):
    z = jnp.dot(hm, win_ref[:, ATTN_WIDTH:], preferred_element_type=F32)
    k_ref[...] = z[:, :KV_WIDTH]
    v_ref[...] = z[:, KV_WIDTH:2 * KV_WIDTH]
    u_ref[...] = z[:, 2 * KV_WIDTH:]
    return z


def _ffn1(x_ref, mod, gain_ref, wg_ref, wu_ref, wd_ref, x1_ref, h_ref, act_ref, layer, before_norm=None):
    x = x_ref[...]
    h_ref[...] = _rms_mod(x, gain_ref[layer, 0:1, :], mod(0), mod(1)).astype(BF16)
    x1 = x + 0.5 * mod(2) * _swiglu(h_ref, act_ref, wg_ref, wu_ref, wd_ref)
    x1_ref[...] = x1
    if before_norm is not None:
        before_norm()
    return _rms_mod(x1, gain_ref[layer, 1:2, :], mod(3), mod(4)).astype(BF16)


def _ka_sample_kernel(x_ref, mod_ref, gain_ref, wg_ref, wu_ref, wd_ref, win_ref,
                      x1_ref, q_ref, k_ref, v_ref, u_ref, h_ref, act_ref, *, layer):
    hm = _ffn1(x_ref, _mod_getter(mod_ref), gain_ref, wg_ref, wu_ref, wd_ref, x1_ref, h_ref, act_ref, layer)
    _project_q(hm, win_ref, q_ref)
    _project_kvu(hm, win_ref, k_ref, v_ref, u_ref)


def _ka_prompt_kernel(x_ref, mod_ref, gain_ref, wg_ref, wu_ref, wd_ref, win_ref,
                      x1_ref, q_ref, k_ref, v_ref, u_ref, kl_ref, vl_ref, ul_ref,
                      h_ref, act_ref, hm_ref, *, layer, tiles_per_seq):
    tm = x_ref.shape[0]
    s = pl.program_id(0)
    main = jnp.minimum(s, pl.num_programs(0) - 2)

    @pl.when(s == 0)
    def _():
        hm_ref[...] = jnp.zeros_like(hm_ref)

    _project_q(hm_ref[...], win_ref, q_ref)

    def project_rest():
        z = _project_kvu(hm_ref[...], win_ref, k_ref, v_ref, u_ref)
        kl_ref[...] = z[tm - WINDOW:, :KV_WIDTH]
        vl_ref[...] = z[tm - WINDOW:, KV_WIDTH:2 * KV_WIDTH]
        ul_ref[...] = z[tm - HIST:, 2 * KV_WIDTH:]

    mod = _mod_getter(mod_ref, main // tiles_per_seq)
    hm = _ffn1(x_ref, mod, gain_ref, wg_ref, wu_ref, wd_ref, x1_ref, h_ref, act_ref, layer,
               before_norm=project_rest)
    hm_ref[...] = hm


def _ka_out_shapes(g, r):
    return (jax.ShapeDtypeStruct((g, r, D_MODEL), F32),
            jax.ShapeDtypeStruct((g, r, ATTN_WIDTH), BF16),
            jax.ShapeDtypeStruct((g, r, KV_WIDTH), F32),
            jax.ShapeDtypeStruct((g, r, KV_WIDTH), F32),
            jax.ShapeDtypeStruct((g, r, POOL_WIDTH), F32))


def _ka_weight_specs(layer):
    return [_full((DEPTH, 3, D_MODEL)),
            _layer_resident((D_MODEL, D_FF), layer), _layer_resident((D_MODEL, D_FF), layer),
            _layer_resident((D_FF, D_MODEL), layer), _layer_resident((D_MODEL, IN_WIDTH), layer)]


def _ka_sample_call(x, mod_all, mod_spec, gain, wg, wu, wd, win, layer, name):
    r = x.shape[0]
    row = lambda w: _full((r, w))
    return pl.pallas_call(
        functools.partial(_ka_sample_kernel, layer=layer),
        out_shape=tuple(jax.ShapeDtypeStruct(s.shape[1:], s.dtype) for s in _ka_out_shapes(1, r)),
        grid=(1,),
        in_specs=[row(D_MODEL), mod_spec] + _ka_weight_specs(layer),
        out_specs=(row(D_MODEL), row(ATTN_WIDTH), row(KV_WIDTH), row(KV_WIDTH), row(POOL_WIDTH)),
        scratch_shapes=[pltpu.VMEM((r, D_MODEL), BF16), pltpu.VMEM((r, D_FF), BF16)],
        compiler_params=pltpu.CompilerParams(dimension_semantics=("arbitrary",), vmem_limit_bytes=VMEM_LIMIT),
        name=name,
    )(x, mod_all, gain, wg, wu, wd, win)


def _ka_prompt_call(x, mod_all, mod_spec, gain, wg, wu, wd, win, layer, name):
    g, r, _ = x.shape
    tm = TM_PROMPT
    tps = r // tm
    n_tiles = g * tps
    main = lambda s: jnp.minimum(s, n_tiles - 1)
    prev = lambda s: jnp.maximum(s - 1, 0)
    main_row = lambda w: pl.BlockSpec((None, tm, w), lambda s: (main(s) // tps, main(s) % tps, 0))
    prev_row = lambda w: pl.BlockSpec((None, tm, w), lambda s: (prev(s) // tps, prev(s) % tps, 0))
    last = lambda rows, w: pl.BlockSpec((None, rows, w), lambda s: (prev(s) // tps, 0, 0))
    return pl.pallas_call(
        functools.partial(_ka_prompt_kernel, layer=layer, tiles_per_seq=tps),
        out_shape=_ka_out_shapes(g, r) + (jax.ShapeDtypeStruct((g, WINDOW, KV_WIDTH), F32),
                                          jax.ShapeDtypeStruct((g, WINDOW, KV_WIDTH), F32),
                                          jax.ShapeDtypeStruct((g, HIST, POOL_WIDTH), F32)),
        grid=(n_tiles + 1,),
        in_specs=[main_row(D_MODEL), mod_spec] + _ka_weight_specs(layer),
        out_specs=(main_row(D_MODEL), prev_row(ATTN_WIDTH), prev_row(KV_WIDTH), prev_row(KV_WIDTH),
                   prev_row(POOL_WIDTH), last(WINDOW, KV_WIDTH), last(WINDOW, KV_WIDTH), last(HIST, POOL_WIDTH)),
        scratch_shapes=[pltpu.VMEM((tm, D_MODEL), BF16), pltpu.VMEM((tm, D_FF), BF16),
                        pltpu.VMEM((tm, D_MODEL), BF16)],
        compiler_params=pltpu.CompilerParams(dimension_semantics=("arbitrary",), vmem_limit_bytes=VMEM_LIMIT),
        name=name,
    )(x, mod_all, gain, wg, wu, wd, win)


def _low_half():
    return lax.broadcasted_iota(jnp.int32, (WINDOW, LANES), 1) < HEAD_DIM


def _attn_probs(qb, kk, bias_ref, sinks_ref, layer, valid, p_ref):
    lo = _low_half()
    zero = jnp.zeros((WINDOW, LANES), BF16)
    pieces = []
    for c in range(4):
        qc = qb[:, c * LANES:(c + 1) * LANES]
        pieces += [jnp.where(lo, qc, zero), jnp.where(lo, zero, qc)]
    qs = jnp.concatenate(pieces, axis=0)
    s = lax.dot_general(qs, kk, (((1,), (1,)), ((), ())), preferred_element_type=F32)
    for r in range(N_HEADS):
        rows = slice(r * WINDOW, (r + 1) * WINDOW)
        sr = jnp.where(valid, s[rows] + bias_ref[rows, :], NEG_INF)
        sink = sinks_ref[layer, _head_of_row(r)]
        m = jnp.maximum(jnp.max(sr, axis=-1, keepdims=True), sink)
        p = jnp.exp(sr - m)
        denom = jnp.sum(p, axis=-1, keepdims=True) + jnp.exp(sink - m)
        p_ref[rows, :] = (p / denom).astype(BF16)


def _attn_values(p_ref, vv):
    lo = _low_half()
    o = jnp.dot(p_ref[...], vv, preferred_element_type=F32)
    cols = [jnp.where(lo, o[(2 * c) * WINDOW:(2 * c + 1) * WINDOW], o[(2 * c + 1) * WINDOW:(2 * c + 2) * WINDOW])
            for c in range(4)]
    return jnp.concatenate(cols, axis=1)


def _mix_prompt_steps(q_ref, k_ref, v_ref, kp_ref, vp_ref, u_ref, up_ref, bias_ref, sinks_ref, poolw_ref,
                      pscale_ref, mixed_ref, uext_ref, pooled_ref, p_ref, layer, seq_tile):
    tm = q_ref.shape[0]
    nb = tm // WINDOW
    first = seq_tile == 0

    def keys_values(ref, prev_ref, j):
        prev = prev_ref[...] if j == 0 else ref[(j - 1) * WINDOW:j * WINDOW, :]
        return jnp.concatenate([prev, ref[j * WINDOW:(j + 1) * WINDOW, :]], axis=0).astype(BF16)

    def probs(j):
        qi = lax.broadcasted_iota(jnp.int32, (WINDOW, 2 * WINDOW), 0)
        kj = lax.broadcasted_iota(jnp.int32, (WINDOW, 2 * WINDOW), 1)
        dist = qi + WINDOW - kj
        valid = (dist >= 0) & (dist <= WINDOW)
        if j == 0:
            valid = valid & (kj >= jnp.where(first, WINDOW, 0))
        _attn_probs(q_ref[j * WINDOW:(j + 1) * WINDOW, :], keys_values(k_ref, kp_ref, j), bias_ref, sinks_ref,
                    layer, valid, p_ref)

    def values(j):
        a = _attn_values(p_ref, keys_values(v_ref, vp_ref, j))
        mixed_ref[j * WINDOW:(j + 1) * WINDOW, 0:ATTN_WIDTH] = a.astype(BF16)

    def pool_history():
        uext_ref[0:HIST, :] = jnp.where(first, 0.0, up_ref[...])
        uext_ref[HIST:, :] = u_ref[...]

    def pool_sum(g):
        w = POOL_SIZES[g]
        cols = slice(g * POOL_GROUP_WIDTH, (g + 1) * POOL_GROUP_WIDTH)
        ug = uext_ref[HIST:, cols]
        acc = ug
        for d in range(1, w):
            acc = acc + uext_ref[HIST - d:HIST - d + tm, cols]
        pos = seq_tile * tm + lax.broadcasted_iota(jnp.int32, (tm, POOL_GROUP_WIDTH), 0)
        count = jnp.minimum(pos + 1, w).astype(F32)
        pooled_ref[:, cols] = (acc / count - ug).astype(BF16)

    def pool_map(g):
        cols = slice(g * POOL_GROUP_WIDTH, (g + 1) * POOL_GROUP_WIDTH)
        pg = jnp.dot(pooled_ref[:, cols], poolw_ref[g], preferred_element_type=F32) * pscale_ref[layer:layer + 1, cols]
        mixed_ref[:, ATTN_WIDTH + g * POOL_GROUP_WIDTH:ATTN_WIDTH + (g + 1) * POOL_GROUP_WIDTH] = pg.astype(BF16)

    assert nb == len(POOL_SIZES)

    def stage(i):
        def run():
            if i > 0:
                values(i - 1)
                pool_map(i - 1)
            if i < nb:
                probs(i)
                pool_sum(i)
        return run

    return pool_history, [stage(i) for i in range(nb + 1)]


def _out_projection(x1, mixed_ref, wout_ref, mod, gain_ref, layer):
    x2 = x1 + mod(5) * jnp.dot(mixed_ref[...], wout_ref[...], preferred_element_type=F32)
    return x2, _rms_mod(x2, gain_ref[layer, 2:3, :], mod(6), mod(7)).astype(BF16)


def _ffn2_out(x2, y, mod, fgain_ref, final):
    x3 = x2 + 0.5 * mod(8) * y
    if final:
        x3 = x3 * lax.rsqrt(jnp.mean(x3 * x3, axis=-1, keepdims=True) + EPS) * fgain_ref[...]
    return x3


def _kb_kernel(sinks_ref, x1_ref, q_ref, k_ref, v_ref, kp_ref, vp_ref, u_ref, up_ref, mod_ref, gain_ref,
               bias_ref, poolw_ref, pscale_ref, wout_ref, wg_ref, wu_ref, wd_ref, fgain_ref,
               o_ref, mixed_ref, h_ref, x2_ref, act_ref, uext_ref, pooled_ref, p_ref,
               *, layer, final, tiles_per_seq):
    s = pl.program_id(0)
    n_tiles = pl.num_programs(0) - 1
    mix_tile = jnp.minimum(s, n_tiles - 1)
    ffn_tile = jnp.maximum(s - 1, 0)
    slot = s % 2

    @pl.when(s == 0)
    def _():
        h_ref[...] = jnp.zeros_like(h_ref)
        x2_ref[1] = jnp.zeros(x2_ref.shape[1:], F32)

    mix_prologue, mix_steps = _mix_prompt_steps(
        q_ref, k_ref, v_ref, kp_ref, vp_ref, u_ref, up_ref, bias_ref, sinks_ref, poolw_ref, pscale_ref,
        mixed_ref, uext_ref, pooled_ref, p_ref, layer, mix_tile % tiles_per_seq)
    mix_prologue()

    def finish_mix():
        x2, h = _out_projection(x1_ref[...], mixed_ref, wout_ref, _mod_getter(mod_ref, mix_tile // tiles_per_seq),
                                gain_ref, layer)
        x2_ref[slot] = x2
        h_ref[...] = h

    y = _swiglu(h_ref, act_ref, wg_ref, wu_ref, wd_ref, side_work=mix_steps, before_down=finish_mix)
    o_ref[...] = _ffn2_out(x2_ref[1 - slot], y, _mod_getter(mod_ref, ffn_tile // tiles_per_seq), fgain_ref, final)


def _kb_call(sinks, x1, q, k, v, u, mod_all, mod_spec, gain, bias, poolw, pscale, wout, wg, wu, wd, fgain,
             layer, final, name):
    g, r, _ = x1.shape
    tm = TM_PROMPT
    tps = r // tm
    n_tiles = g * tps
    nb = tm // WINDOW
    nh = tm // HIST
    mix = lambda s: jnp.minimum(s, n_tiles - 1)
    tail = lambda s: jnp.maximum(s - 1, 0)
    mix_row = lambda w: pl.BlockSpec((None, tm, w), lambda s: (mix(s) // tps, mix(s) % tps, 0))
    tail_row = lambda w: pl.BlockSpec((None, tm, w), lambda s: (tail(s) // tps, tail(s) % tps, 0))
    prev_kv = pl.BlockSpec((None, WINDOW, KV_WIDTH),
                           lambda s: (mix(s) // tps, jnp.maximum((mix(s) % tps) * nb - 1, 0), 0))
    prev_u = pl.BlockSpec((None, HIST, POOL_WIDTH),
                          lambda s: (mix(s) // tps, jnp.maximum((mix(s) % tps) * nh - 1, 0), 0))
    return pl.pallas_call(
        functools.partial(_kb_kernel, layer=layer, final=final, tiles_per_seq=tps),
        out_shape=jax.ShapeDtypeStruct((g, r, D_MODEL), F32),
        grid=(n_tiles + 1,),
        in_specs=[_SMEM, mix_row(D_MODEL), mix_row(ATTN_WIDTH), mix_row(KV_WIDTH), mix_row(KV_WIDTH),
                  prev_kv, prev_kv, mix_row(POOL_WIDTH), prev_u, mod_spec, _full((DEPTH, 3, D_MODEL)),
                  _full((N_HEADS * WINDOW, 2 * WINDOW)),
                  _layer_resident((len(POOL_SIZES), POOL_GROUP_WIDTH, POOL_GROUP_WIDTH), layer),
                  _full((DEPTH, POOL_WIDTH)),
                  _layer_resident((D_MODEL, D_MODEL), layer), _layer_resident((D_MODEL, D_FF), layer),
                  _layer_resident((D_MODEL, D_FF), layer), _layer_resident((D_FF, D_MODEL), layer),
                  _full((1, D_MODEL))],
        out_specs=tail_row(D_MODEL),
        scratch_shapes=[pltpu.VMEM((tm, D_MODEL), BF16), pltpu.VMEM((tm, D_MODEL), BF16),
                        pltpu.VMEM((2, tm, D_MODEL), F32), pltpu.VMEM((tm, D_FF), BF16),
                        pltpu.VMEM((HIST + tm, POOL_WIDTH), F32), pltpu.VMEM((tm, POOL_WIDTH), BF16),
                        pltpu.VMEM((N_HEADS * WINDOW, 2 * WINDOW), BF16)],
        compiler_params=pltpu.CompilerParams(
            dimension_semantics=("arbitrary",), vmem_limit_bytes=VMEM_LIMIT),
        name=name,
    )(sinks, x1, q, k, v, k, v, u, u, mod_all, gain, bias, poolw, pscale, wout, wg, wu, wd, fgain)


def _ks_kernel(sinks_ref, qs_ref, kn_ref, vn_ref, ck_ref, cv_ref, bias_ref, o_ref, *, layer):
    bt = qs_ref.shape[0]
    bias_c = bias_ref[:, 0:WINDOW]
    bias_n = bias_ref[:, WINDOW:WINDOW + 1]
    head = lax.broadcasted_iota(jnp.int32, (N_HEADS, 1), 0)
    sink = jnp.zeros((N_HEADS, 1), F32)
    for r in range(N_HEADS):
        sink = jnp.where(head == r, sinks_ref[layer, _head_of_row(r)], sink)
    for b in range(bt):
        qs = qs_ref[b]
        kc = ck_ref[b].astype(BF16)
        vc = cv_ref[b].astype(BF16)
        kn = kn_ref[b].astype(BF16).astype(F32)
        vn = vn_ref[b].astype(BF16).astype(F32)
        s_c = lax.dot_general(qs, kc, (((1,), (1,)), ((), ())), preferred_element_type=F32) + bias_c
        s_n = jnp.sum(qs.astype(F32) * kn, axis=-1, keepdims=True) + bias_n
        m = jnp.maximum(jnp.maximum(jnp.max(s_c, axis=-1, keepdims=True), s_n), sink)
        p_c = jnp.exp(s_c - m)
        p_n = jnp.exp(s_n - m)
        denom = jnp.sum(p_c, axis=-1, keepdims=True) + p_n + jnp.exp(sink - m)
        o = jnp.dot((p_c / denom).astype(BF16), vc, preferred_element_type=F32)
        o_ref[b] = o + (p_n / denom).astype(BF16).astype(F32) * vn


def _ks_call(sinks, qs, kn, vn, cache_k, cache_v, layer, bias_s):
    nb = qs.shape[0]
    bt = 8
    per_b = lambda a, c: pl.BlockSpec((bt, a, c), lambda i: (i, 0, 0))
    cache = pl.BlockSpec((None, bt, WINDOW, KV_WIDTH), lambda i: (layer, i, 0, 0))
    return pl.pallas_call(
        functools.partial(_ks_kernel, layer=layer),
        out_shape=jax.ShapeDtypeStruct((nb, N_HEADS, KV_WIDTH), F32),
        grid=(nb // bt,),
        in_specs=[_SMEM, per_b(N_HEADS, KV_WIDTH), per_b(1, KV_WIDTH), per_b(1, KV_WIDTH), cache, cache,
                  _full((N_HEADS, 2 * WINDOW))],
        out_specs=per_b(N_HEADS, KV_WIDTH),
        compiler_params=pltpu.CompilerParams(dimension_semantics=("arbitrary",), vmem_limit_bytes=VMEM_LIMIT),
        name="sample_attention",
    )(sinks, qs, kn, vn, cache_k, cache_v, bias_s)


def _kc_kernel(x1_ref, attn_ref, u_ref, hist_ref, mod_ref, gain_ref, poolw_ref, pscale_ref,
               wout_ref, wg_ref, wu_ref, wd_ref, fgain_ref, o_ref, mixed_ref, h_ref, act_ref, *, layer, final):
    mixed_ref[:, 0:ATTN_WIDTH] = attn_ref[...].astype(BF16)
    for g, w in enumerate(POOL_SIZES):
        cols = slice(g * POOL_GROUP_WIDTH, (g + 1) * POOL_GROUP_WIDTH)
        ug = u_ref[:, cols]
        acc = ug
        for d in range(1, w):
            r = POOL_STATE - d
            acc = acc + hist_ref[:, r * POOL_WIDTH + g * POOL_GROUP_WIDTH:r * POOL_WIDTH + (g + 1) * POOL_GROUP_WIDTH]
        pooled = (acc / float(w) - ug).astype(BF16)
        pg = jnp.dot(pooled, poolw_ref[g], preferred_element_type=F32) * pscale_ref[layer:layer + 1, cols]
        mixed_ref[:, ATTN_WIDTH + g * POOL_GROUP_WIDTH:ATTN_WIDTH + (g + 1) * POOL_GROUP_WIDTH] = pg.astype(BF16)
    mod = _mod_getter(mod_ref)
    x2, h = _out_projection(x1_ref[...], mixed_ref, wout_ref, mod, gain_ref, layer)
    h_ref[...] = h
    o_ref[...] = _ffn2_out(x2, _swiglu(h_ref, act_ref, wg_ref, wu_ref, wd_ref), mod, fgain_ref, final)


def _kc_call(x1, attn, u, hist, mod_all, mod_spec, gain, poolw, pscale, wout, wg, wu, wd, fgain, layer, final, name):
    r = x1.shape[0]
    return pl.pallas_call(
        functools.partial(_kc_kernel, layer=layer, final=final),
        out_shape=jax.ShapeDtypeStruct((r, D_MODEL), F32),
        grid=(1,),
        in_specs=[_full((r, D_MODEL)), _full((r, ATTN_WIDTH)), _full((r, POOL_WIDTH)),
                  _full((r, POOL_STATE * POOL_WIDTH)), mod_spec, _full((DEPTH, 3, D_MODEL)),
                  _layer_resident((len(POOL_SIZES), POOL_GROUP_WIDTH, POOL_GROUP_WIDTH), layer),
                  _full((DEPTH, POOL_WIDTH)),
                  _layer_resident((D_MODEL, D_MODEL), layer), _layer_resident((D_MODEL, D_FF), layer),
                  _layer_resident((D_MODEL, D_FF), layer), _layer_resident((D_FF, D_MODEL), layer),
                  _full((1, D_MODEL))],
        out_specs=_full((r, D_MODEL)),
        scratch_shapes=[pltpu.VMEM((r, D_MODEL), BF16), pltpu.VMEM((r, D_MODEL), BF16),
                        pltpu.VMEM((r, D_FF), BF16)],
        compiler_params=pltpu.CompilerParams(dimension_semantics=("arbitrary",), vmem_limit_bytes=VMEM_LIMIT),
        name=name,
    )(x1, attn, u, hist, mod_all, gain, poolw, pscale, wout, wg, wu, wd, fgain)


def kernel(x_prompt, x_sample, c_prompt, c_sample, cache_k, cache_v, state_pool, w_ada, b_ada, norm_gain,
           w_in, w_out, sinks, rel_bias, pool_w, pool_scale, ffn1_wg, ffn1_wu, ffn1_wd, ffn2_wg, ffn2_wu,
           ffn2_wd, final_gain):
    n_p, seq, _ = x_prompt.shape
    n_s = x_sample.shape[0]

    mod_all = _modulation(jnp.concatenate([c_sample, c_prompt], axis=0), w_ada, b_ada)
    bias = _bias_table(rel_bias)
    bias_s = bias.reshape(N_HEADS, WINDOW, 2 * WINDOW)[:, 0, :]
    fgain = final_gain.reshape(1, D_MODEL)

    wq = (w_in[:, :, :ATTN_WIDTH].reshape(DEPTH, D_MODEL, 2, 4, HEAD_DIM).transpose(0, 1, 3, 2, 4)
          .reshape(DEPTH, D_MODEL, ATTN_WIDTH)) * (HEAD_DIM ** -0.5)
    win = jnp.concatenate([wq, w_in[:, :, ATTN_WIDTH:]], axis=2).astype(BF16)
    wo_attn = (w_out[:, :ATTN_WIDTH].reshape(DEPTH, 2, 4, HEAD_DIM, D_MODEL).transpose(0, 2, 1, 3, 4)
               .reshape(DEPTH, ATTN_WIDTH, D_MODEL))
    wout = jnp.concatenate([wo_attn, w_out[:, ATTN_WIDTH:]], axis=1).astype(BF16)
    wg1, wu1, wd1 = ffn1_wg.astype(BF16), ffn1_wu.astype(BF16), ffn1_wd.astype(BF16)
    wg2, wu2, wd2 = ffn2_wg.astype(BF16), ffn2_wu.astype(BF16), ffn2_wd.astype(BF16)
    poolw = pool_w.astype(BF16)

    xp = x_prompt
    xs = x_sample.reshape(n_s, D_MODEL)
    ck = cache_k.reshape(DEPTH, n_s, WINDOW, KV_WIDTH)
    cv = cache_v.reshape(DEPTH, n_s, WINDOW, KV_WIDTH)
    half_mask = jnp.asarray((np.arange(KV_WIDTH)[None, :] // HEAD_DIM == np.arange(2)[:, None]), BF16)

    new_kp, new_vp, new_pp, new_ks, new_vs, new_ps = [], [], [], [], [], []
    for l in range(DEPTH):
        final = l == DEPTH - 1
        mod_p = pl.BlockSpec((None, n_p, MOD_WIDTH), lambda *_, l=l: (l, n_s // n_p, 0))
        mod_s = pl.BlockSpec((None, n_s, MOD_WIDTH), lambda *_, l=l: (l, 0, 0))

        x1, q, k, v, u, k_last, v_last, u_last = _ka_prompt_call(
            xp, mod_all, mod_p, norm_gain, wg1, wu1, wd1, win, l, f"prompt_ffn1_inproj_l{l}")
        xp = _kb_call(sinks, x1, q, k, v, u, mod_all, mod_p, norm_gain, bias, poolw, pool_scale, wout,
                      wg2, wu2, wd2, fgain, l, final, f"prompt_mixer_ffn2_l{l}")
        new_kp.append(k_last.reshape(n_p, WINDOW, N_KV_HEADS, HEAD_DIM))
        new_vp.append(v_last.reshape(n_p, WINDOW, N_KV_HEADS, HEAD_DIM))
        new_pp.append(u_last[:, HIST - POOL_STATE:])

        x1s, qs, ks, vs, us = _ka_sample_call(xs, mod_all, mod_s, norm_gain, wg1, wu1, wd1, win, l,
                                              f"sample_ffn1_inproj_l{l}")
        qrows = (qs.reshape(n_s, 4, 1, KV_WIDTH) * half_mask[None, None]).reshape(n_s, N_HEADS, KV_WIDTH)
        o = _ks_call(sinks, qrows, ks.reshape(n_s, 1, KV_WIDTH), vs.reshape(n_s, 1, KV_WIDTH), ck, cv, l, bias_s)
        o5 = o.reshape(n_s, 4, 2, 2, HEAD_DIM)
        attn_s = jnp.stack([o5[:, :, 0, 0], o5[:, :, 1, 1]], axis=2).reshape(n_s, ATTN_WIDTH)
        hist = state_pool[l].reshape(n_s, POOL_STATE * POOL_WIDTH)
        xs = _kc_call(x1s, attn_s, us, hist, mod_all, mod_s, norm_gain, poolw, pool_scale, wout, wg2, wu2, wd2,
                      fgain, l, final, f"sample_mixer_ffn2_l{l}")
        new_ks.append(jnp.concatenate([ck[l][:, 1:], ks[:, None, :]], axis=1)
                      .reshape(n_s, WINDOW, N_KV_HEADS, HEAD_DIM))
        new_vs.append(jnp.concatenate([cv[l][:, 1:], vs[:, None, :]], axis=1)
                      .reshape(n_s, WINDOW, N_KV_HEADS, HEAD_DIM))
        new_ps.append(jnp.concatenate([state_pool[l][:, 1:], us[:, None, :]], axis=1))

    return (xp, xs.reshape(n_s, 1, D_MODEL), jnp.stack(new_kp), jnp.stack(new_vp), jnp.stack(new_pp),
            jnp.stack(new_ks), jnp.stack(new_vs), jnp.stack(new_ps))
```

```python
import functools
import math

import numpy as np
import jax
import jax.numpy as jnp
from jax import lax
from jax.experimental import pallas as pl
from jax.experimental.pallas import tpu as pltpu

D_MODEL = 1024
DEPTH = 2
HEAD_DIM = 64
N_HEADS = 8
N_KV_HEADS = 2
ATTN_WIDTH = N_HEADS * HEAD_DIM
KV_WIDTH = N_KV_HEADS * HEAD_DIM
WINDOW = 128
POOL_SIZES = (2, 4, 8, 16)
POOL_GROUP_WIDTH = 128
POOL_WIDTH = len(POOL_SIZES) * POOL_GROUP_WIDTH
POOL_STATE = max(POOL_SIZES) - 1
IN_WIDTH = ATTN_WIDTH + 2 * KV_WIDTH + POOL_WIDTH
D_FF = 2816
N_BUCKETS = 32
MAX_DISTANCE = 128
N_MOD = 9
MOD_WIDTH = N_MOD * D_MODEL
EPS = 1e-6

LANES = 128
SUBLANES = 8
MXU_COLS = 256
FF_CHUNK = 2 * MXU_COLS
FF_CHUNKS = tuple((c, min(FF_CHUNK, D_FF - c)) for c in range(0, D_FF, FF_CHUNK))
TM_PROMPT = 512
HIST = 16
VMEM_LIMIT = 56 * 1024 * 1024

F32 = jnp.float32
BF16 = jnp.bfloat16
NEG_INF = float("-inf")


def _head_of_row(r):
    return (r // 2) + 4 * (r % 2)


def _t5_bucket_np(dist):
    n = np.maximum(dist, 0)
    max_exact = N_BUCKETS // 2
    nf = np.maximum(n, 1).astype(np.float32)
    large = max_exact + (np.log(nf / max_exact) / math.log(MAX_DISTANCE / max_exact)
                         * (N_BUCKETS - max_exact)).astype(np.int32)
    large = np.minimum(large, N_BUCKETS - 1)
    return np.where(n < max_exact, n, large).astype(np.int32)


def _bucket_table():
    dist = np.arange(WINDOW)[:, None] + WINDOW - np.arange(2 * WINDOW)[None, :]
    return _t5_bucket_np(dist)


def _full(shape):
    return pl.BlockSpec(shape, lambda *_: (0,) * len(shape))


def _layer_resident(shape, layer):
    return pl.BlockSpec((None,) + tuple(shape), lambda *_: (layer,) + (0,) * len(shape),
                        pipeline_mode=pl.Buffered(1))


_SMEM = pl.BlockSpec(memory_space=pltpu.SMEM)


def _rms_mod(x, gain, shift, scale):
    y = x * lax.rsqrt(jnp.mean(x * x, axis=-1, keepdims=True) + EPS)
    return (y * gain) * (1.0 + scale) + shift


def _swiglu(h_ref, act_ref, wg_ref, wu_ref, wd_ref, side_work=(), before_down=None):
    n = len(FF_CHUNKS)
    assert len(side_work) <= n
    after_chunk = {((j + 1) * n) // (len(side_work) + 1) - 1: work for j, work in enumerate(side_work)}
    for i, (c0, cw) in enumerate(FF_CHUNKS):
        g = jnp.dot(h_ref[...], wg_ref[:, c0:c0 + cw], preferred_element_type=F32)
        u = jnp.dot(h_ref[...], wu_ref[:, c0:c0 + cw], preferred_element_type=F32)
        act_ref[:, c0:c0 + cw] = ((g / (1.0 + jnp.exp(-g))) * u).astype(BF16)
        if i in after_chunk:
            after_chunk[i]()
    if before_down is not None:
        before_down()
    return jnp.dot(act_ref[...], wd_ref[...], preferred_element_type=F32)


def _mod_getter(mod_ref, row=None):
    rows = slice(None) if row is None else pl.ds(row, 1)
    return lambda k: mod_ref[rows, k * D_MODEL:(k + 1) * D_MODEL]


def _mod_kernel(c_ref, w_ref, b_ref, o_ref):
    c = c_ref[...]
    a = (c / (1.0 + jnp.exp(-c))).astype(BF16)
    o_ref[...] = jnp.dot(a, w_ref[...].astype(BF16), preferred_element_type=F32) + b_ref[...]


def _modulation(c_all, w_ada, b_ada):
    rows = c_all.shape[0]
    n_chunk = 1024
    return pl.pallas_call(
        _mod_kernel,
        out_shape=jax.ShapeDtypeStruct((DEPTH, rows, MOD_WIDTH), F32),
        grid=(DEPTH, MOD_WIDTH // n_chunk),
        in_specs=[
            pl.BlockSpec((rows, D_MODEL), lambda l, j: (0, 0)),
            pl.BlockSpec((None, D_MODEL, n_chunk), lambda l, j: (l, 0, j)),
            pl.BlockSpec((None, 1, n_chunk), lambda l, j: (l, 0, j)),
        ],
        out_specs=pl.BlockSpec((None, rows, n_chunk), lambda l, j: (l, 0, j)),
        compiler_params=pltpu.CompilerParams(
            dimension_semantics=("arbitrary", "arbitrary"), vmem_limit_bytes=VMEM_LIMIT),
        name="adaln_modulation",
    )(c_all, w_ada, b_ada.reshape(DEPTH, 1, MOD_WIDTH))


def _bias_kernel(rb_ref, bucket_ref, o_ref):
    bucket = bucket_ref[...]
    eq = [bucket == b for b in range(N_BUCKETS)]
    for r in range(N_HEADS):
        acc = jnp.zeros((WINDOW, 2 * WINDOW), F32)
        for b in range(N_BUCKETS):
            acc = jnp.where(eq[b], rb_ref[b, _head_of_row(r)], acc)
        o_ref[r * WINDOW:(r + 1) * WINDOW, :] = acc


def _bias_table(rel_bias):
    return pl.pallas_call(
        _bias_kernel,
        out_shape=jax.ShapeDtypeStruct((N_HEADS * WINDOW, 2 * WINDOW), F32),
        in_specs=[_SMEM, pl.BlockSpec(memory_space=pltpu.VMEM)],
        out_specs=pl.BlockSpec(memory_space=pltpu.VMEM),
        name="rel_bias_table",
    )(rel_bias, jnp.asarray(_bucket_table()))


def _project_q(hm, win_ref, q_ref):
    q_ref[...] = jnp.dot(hm, win_ref[:, :ATTN_WIDTH], preferred_element_type=F32).astype(BF16)


def _project_kvu(hm, win_ref, k_ref, v_ref, u_ref):
    z = jnp.dot(hm, win_ref[:, ATTN_WIDTH:], preferred_element_type=F32)
    k_ref[...] = z[:, :KV_WIDTH]
    v_ref[...] = z[:, KV_WIDTH:2 * KV_WIDTH]
    u_ref[...] = z[:, 2 * KV_WIDTH:]
    return z


def _ffn1(x_ref, mod, gain_ref, wg_ref, wu_ref, wd_ref, x1_ref, h_ref, act_ref, layer, before_norm=None):
    x = x_ref[...]
    h_ref[...] = _rms_mod(x, gain_ref[layer, 0:1, :], mod(0), mod(1)).astype(BF16)
    x1 = x + 0.5 * mod(2) * _swiglu(h_ref, act_ref, wg_ref, wu_ref, wd_ref)
    x1_ref[...] = x1
    if before_norm is not None:
        before_norm()
    return _rms_mod(x1, gain_ref[layer, 1:2, :], mod(3), mod(4)).astype(BF16)


def _ka_sample_kernel(x_ref, mod_ref, gain_ref, wg_ref, wu_ref, wd_ref, win_ref,
                      x1_ref, q_ref, k_ref, v_ref, u_ref, h_ref, act_ref, *, layer):
    hm = _ffn1(x_ref, _mod_getter(mod_ref), gain_ref, wg_ref, wu_ref, wd_ref, x1_ref, h_ref, act_ref, layer)
    _project_q(hm, win_ref, q_ref)
    _project_kvu(hm, win_ref, k_ref, v_ref, u_ref)


def _ka_prompt_kernel(x_ref, mod_ref, gain_ref, wg_ref, wu_ref, wd_ref, win_ref,
                      x1_ref, q_ref, k_ref, v_ref, u_ref, kl_ref, vl_ref, ul_ref,
                      h_ref, act_ref, hm_ref, *, layer, tiles_per_seq):
    tm = x_ref.shape[0]
    s = pl.program_id(0)
    main = jnp.minimum(s, pl.num_programs(0) - 2)

    @pl.when(s == 0)
    def _():
        hm_ref[...] = jnp.zeros_like(hm_ref)

    _project_q(hm_ref[...], win_ref, q_ref)

    def project_rest():
        z = _project_kvu(hm_ref[...], win_ref, k_ref, v_ref, u_ref)
        kl_ref[...] = z[tm - WINDOW:, :KV_WIDTH]
        vl_ref[...] = z[tm - WINDOW:, KV_WIDTH:2 * KV_WIDTH]
        ul_ref[...] = z[tm - HIST:, 2 * KV_WIDTH:]

    mod = _mod_getter(mod_ref, main // tiles_per_seq)
    hm = _ffn1(x_ref, mod, gain_ref, wg_ref, wu_ref, wd_ref, x1_ref, h_ref, act_ref, layer,
               before_norm=project_rest)
    hm_ref[...] = hm


def _ka_out_shapes(g, r):
    return (jax.ShapeDtypeStruct((g, r, D_MODEL), F32),
            jax.ShapeDtypeStruct((g, r, ATTN_WIDTH), BF16),
            jax.ShapeDtypeStruct((g, r, KV_WIDTH), F32),
            jax.ShapeDtypeStruct((g, r, KV_WIDTH), F32),
            jax.ShapeDtypeStruct((g, r, POOL_WIDTH), F32))


def _ka_weight_specs(layer):
    return [_full((DEPTH, 3, D_MODEL)),
            _layer_resident((D_MODEL, D_FF), layer), _layer_resident((D_MODEL, D_FF), layer),
            _layer_resident((D_FF, D_MODEL), layer), _layer_resident((D_MODEL, IN_WIDTH), layer)]


def _ka_sample_call(x, mod_all, mod_spec, gain, wg, wu, wd, win, layer, name):
    r = x.shape[0]
    row = lambda w: _full((r, w))
    return pl.pallas_call(
        functools.partial(_ka_sample_kernel, layer=layer),
        out_shape=tuple(jax.ShapeDtypeStruct(s.shape[1:], s.dtype) for s in _ka_out_shapes(1, r)),
        grid=(1,),
        in_specs=[row(D_MODEL), mod_spec] + _ka_weight_specs(layer),
        out_specs=(row(D_MODEL), row(ATTN_WIDTH), row(KV_WIDTH), row(KV_WIDTH), row(POOL_WIDTH)),
        scratch_shapes=[pltpu.VMEM((r, D_MODEL), BF16), pltpu.VMEM((r, D_FF), BF16)],
        compiler_params=pltpu.CompilerParams(dimension_semantics=("arbitrary",), vmem_limit_bytes=VMEM_LIMIT),
        name=name,
    )(x, mod_all, gain, wg, wu, wd, win)


def _ka_prompt_call(x, mod_all, mod_spec, gain, wg, wu, wd, win, layer, name):
    g, r, _ = x.shape
    tm = TM_PROMPT
    tps = r // tm
    n_tiles = g * tps
    main = lambda s: jnp.minimum(s, n_tiles - 1)
    prev = lambda s: jnp.maximum(s - 1, 0)
    main_row = lambda w: pl.BlockSpec((None, tm, w), lambda s: (main(s) // tps, main(s) % tps, 0))
    prev_row = lambda w: pl.BlockSpec((None, tm, w), lambda s: (prev(s) // tps, prev(s) % tps, 0))
    last = lambda rows, w: pl.BlockSpec((None, rows, w), lambda s: (prev(s) // tps, 0, 0))
    return pl.pallas_call(
        functools.partial(_ka_prompt_kernel, layer=layer, tiles_per_seq=tps),
        out_shape=_ka_out_shapes(g, r) + (jax.ShapeDtypeStruct((g, WINDOW, KV_WIDTH), F32),
                                          jax.ShapeDtypeStruct((g, WINDOW, KV_WIDTH), F32),
                                          jax.ShapeDtypeStruct((g, HIST, POOL_WIDTH), F32)),
        grid=(n_tiles + 1,),
        in_specs=[main_row(D_MODEL), mod_spec] + _ka_weight_specs(layer),
        out_specs=(main_row(D_MODEL), prev_row(ATTN_WIDTH), prev_row(KV_WIDTH), prev_row(KV_WIDTH),
                   prev_row(POOL_WIDTH), last(WINDOW, KV_WIDTH), last(WINDOW, KV_WIDTH), last(HIST, POOL_WIDTH)),
        scratch_shapes=[pltpu.VMEM((tm, D_MODEL), BF16), pltpu.VMEM((tm, D_FF), BF16),
                        pltpu.VMEM((tm, D_MODEL), BF16)],
        compiler_params=pltpu.CompilerParams(dimension_semantics=("arbitrary",), vmem_limit_bytes=VMEM_LIMIT),
        name=name,
    )(x, mod_all, gain, wg, wu, wd, win)


def _low_half():
    return lax.broadcasted_iota(jnp.int32, (WINDOW, LANES), 1) < HEAD_DIM


def _attn_probs(qb, kk, bias_ref, sinks_ref, layer, valid, p_ref):
    lo = _low_half()
    zero = jnp.zeros((WINDOW, LANES), BF16)
    pieces = []
    for c in range(4):
        qc = qb[:, c * LANES:(c + 1) * LANES]
        pieces += [jnp.where(lo, qc, zero), jnp.where(lo, zero, qc)]
    qs = jnp.concatenate(pieces, axis=0)
    s = lax.dot_general(qs, kk, (((1,), (1,)), ((), ())), preferred_element_type=F32)
    for r in range(N_HEADS):
        rows = slice(r * WINDOW, (r + 1) * WINDOW)
        sr = jnp.where(valid, s[rows] + bias_ref[rows, :], NEG_INF)
        sink = sinks_ref[layer, _head_of_row(r)]
        m = jnp.maximum(jnp.max(sr, axis=-1, keepdims=True), sink)
        p = jnp.exp(sr - m)
        denom = jnp.sum(p, axis=-1, keepdims=True) + jnp.exp(sink - m)
        p_ref[rows, :] = (p / denom).astype(BF16)


def _attn_values(p_ref, vv):
    lo = _low_half()
    o = jnp.dot(p_ref[...], vv, preferred_element_type=F32)
    cols = [jnp.where(lo, o[(2 * c) * WINDOW:(2 * c + 1) * WINDOW], o[(2 * c + 1) * WINDOW:(2 * c + 2) * WINDOW])
            for c in range(4)]
    return jnp.concatenate(cols, axis=1)


def _mix_prompt_steps(q_ref, k_ref, v_ref, kp_ref, vp_ref, u_ref, up_ref, bias_ref, sinks_ref, poolw_ref,
                      pscale_ref, mixed_ref, uext_ref, pooled_ref, p_ref, layer, seq_tile):
    tm = q_ref.shape[0]
    nb = tm // WINDOW
    first = seq_tile == 0

    def keys_values(ref, prev_ref, j):
        prev = prev_ref[...] if j == 0 else ref[(j - 1) * WINDOW:j * WINDOW, :]
        return jnp.concatenate([prev, ref[j * WINDOW:(j + 1) * WINDOW, :]], axis=0).astype(BF16)

    def probs(j):
        qi = lax.broadcasted_iota(jnp.int32, (WINDOW, 2 * WINDOW), 0)
        kj = lax.broadcasted_iota(jnp.int32, (WINDOW, 2 * WINDOW), 1)
        dist = qi + WINDOW - kj
        valid = (dist >= 0) & (dist <= WINDOW)
        if j == 0:
            valid = valid & (kj >= jnp.where(first, WINDOW, 0))
        _attn_probs(q_ref[j * WINDOW:(j + 1) * WINDOW, :], keys_values(k_ref, kp_ref, j), bias_ref, sinks_ref,
                    layer, valid, p_ref)

    def values(j):
        a = _attn_values(p_ref, keys_values(v_ref, vp_ref, j))
        mixed_ref[j * WINDOW:(j + 1) * WINDOW, 0:ATTN_WIDTH] = a.astype(BF16)

    def pool_history():
        uext_ref[0:HIST, :] = jnp.where(first, 0.0, up_ref[...])
        uext_ref[HIST:, :] = u_ref[...]

    def pool_sum(g):
        w = POOL_SIZES[g]
        cols = slice(g * POOL_GROUP_WIDTH, (g + 1) * POOL_GROUP_WIDTH)
        ug = uext_ref[HIST:, cols]
        acc = ug
        for d in range(1, w):
            acc = acc + uext_ref[HIST - d:HIST - d + tm, cols]
        pos = seq_tile * tm + lax.broadcasted_iota(jnp.int32, (tm, POOL_GROUP_WIDTH), 0)
        count = jnp.minimum(pos + 1, w).astype(F32)
        pooled_ref[:, cols] = (acc / count - ug).astype(BF16)

    def pool_map(g):
        cols = slice(g * POOL_GROUP_WIDTH, (g + 1) * POOL_GROUP_WIDTH)
        pg = jnp.dot(pooled_ref[:, cols], poolw_ref[g], preferred_element_type=F32) * pscale_ref[layer:layer + 1, cols]
        mixed_ref[:, ATTN_WIDTH + g * POOL_GROUP_WIDTH:ATTN_WIDTH + (g + 1) * POOL_GROUP_WIDTH] = pg.astype(BF16)

    assert nb == len(POOL_SIZES)

    def stage(i):
        def run():
            if i > 0:
                values(i - 1)
                pool_map(i - 1)
            if i < nb:
                probs(i)
                pool_sum(i)
        return run

    return pool_history, [stage(i) for i in range(nb + 1)]


def _out_projection(x1, mixed_ref, wout_ref, mod, gain_ref, layer):
    x2 = x1 + mod(5) * jnp.dot(mixed_ref[...], wout_ref[...], preferred_element_type=F32)
    return x2, _rms_mod(x2, gain_ref[layer, 2:3, :], mod(6), mod(7)).astype(BF16)


def _ffn2_out(x2, y, mod, fgain_ref, final):
    x3 = x2 + 0.5 * mod(8) * y
    if final:
        x3 = x3 * lax.rsqrt(jnp.mean(x3 * x3, axis=-1, keepdims=True) + EPS) * fgain_ref[...]
    return x3


def _kb_kernel(sinks_ref, x1_ref, q_ref, k_ref, v_ref, kp_ref, vp_ref, u_ref, up_ref, mod_ref, gain_ref,
               bias_ref, poolw_ref, pscale_ref, wout_ref, wg_ref, wu_ref, wd_ref, fgain_ref,
               o_ref, mixed_ref, h_ref, x2_ref, act_ref, uext_ref, pooled_ref, p_ref,
               *, layer, final, tiles_per_seq):
    s = pl.program_id(0)
    n_tiles = pl.num_programs(0) - 1
    mix_tile = jnp.minimum(s, n_tiles - 1)
    ffn_tile = jnp.maximum(s - 1, 0)
    slot = s % 2

    @pl.when(s == 0)
    def _():
        h_ref[...] = jnp.zeros_like(h_ref)
        x2_ref[1] = jnp.zeros(x2_ref.shape[1:], F32)

    mix_prologue, mix_steps = _mix_prompt_steps(
        q_ref, k_ref, v_ref, kp_ref, vp_ref, u_ref, up_ref, bias_ref, sinks_ref, poolw_ref, pscale_ref,
        mixed_ref, uext_ref, pooled_ref, p_ref, layer, mix_tile % tiles_per_seq)
    mix_prologue()

    def finish_mix():
        x2, h = _out_projection(x1_ref[...], mixed_ref, wout_ref, _mod_getter(mod_ref, mix_tile // tiles_per_seq),
                                gain_ref, layer)
        x2_ref[slot] = x2
        h_ref[...] = h

    y = _swiglu(h_ref, act_ref, wg_ref, wu_ref, wd_ref, side_work=mix_steps, before_down=finish_mix)
    o_ref[...] = _ffn2_out(x2_ref[1 - slot], y, _mod_getter(mod_ref, ffn_tile // tiles_per_seq), fgain_ref, final)


def _kb_call(sinks, x1, q, k, v, u, mod_all, mod_spec, gain, bias, poolw, pscale, wout, wg, wu, wd, fgain,
             layer, final, name):
    g, r, _ = x1.shape
    tm = TM_PROMPT
    tps = r // tm
    n_tiles = g * tps
    nb = tm // WINDOW
    nh = tm // HIST
    mix = lambda s: jnp.minimum(s, n_tiles - 1)
    tail = lambda s: jnp.maximum(s - 1, 0)
    mix_row = lambda w: pl.BlockSpec((None, tm, w), lambda s: (mix(s) // tps, mix(s) % tps, 0))
    tail_row = lambda w: pl.BlockSpec((None, tm, w), lambda s: (tail(s) // tps, tail(s) % tps, 0))
    prev_kv = pl.BlockSpec((None, WINDOW, KV_WIDTH),
                           lambda s: (mix(s) // tps, jnp.maximum((mix(s) % tps) * nb - 1, 0), 0))
    prev_u = pl.BlockSpec((None, HIST, POOL_WIDTH),
                          lambda s: (mix(s) // tps, jnp.maximum((mix(s) % tps) * nh - 1, 0), 0))
    return pl.pallas_call(
        functools.partial(_kb_kernel, layer=layer, final=final, tiles_per_seq=tps),
        out_shape=jax.ShapeDtypeStruct((g, r, D_MODEL), F32),
        grid=(n_tiles + 1,),
        in_specs=[_SMEM, mix_row(D_MODEL), mix_row(ATTN_WIDTH), mix_row(KV_WIDTH), mix_row(KV_WIDTH),
                  prev_kv, prev_kv, mix_row(POOL_WIDTH), prev_u, mod_spec, _full((DEPTH, 3, D_MODEL)),
                  _full((N_HEADS * WINDOW, 2 * WINDOW)),
                  _layer_resident((len(POOL_SIZES), POOL_GROUP_WIDTH, POOL_GROUP_WIDTH), layer),
                  _full((DEPTH, POOL_WIDTH)),
                  _layer_resident((D_MODEL, D_MODEL), layer), _layer_resident((D_MODEL, D_FF), layer),
                  _layer_resident((D_MODEL, D_FF), layer), _layer_resident((D_FF, D_MODEL), layer),
                  _full((1, D_MODEL))],
        out_specs=tail_row(D_MODEL),
        scratch_shapes=[pltpu.VMEM((tm, D_MODEL), BF16), pltpu.VMEM((tm, D_MODEL), BF16),
                        pltpu.VMEM((2, tm, D_MODEL), F32), pltpu.VMEM((tm, D_FF), BF16),
                        pltpu.VMEM((HIST + tm, POOL_WIDTH), F32), pltpu.VMEM((tm, POOL_WIDTH), BF16),
                        pltpu.VMEM((N_HEADS * WINDOW, 2 * WINDOW), BF16)],
        compiler_params=pltpu.CompilerParams(
            dimension_semantics=("arbitrary",), vmem_limit_bytes=VMEM_LIMIT),
        name=name,
    )(sinks, x1, q, k, v, k, v, u, u, mod_all, gain, bias, poolw, pscale, wout, wg, wu, wd, fgain)


def _ks_kernel(sinks_ref, q_ref, kn_ref, vn_ref, ck_ref, cv_ref, bias_ref, o_ref,
               qs_ref, knr_ref, vnr_ref, s_ref, p_ref, acc_ref, *, layer):
    bt = q_ref.shape[0]
    rows = bt * N_HEADS
    nt = (((1,), (1,)), ((), ()))
    head_rows = lambda r: pl.ds(r, bt, stride=N_HEADS)
    lo = lax.broadcasted_iota(jnp.int32, (bt, LANES), 1) < HEAD_DIM
    kn = kn_ref[...].astype(BF16).astype(F32)
    vn = vn_ref[...].astype(BF16).astype(F32)
    for c in range(4):
        qc = q_ref[:, c * LANES:(c + 1) * LANES].astype(F32)
        qs_ref[head_rows(2 * c), :] = jnp.where(lo, qc, 0.0)
        qs_ref[head_rows(2 * c + 1), :] = jnp.where(lo, 0.0, qc)
    for r in range(N_HEADS):
        knr_ref[head_rows(r), :] = kn
        vnr_ref[head_rows(r), :] = vn

    for b in range(bt):
        rb = slice(b * N_HEADS, (b + 1) * N_HEADS)
        s_ref[rb, :] = lax.dot_general(qs_ref[rb, :].astype(BF16), ck_ref[b].astype(BF16), nt,
                                       preferred_element_type=F32)

    head = jnp.bitwise_and(lax.broadcasted_iota(jnp.int32, (rows, 1), 0), N_HEADS - 1)
    sink = jnp.zeros((rows, 1), F32)
    for r in range(N_HEADS):
        sink = jnp.where(head == r, sinks_ref[layer, _head_of_row(r)], sink)
    bias = jnp.concatenate([bias_ref[...]] * bt, axis=0)
    s_c = s_ref[...] + bias[:, 0:WINDOW]
    s_n = jnp.sum(qs_ref[...] * knr_ref[...], axis=-1, keepdims=True) + bias[:, WINDOW:WINDOW + 1]
    m = jnp.maximum(jnp.maximum(jnp.max(s_c, axis=-1, keepdims=True), s_n), sink)
    p_c = jnp.exp(s_c - m)
    p_n = jnp.exp(s_n - m)
    denom = jnp.sum(p_c, axis=-1, keepdims=True) + p_n + jnp.exp(sink - m)
    p_ref[...] = p_c / denom

    for b in range(bt):
        rb = slice(b * N_HEADS, (b + 1) * N_HEADS)
        acc_ref[rb, :] = jnp.dot(p_ref[rb, :].astype(BF16), cv_ref[b].astype(BF16), preferred_element_type=F32)
    acc_ref[...] = acc_ref[...] + (p_n / denom).astype(BF16).astype(F32) * vnr_ref[...]

    for c in range(4):
        o_ref[:, c * LANES:(c + 1) * LANES] = jnp.where(
            lo, acc_ref[head_rows(2 * c), :], acc_ref[head_rows(2 * c + 1), :]).astype(BF16)


def _ks_call(sinks, q, kn, vn, cache_k, cache_v, layer, bias_s):
    nb = q.shape[0]
    bt = 32
    rows = bt * N_HEADS
    per_b = lambda w: pl.BlockSpec((bt, w), lambda i: (i, 0))
    cache = pl.BlockSpec((None, bt, WINDOW, KV_WIDTH), lambda i: (layer, i, 0, 0))
    return pl.pallas_call(
        functools.partial(_ks_kernel, layer=layer),
        out_shape=jax.ShapeDtypeStruct((nb, ATTN_WIDTH), BF16),
        grid=(nb // bt,),
        in_specs=[_SMEM, per_b(ATTN_WIDTH), per_b(KV_WIDTH), per_b(KV_WIDTH), cache, cache,
                  _full((N_HEADS, 2 * WINDOW))],
        out_specs=per_b(ATTN_WIDTH),
        scratch_shapes=[pltpu.VMEM((rows, KV_WIDTH), F32), pltpu.VMEM((rows, KV_WIDTH), F32),
                        pltpu.VMEM((rows, KV_WIDTH), F32), pltpu.VMEM((rows, WINDOW), F32),
                        pltpu.VMEM((rows, WINDOW), F32), pltpu.VMEM((rows, KV_WIDTH), F32)],
        compiler_params=pltpu.CompilerParams(dimension_semantics=("arbitrary",), vmem_limit_bytes=VMEM_LIMIT),
        name="sample_attention",
    )(sinks, q, kn, vn, cache_k, cache_v, bias_s)


def _kc_kernel(x1_ref, attn_ref, u_ref, hist_ref, mod_ref, gain_ref, poolw_ref, pscale_ref,
               wout_ref, wg_ref, wu_ref, wd_ref, fgain_ref, o_ref, mixed_ref, h_ref, act_ref, *, layer, final):
    mixed_ref[:, 0:ATTN_WIDTH] = attn_ref[...]
    for g, w in enumerate(POOL_SIZES):
        cols = slice(g * POOL_GROUP_WIDTH, (g + 1) * POOL_GROUP_WIDTH)
        ug = u_ref[:, cols]
        acc = ug
        for d in range(1, w):
            r = POOL_STATE - d
            acc = acc + hist_ref[:, r * POOL_WIDTH + g * POOL_GROUP_WIDTH:r * POOL_WIDTH + (g + 1) * POOL_GROUP_WIDTH]
        pooled = (acc / float(w) - ug).astype(BF16)
        pg = jnp.dot(pooled, poolw_ref[g], preferred_element_type=F32) * pscale_ref[layer:layer + 1, cols]
        mixed_ref[:, ATTN_WIDTH + g * POOL_GROUP_WIDTH:ATTN_WIDTH + (g + 1) * POOL_GROUP_WIDTH] = pg.astype(BF16)
    mod = _mod_getter(mod_ref)
    x2, h = _out_projection(x1_ref[...], mixed_ref, wout_ref, mod, gain_ref, layer)
    h_ref[...] = h
    o_ref[...] = _ffn2_out(x2, _swiglu(h_ref, act_ref, wg_ref, wu_ref, wd_ref), mod, fgain_ref, final)


def _kc_call(x1, attn, u, hist, mod_all, mod_spec, gain, poolw, pscale, wout, wg, wu, wd, fgain, layer, final, name):
    r = x1.shape[0]
    return pl.pallas_call(
        functools.partial(_kc_kernel, layer=layer, final=final),
        out_shape=jax.ShapeDtypeStruct((r, D_MODEL), F32),
        grid=(1,),
        in_specs=[_full((r, D_MODEL)), _full((r, ATTN_WIDTH)), _full((r, POOL_WIDTH)),
                  pl.BlockSpec((None, r, POOL_STATE * POOL_WIDTH), lambda *_: (layer, 0, 0)),
                  mod_spec, _full((DEPTH, 3, D_MODEL)),
                  _layer_resident((len(POOL_SIZES), POOL_GROUP_WIDTH, POOL_GROUP_WIDTH), layer),
                  _full((DEPTH, POOL_WIDTH)),
                  _layer_resident((D_MODEL, D_MODEL), layer), _layer_resident((D_MODEL, D_FF), layer),
                  _layer_resident((D_MODEL, D_FF), layer), _layer_resident((D_FF, D_MODEL), layer),
                  _full((1, D_MODEL))],
        out_specs=_full((r, D_MODEL)),
        scratch_shapes=[pltpu.VMEM((r, D_MODEL), BF16), pltpu.VMEM((r, D_MODEL), BF16),
                        pltpu.VMEM((r, D_FF), BF16)],
        compiler_params=pltpu.CompilerParams(dimension_semantics=("arbitrary",), vmem_limit_bytes=VMEM_LIMIT),
        name=name,
    )(x1, attn, u, hist, mod_all, gain, poolw, pscale, wout, wg, wu, wd, fgain)


def kernel(x_prompt, x_sample, c_prompt, c_sample, cache_k, cache_v, state_pool, w_ada, b_ada, norm_gain,
           w_in, w_out, sinks, rel_bias, pool_w, pool_scale, ffn1_wg, ffn1_wu, ffn1_wd, ffn2_wg, ffn2_wu,
           ffn2_wd, final_gain):
    n_p, seq, _ = x_prompt.shape
    n_s = x_sample.shape[0]

    mod_all = _modulation(jnp.concatenate([c_sample, c_prompt], axis=0), w_ada, b_ada)
    bias = _bias_table(rel_bias)
    bias_s = bias.reshape(N_HEADS, WINDOW, 2 * WINDOW)[:, 0, :]
    fgain = final_gain.reshape(1, D_MODEL)

    wq = (w_in[:, :, :ATTN_WIDTH].reshape(DEPTH, D_MODEL, 2, 4, HEAD_DIM).transpose(0, 1, 3, 2, 4)
          .reshape(DEPTH, D_MODEL, ATTN_WIDTH)) * (HEAD_DIM ** -0.5)
    win = jnp.concatenate([wq, w_in[:, :, ATTN_WIDTH:]], axis=2).astype(BF16)
    wo_attn = (w_out[:, :ATTN_WIDTH].reshape(DEPTH, 2, 4, HEAD_DIM, D_MODEL).transpose(0, 2, 1, 3, 4)
               .reshape(DEPTH, ATTN_WIDTH, D_MODEL))
    wout = jnp.concatenate([wo_attn, w_out[:, ATTN_WIDTH:]], axis=1).astype(BF16)
    wg1, wu1, wd1 = ffn1_wg.astype(BF16), ffn1_wu.astype(BF16), ffn1_wd.astype(BF16)
    wg2, wu2, wd2 = ffn2_wg.astype(BF16), ffn2_wu.astype(BF16), ffn2_wd.astype(BF16)
    poolw = pool_w.astype(BF16)

    xp = x_prompt
    xs = x_sample.reshape(n_s, D_MODEL)
    ck = cache_k.reshape(DEPTH, n_s, WINDOW, KV_WIDTH)
    cv = cache_v.reshape(DEPTH, n_s, WINDOW, KV_WIDTH)
    hist = state_pool.reshape(DEPTH, n_s, POOL_STATE * POOL_WIDTH)

    new_kp, new_vp, new_pp, new_ks, new_vs, new_ps = [], [], [], [], [], []
    for l in range(DEPTH):
        final = l == DEPTH - 1
        mod_p = pl.BlockSpec((None, n_p, MOD_WIDTH), lambda *_, l=l: (l, n_s // n_p, 0))
        mod_s = pl.BlockSpec((None, n_s, MOD_WIDTH), lambda *_, l=l: (l, 0, 0))

        x1, q, k, v, u, k_last, v_last, u_last = _ka_prompt_call(
            xp, mod_all, mod_p, norm_gain, wg1, wu1, wd1, win, l, f"prompt_ffn1_inproj_l{l}")
        xp = _kb_call(sinks, x1, q, k, v, u, mod_all, mod_p, norm_gain, bias, poolw, pool_scale, wout,
                      wg2, wu2, wd2, fgain, l, final, f"prompt_mixer_ffn2_l{l}")
        new_kp.append(k_last)
        new_vp.append(v_last)
        new_pp.append(u_last)

        x1s, qs, ks, vs, us = _ka_sample_call(xs, mod_all, mod_s, norm_gain, wg1, wu1, wd1, win, l,
                                              f"sample_ffn1_inproj_l{l}")
        attn_s = _ks_call(sinks, qs, ks, vs, ck, cv, l, bias_s)
        xs = _kc_call(x1s, attn_s, us, hist, mod_all, mod_s, norm_gain, poolw, pool_scale, wout, wg2, wu2, wd2,
                      fgain, l, final, f"sample_mixer_ffn2_l{l}")
        new_ks.append(ks)
        new_vs.append(vs)
        new_ps.append(us)

    row = lambda rows, shape: jnp.stack(rows).reshape((DEPTH, n_s, 1) + shape)
    new_k_sample = jnp.concatenate([cache_k[:, :, 1:], row(new_ks, (N_KV_HEADS, HEAD_DIM))], axis=2)
    new_v_sample = jnp.concatenate([cache_v[:, :, 1:], row(new_vs, (N_KV_HEADS, HEAD_DIM))], axis=2)
    new_pool_sample = jnp.concatenate([state_pool[:, :, 1:], row(new_ps, (POOL_WIDTH,))], axis=2)
    kv_shape = (DEPTH, n_p, WINDOW, N_KV_HEADS, HEAD_DIM)
    return (xp, xs.reshape(n_s, 1, D_MODEL), jnp.stack(new_kp).reshape(kv_shape), jnp.stack(new_vp).reshape(kv_shape),
            jnp.stack(new_pp)[:, :, HIST - POOL_STATE:], new_k_sample, new_v_sample, new_pool_sample)
```

```python
import functools
import math

import numpy as np
import jax
import jax.numpy as jnp
from jax import lax
from jax.experimental import pallas as pl
from jax.experimental.pallas import tpu as pltpu

D_MODEL = 1024
DEPTH = 2
HEAD_DIM = 64
N_HEADS = 8
N_KV_HEADS = 2
ATTN_WIDTH = N_HEADS * HEAD_DIM
KV_WIDTH = N_KV_HEADS * HEAD_DIM
WINDOW = 128
POOL_SIZES = (2, 4, 8, 16)
POOL_GROUP_WIDTH = 128
POOL_WIDTH = len(POOL_SIZES) * POOL_GROUP_WIDTH
POOL_STATE = max(POOL_SIZES) - 1
IN_WIDTH = ATTN_WIDTH + 2 * KV_WIDTH + POOL_WIDTH
D_FF = 2816
N_BUCKETS = 32
MAX_DISTANCE = 128
N_MOD = 9
MOD_WIDTH = N_MOD * D_MODEL
EPS = 1e-6

LANES = 128
SUBLANES = 8
MXU_COLS = 256
FF_CHUNK = 2 * MXU_COLS
FF_CHUNKS = tuple((c, min(FF_CHUNK, D_FF - c)) for c in range(0, D_FF, FF_CHUNK))
TM_PROMPT = 512
HIST = 16
VMEM_LIMIT = 56 * 1024 * 1024

F32 = jnp.float32
BF16 = jnp.bfloat16
NEG_INF = float("-inf")


def _head_of_row(r):
    return (r // 2) + 4 * (r % 2)


def _t5_bucket_np(dist):
    n = np.maximum(dist, 0)
    max_exact = N_BUCKETS // 2
    nf = np.maximum(n, 1).astype(np.float32)
    large = max_exact + (np.log(nf / max_exact) / math.log(MAX_DISTANCE / max_exact)
                         * (N_BUCKETS - max_exact)).astype(np.int32)
    large = np.minimum(large, N_BUCKETS - 1)
    return np.where(n < max_exact, n, large).astype(np.int32)


def _bucket_table():
    dist = np.arange(WINDOW)[:, None] + WINDOW - np.arange(2 * WINDOW)[None, :]
    return _t5_bucket_np(dist)


def _full(shape):
    return pl.BlockSpec(shape, lambda *_: (0,) * len(shape))


def _layer_resident(shape, layer):
    return pl.BlockSpec((None,) + tuple(shape), lambda *_: (layer,) + (0,) * len(shape),
                        pipeline_mode=pl.Buffered(1))


_SMEM = pl.BlockSpec(memory_space=pltpu.SMEM)


def _rms_mod(x, gain, shift, scale):
    y = x * lax.rsqrt(jnp.mean(x * x, axis=-1, keepdims=True) + EPS)
    return (y * gain) * (1.0 + scale) + shift


def _swiglu(h_ref, act_ref, wg_ref, wu_ref, wd_ref, side_work=(), before_down=None):
    n = len(FF_CHUNKS)
    assert len(side_work) <= n
    after_chunk = {((j + 1) * n) // (len(side_work) + 1) - 1: work for j, work in enumerate(side_work)}
    for i, (c0, cw) in enumerate(FF_CHUNKS):
        g = jnp.dot(h_ref[...], wg_ref[:, c0:c0 + cw], preferred_element_type=F32)
        u = jnp.dot(h_ref[...], wu_ref[:, c0:c0 + cw], preferred_element_type=F32)
        act_ref[:, c0:c0 + cw] = ((g / (1.0 + jnp.exp(-g))) * u).astype(BF16)
        if i in after_chunk:
            after_chunk[i]()
    if before_down is not None:
        before_down()
    return jnp.dot(act_ref[...], wd_ref[...], preferred_element_type=F32)


def _mod_getter(mod_ref, row=None):
    rows = slice(None) if row is None else pl.ds(row, 1)
    return lambda k: mod_ref[rows, k * D_MODEL:(k + 1) * D_MODEL]


def _mod_kernel(c_ref, w_ref, b_ref, o_ref):
    c = c_ref[...]
    a = (c / (1.0 + jnp.exp(-c))).astype(BF16)
    o_ref[...] = jnp.dot(a, w_ref[...].astype(BF16), preferred_element_type=F32) + b_ref[...]


def _modulation(c_all, w_ada, b_ada):
    rows = c_all.shape[0]
    n_chunk = 1024
    return pl.pallas_call(
        _mod_kernel,
        out_shape=jax.ShapeDtypeStruct((DEPTH, rows, MOD_WIDTH), F32),
        grid=(DEPTH, MOD_WIDTH // n_chunk),
        in_specs=[
            pl.BlockSpec((rows, D_MODEL), lambda l, j: (0, 0)),
            pl.BlockSpec((None, D_MODEL, n_chunk), lambda l, j: (l, 0, j)),
            pl.BlockSpec((None, 1, n_chunk), lambda l, j: (l, 0, j)),
        ],
        out_specs=pl.BlockSpec((None, rows, n_chunk), lambda l, j: (l, 0, j)),
        compiler_params=pltpu.CompilerParams(
            dimension_semantics=("arbitrary", "arbitrary"), vmem_limit_bytes=VMEM_LIMIT),
        name="adaln_modulation",
    )(c_all, w_ada, b_ada.reshape(DEPTH, 1, MOD_WIDTH))


def _bias_kernel(rb_ref, bucket_ref, o_ref):
    bucket = bucket_ref[...]
    eq = [bucket == b for b in range(N_BUCKETS)]
    for r in range(N_HEADS):
        acc = jnp.zeros((WINDOW, 2 * WINDOW), F32)
        for b in range(N_BUCKETS):
            acc = jnp.where(eq[b], rb_ref[b, _head_of_row(r)], acc)
        o_ref[r * WINDOW:(r + 1) * WINDOW, :] = acc


def _bias_table(rel_bias):
    return pl.pallas_call(
        _bias_kernel,
        out_shape=jax.ShapeDtypeStruct((N_HEADS * WINDOW, 2 * WINDOW), F32),
        in_specs=[_SMEM, pl.BlockSpec(memory_space=pltpu.VMEM)],
        out_specs=pl.BlockSpec(memory_space=pltpu.VMEM),
        name="rel_bias_table",
    )(rel_bias, jnp.asarray(_bucket_table()))


def _project_q(hm, win_ref, q_ref):
    q_ref[...] = jnp.dot(hm, win_ref[:, :ATTN_WIDTH], preferred_element_type=F32).astype(BF16)


def _project_kvu(hm, win_ref, k_ref, v_ref, u_ref):
    z = jnp.dot(hm, win_ref[:, ATTN_WIDTH:], preferred_element_type=F32)
    k_ref[...] = z[:, :KV_WIDTH]
    v_ref[...] = z[:, KV_WIDTH:2 * KV_WIDTH]
    u_ref[...] = z[:, 2 * KV_WIDTH:]
    return z


def _ffn1(x_ref, mod, gain_ref, wg_ref, wu_ref, wd_ref, x1_ref, h_ref, act_ref, layer, before_norm=None):
    x = x_ref[...]
    h_ref[...] = _rms_mod(x, gain_ref[layer, 0:1, :], mod(0), mod(1)).astype(BF16)
    x1 = x + 0.5 * mod(2) * _swiglu(h_ref, act_ref, wg_ref, wu_ref, wd_ref)
    x1_ref[...] = x1
    if before_norm is not None:
        before_norm()
    return _rms_mod(x1, gain_ref[layer, 1:2, :], mod(3), mod(4)).astype(BF16)


def _ka_sample_kernel(x_ref, mod_ref, gain_ref, wg_ref, wu_ref, wd_ref, win_ref,
                      x1_ref, q_ref, k_ref, v_ref, u_ref, h_ref, act_ref, *, layer):
    hm = _ffn1(x_ref, _mod_getter(mod_ref), gain_ref, wg_ref, wu_ref, wd_ref, x1_ref, h_ref, act_ref, layer)
    _project_q(hm, win_ref, q_ref)
    _project_kvu(hm, win_ref, k_ref, v_ref, u_ref)


def _ka_prompt_kernel(x_ref, mod_ref, gain_ref, wg_ref, wu_ref, wd_ref, win_ref,
                      x1_ref, q_ref, k_ref, v_ref, u_ref, kl_ref, vl_ref, ul_ref,
                      h_ref, act_ref, hm_ref, *, layer, tiles_per_seq):
    tm = x_ref.shape[0]
    s = pl.program_id(0)
    main = jnp.minimum(s, pl.num_programs(0) - 2)

    @pl.when(s == 0)
    def _():
        hm_ref[...] = jnp.zeros_like(hm_ref)

    _project_q(hm_ref[...], win_ref, q_ref)

    def project_rest():
        z = _project_kvu(hm_ref[...], win_ref, k_ref, v_ref, u_ref)
        kl_ref[...] = z[tm - WINDOW:, :KV_WIDTH]
        vl_ref[...] = z[tm - WINDOW:, KV_WIDTH:2 * KV_WIDTH]
        ul_ref[...] = z[tm - HIST:, 2 * KV_WIDTH:]

    mod = _mod_getter(mod_ref, main // tiles_per_seq)
    hm = _ffn1(x_ref, mod, gain_ref, wg_ref, wu_ref, wd_ref, x1_ref, h_ref, act_ref, layer,
               before_norm=project_rest)
    hm_ref[...] = hm


def _ka_out_shapes(g, r):
    return (jax.ShapeDtypeStruct((g, r, D_MODEL), F32),
            jax.ShapeDtypeStruct((g, r, ATTN_WIDTH), BF16),
            jax.ShapeDtypeStruct((g, r, KV_WIDTH), F32),
            jax.ShapeDtypeStruct((g, r, KV_WIDTH), F32),
            jax.ShapeDtypeStruct((g, r, POOL_WIDTH), F32))


def _ka_weight_specs(layer):
    return [_full((DEPTH, 3, D_MODEL)),
            _layer_resident((D_MODEL, D_FF), layer), _layer_resident((D_MODEL, D_FF), layer),
            _layer_resident((D_FF, D_MODEL), layer), _layer_resident((D_MODEL, IN_WIDTH), layer)]


def _ka_sample_call(x, mod_all, mod_spec, gain, wg, wu, wd, win, layer, name):
    r = x.shape[0]
    row = lambda w: _full((r, w))
    return pl.pallas_call(
        functools.partial(_ka_sample_kernel, layer=layer),
        out_shape=tuple(jax.ShapeDtypeStruct(s.shape[1:], s.dtype) for s in _ka_out_shapes(1, r)),
        grid=(1,),
        in_specs=[row(D_MODEL), mod_spec] + _ka_weight_specs(layer),
        out_specs=(row(D_MODEL), row(ATTN_WIDTH), row(KV_WIDTH), row(KV_WIDTH), row(POOL_WIDTH)),
        scratch_shapes=[pltpu.VMEM((r, D_MODEL), BF16), pltpu.VMEM((r, D_FF), BF16)],
        compiler_params=pltpu.CompilerParams(dimension_semantics=("arbitrary",), vmem_limit_bytes=VMEM_LIMIT),
        name=name,
    )(x, mod_all, gain, wg, wu, wd, win)


def _ka_prompt_call(x, mod_all, mod_spec, gain, wg, wu, wd, win, layer, name):
    g, r, _ = x.shape
    tm = TM_PROMPT
    tps = r // tm
    n_tiles = g * tps
    main = lambda s: jnp.minimum(s, n_tiles - 1)
    prev = lambda s: jnp.maximum(s - 1, 0)
    main_row = lambda w: pl.BlockSpec((None, tm, w), lambda s: (main(s) // tps, main(s) % tps, 0))
    prev_row = lambda w: pl.BlockSpec((None, tm, w), lambda s: (prev(s) // tps, prev(s) % tps, 0))
    last = lambda rows, w: pl.BlockSpec((None, rows, w), lambda s: (prev(s) // tps, 0, 0))
    return pl.pallas_call(
        functools.partial(_ka_prompt_kernel, layer=layer, tiles_per_seq=tps),
        out_shape=_ka_out_shapes(g, r) + (jax.ShapeDtypeStruct((g, WINDOW, KV_WIDTH), F32),
                                          jax.ShapeDtypeStruct((g, WINDOW, KV_WIDTH), F32),
                                          jax.ShapeDtypeStruct((g, HIST, POOL_WIDTH), F32)),
        grid=(n_tiles + 1,),
        in_specs=[main_row(D_MODEL), mod_spec] + _ka_weight_specs(layer),
        out_specs=(main_row(D_MODEL), prev_row(ATTN_WIDTH), prev_row(KV_WIDTH), prev_row(KV_WIDTH),
                   prev_row(POOL_WIDTH), last(WINDOW, KV_WIDTH), last(WINDOW, KV_WIDTH), last(HIST, POOL_WIDTH)),
        scratch_shapes=[pltpu.VMEM((tm, D_MODEL), BF16), pltpu.VMEM((tm, D_FF), BF16),
                        pltpu.VMEM((tm, D_MODEL), BF16)],
        compiler_params=pltpu.CompilerParams(dimension_semantics=("arbitrary",), vmem_limit_bytes=VMEM_LIMIT),
        name=name,
    )(x, mod_all, gain, wg, wu, wd, win)


def _low_half():
    return lax.broadcasted_iota(jnp.int32, (WINDOW, LANES), 1) < HEAD_DIM


def _attn_probs(qb, kk, bias_ref, sinks_ref, layer, valid, p_ref):
    lo = _low_half()
    zero = jnp.zeros((WINDOW, LANES), BF16)
    pieces = []
    for c in range(4):
        qc = qb[:, c * LANES:(c + 1) * LANES]
        pieces += [jnp.where(lo, qc, zero), jnp.where(lo, zero, qc)]
    qs = jnp.concatenate(pieces, axis=0)
    s = lax.dot_general(qs, kk, (((1,), (1,)), ((), ())), preferred_element_type=F32)
    for r in range(N_HEADS):
        rows = slice(r * WINDOW, (r + 1) * WINDOW)
        sr = jnp.where(valid, s[rows] + bias_ref[rows, :], NEG_INF)
        sink = sinks_ref[layer, _head_of_row(r)]
        m = jnp.maximum(jnp.max(sr, axis=-1, keepdims=True), sink)
        p = jnp.exp(sr - m)
        denom = jnp.sum(p, axis=-1, keepdims=True) + jnp.exp(sink - m)
        p_ref[rows, :] = (p / denom).astype(BF16)


def _attn_values(p_ref, vv):
    lo = _low_half()
    o = jnp.dot(p_ref[...], vv, preferred_element_type=F32)
    cols = [jnp.where(lo, o[(2 * c) * WINDOW:(2 * c + 1) * WINDOW], o[(2 * c + 1) * WINDOW:(2 * c + 2) * WINDOW])
            for c in range(4)]
    return jnp.concatenate(cols, axis=1)


def _pool_sums(u_ref, up_ref, uext_ref, pooled_ref, seq_tile):
    tm = u_ref.shape[0]
    uext_ref[0:HIST, :] = jnp.where(seq_tile == 0, 0.0, up_ref[...])
    uext_ref[HIST:, :] = u_ref[...]
    pos = seq_tile * tm + lax.broadcasted_iota(jnp.int32, (tm, POOL_GROUP_WIDTH), 0)
    for g, w in enumerate(POOL_SIZES):
        cols = slice(g * POOL_GROUP_WIDTH, (g + 1) * POOL_GROUP_WIDTH)
        ug = uext_ref[HIST:, cols]
        acc = ug
        for d in range(1, w):
            acc = acc + uext_ref[HIST - d:HIST - d + tm, cols]
        count = jnp.minimum(pos + 1, w).astype(F32)
        pooled_ref[:, cols] = (acc / count - ug).astype(BF16)


def _mix_prompt_steps(q_ref, k_ref, v_ref, kp_ref, vp_ref, bias_ref, sinks_ref, poolw_ref,
                      pscale_ref, mixed_ref, pooled_ref, p_ref, layer, seq_tile):
    tm = q_ref.shape[0]
    nb = tm // WINDOW
    first = seq_tile == 0

    def keys_values(ref, prev_ref, j):
        prev = prev_ref[...] if j == 0 else ref[(j - 1) * WINDOW:j * WINDOW, :]
        return jnp.concatenate([prev, ref[j * WINDOW:(j + 1) * WINDOW, :]], axis=0).astype(BF16)

    def probs(j):
        qi = lax.broadcasted_iota(jnp.int32, (WINDOW, 2 * WINDOW), 0)
        kj = lax.broadcasted_iota(jnp.int32, (WINDOW, 2 * WINDOW), 1)
        dist = qi + WINDOW - kj
        valid = (dist >= 0) & (dist <= WINDOW)
        if j == 0:
            valid = valid & (kj >= jnp.where(first, WINDOW, 0))
        _attn_probs(q_ref[j * WINDOW:(j + 1) * WINDOW, :], keys_values(k_ref, kp_ref, j), bias_ref, sinks_ref,
                    layer, valid, p_ref)

    def values(j):
        a = _attn_values(p_ref, keys_values(v_ref, vp_ref, j))
        mixed_ref[j * WINDOW:(j + 1) * WINDOW, 0:ATTN_WIDTH] = a.astype(BF16)

    def pool_map(g):
        cols = slice(g * POOL_GROUP_WIDTH, (g + 1) * POOL_GROUP_WIDTH)
        pg = jnp.dot(pooled_ref[:, cols], poolw_ref[g], preferred_element_type=F32) * pscale_ref[layer:layer + 1, cols]
        mixed_ref[:, ATTN_WIDTH + g * POOL_GROUP_WIDTH:ATTN_WIDTH + (g + 1) * POOL_GROUP_WIDTH] = pg.astype(BF16)

    assert nb == len(POOL_SIZES)

    def stage(i):
        def run():
            if i > 0:
                values(i - 1)
                pool_map(i - 1)
            if i < nb:
                probs(i)
        return run

    return [stage(i) for i in range(nb + 1)]


def _out_projection(x1, mixed_ref, wout_ref, mod, gain_ref, layer):
    x2 = x1 + mod(5) * jnp.dot(mixed_ref[...], wout_ref[...], preferred_element_type=F32)
    return x2, _rms_mod(x2, gain_ref[layer, 2:3, :], mod(6), mod(7)).astype(BF16)


def _ffn2_out(x2, y, mod, fgain_ref, final):
    x3 = x2 + 0.5 * mod(8) * y
    if final:
        x3 = x3 * lax.rsqrt(jnp.mean(x3 * x3, axis=-1, keepdims=True) + EPS) * fgain_ref[...]
    return x3


def _kb_kernel(sinks_ref, x1_ref, q_ref, k_ref, v_ref, kp_ref, vp_ref, u_ref, up_ref, un_ref, upn_ref,
               mod_ref, gain_ref, bias_ref, poolw_ref, pscale_ref, wout_ref, wg_ref, wu_ref, wd_ref, fgain_ref,
               o_ref, mixed_ref, h_ref, x2_ref, act_ref, uext_ref, pooled_ref, p_ref,
               *, layer, final, tiles_per_seq):
    s = pl.program_id(0)
    n_tiles = pl.num_programs(0) - 1
    mix_tile = jnp.minimum(s, n_tiles - 1)
    next_tile = jnp.minimum(s + 1, n_tiles - 1)
    ffn_tile = jnp.maximum(s - 1, 0)
    slot = s % 2

    @pl.when(s == 0)
    def _():
        h_ref[...] = jnp.zeros_like(h_ref)
        x2_ref[1] = jnp.zeros(x2_ref.shape[1:], F32)
        _pool_sums(u_ref, up_ref, uext_ref, pooled_ref, 0)

    mix_steps = _mix_prompt_steps(q_ref, k_ref, v_ref, kp_ref, vp_ref, bias_ref, sinks_ref, poolw_ref, pscale_ref,
                                  mixed_ref, pooled_ref, p_ref, layer, mix_tile % tiles_per_seq)

    def finish_mix():
        x2, h = _out_projection(x1_ref[...], mixed_ref, wout_ref, _mod_getter(mod_ref, mix_tile // tiles_per_seq),
                                gain_ref, layer)
        x2_ref[slot] = x2
        h_ref[...] = h
        _pool_sums(un_ref, upn_ref, uext_ref, pooled_ref, next_tile % tiles_per_seq)

    y = _swiglu(h_ref, act_ref, wg_ref, wu_ref, wd_ref, side_work=mix_steps, before_down=finish_mix)
    o_ref[...] = _ffn2_out(x2_ref[1 - slot], y, _mod_getter(mod_ref, ffn_tile // tiles_per_seq), fgain_ref, final)


def _kb_call(sinks, x1, q, k, v, u, mod_all, mod_spec, gain, bias, poolw, pscale, wout, wg, wu, wd, fgain,
             layer, final, name):
    g, r, _ = x1.shape
    tm = TM_PROMPT
    tps = r // tm
    n_tiles = g * tps
    nb = tm // WINDOW
    nh = tm // HIST
    mix = lambda s: jnp.minimum(s, n_tiles - 1)
    tail = lambda s: jnp.maximum(s - 1, 0)
    mix_row = lambda w: pl.BlockSpec((None, tm, w), lambda s: (mix(s) // tps, mix(s) % tps, 0))
    tail_row = lambda w: pl.BlockSpec((None, tm, w), lambda s: (tail(s) // tps, tail(s) % tps, 0))
    prev_kv = pl.BlockSpec((None, WINDOW, KV_WIDTH),
                           lambda s: (mix(s) // tps, jnp.maximum((mix(s) % tps) * nb - 1, 0), 0))
    nxt = lambda s: jnp.minimum(s + 1, n_tiles - 1)
    first_u = pl.BlockSpec((None, tm, POOL_WIDTH), lambda s: (0, 0, 0))
    first_hist = pl.BlockSpec((None, HIST, POOL_WIDTH), lambda s: (0, 0, 0))
    next_u = pl.BlockSpec((None, tm, POOL_WIDTH), lambda s: (nxt(s) // tps, nxt(s) % tps, 0))
    next_hist = pl.BlockSpec((None, HIST, POOL_WIDTH),
                             lambda s: (nxt(s) // tps, jnp.maximum((nxt(s) % tps) * nh - 1, 0), 0))
    return pl.pallas_call(
        functools.partial(_kb_kernel, layer=layer, final=final, tiles_per_seq=tps),
        out_shape=jax.ShapeDtypeStruct((g, r, D_MODEL), F32),
        grid=(n_tiles + 1,),
        in_specs=[_SMEM, mix_row(D_MODEL), mix_row(ATTN_WIDTH), mix_row(KV_WIDTH), mix_row(KV_WIDTH),
                  prev_kv, prev_kv, first_u, first_hist, next_u, next_hist, mod_spec, _full((DEPTH, 3, D_MODEL)),
                  _full((N_HEADS * WINDOW, 2 * WINDOW)),
                  _layer_resident((len(POOL_SIZES), POOL_GROUP_WIDTH, POOL_GROUP_WIDTH), layer),
                  _full((DEPTH, POOL_WIDTH)),
                  _layer_resident((D_MODEL, D_MODEL), layer), _layer_resident((D_MODEL, D_FF), layer),
                  _layer_resident((D_MODEL, D_FF), layer), _layer_resident((D_FF, D_MODEL), layer),
                  _full((1, D_MODEL))],
        out_specs=tail_row(D_MODEL),
        scratch_shapes=[pltpu.VMEM((tm, D_MODEL), BF16), pltpu.VMEM((tm, D_MODEL), BF16),
                        pltpu.VMEM((2, tm, D_MODEL), F32), pltpu.VMEM((tm, D_FF), BF16),
                        pltpu.VMEM((HIST + tm, POOL_WIDTH), F32), pltpu.VMEM((tm, POOL_WIDTH), BF16),
                        pltpu.VMEM((N_HEADS * WINDOW, 2 * WINDOW), BF16)],
        compiler_params=pltpu.CompilerParams(
            dimension_semantics=("arbitrary",), vmem_limit_bytes=VMEM_LIMIT),
        name=name,
    )(sinks, x1, q, k, v, k, v, u, u, u, u, mod_all, gain, bias, poolw, pscale, wout, wg, wu, wd, fgain)


def _ks_kernel(sinks_ref, q_ref, kn_ref, vn_ref, ck_ref, cv_ref, bias_ref, o_ref,
               qs_ref, knr_ref, vnr_ref, s_ref, p_ref, acc_ref, *, layer):
    bt = q_ref.shape[0]
    rows = bt * N_HEADS
    nt = (((1,), (1,)), ((), ()))
    head_rows = lambda r: pl.ds(r, bt, stride=N_HEADS)
    lo = lax.broadcasted_iota(jnp.int32, (bt, LANES), 1) < HEAD_DIM
    kn = kn_ref[...].astype(BF16).astype(F32)
    vn = vn_ref[...].astype(BF16).astype(F32)
    for c in range(4):
        qc = q_ref[:, c * LANES:(c + 1) * LANES].astype(F32)
        qs_ref[head_rows(2 * c), :] = jnp.where(lo, qc, 0.0)
        qs_ref[head_rows(2 * c + 1), :] = jnp.where(lo, 0.0, qc)
    for r in range(N_HEADS):
        knr_ref[head_rows(r), :] = kn
        vnr_ref[head_rows(r), :] = vn

    for b in range(bt):
        rb = slice(b * N_HEADS, (b + 1) * N_HEADS)
        s_ref[rb, :] = lax.dot_general(qs_ref[rb, :].astype(BF16), ck_ref[b].astype(BF16), nt,
                                       preferred_element_type=F32)

    head = jnp.bitwise_and(lax.broadcasted_iota(jnp.int32, (rows, 1), 0), N_HEADS - 1)
    sink = jnp.zeros((rows, 1), F32)
    for r in range(N_HEADS):
        sink = jnp.where(head == r, sinks_ref[layer, _head_of_row(r)], sink)
    bias = jnp.concatenate([bias_ref[...]] * bt, axis=0)
    s_c = s_ref[...] + bias[:, 0:WINDOW]
    s_n = jnp.sum(qs_ref[...] * knr_ref[...], axis=-1, keepdims=True) + bias[:, WINDOW:WINDOW + 1]
    m = jnp.maximum(jnp.maximum(jnp.max(s_c, axis=-1, keepdims=True), s_n), sink)
    p_c = jnp.exp(s_c - m)
    p_n = jnp.exp(s_n - m)
    denom = jnp.sum(p_c, axis=-1, keepdims=True) + p_n + jnp.exp(sink - m)
    p_ref[...] = p_c / denom

    for b in range(bt):
        rb = slice(b * N_HEADS, (b + 1) * N_HEADS)
        acc_ref[rb, :] = jnp.dot(p_ref[rb, :].astype(BF16), cv_ref[b].astype(BF16), preferred_element_type=F32)
    acc_ref[...] = acc_ref[...] + (p_n / denom).astype(BF16).astype(F32) * vnr_ref[...]

    for c in range(4):
        o_ref[:, c * LANES:(c + 1) * LANES] = jnp.where(
            lo, acc_ref[head_rows(2 * c), :], acc_ref[head_rows(2 * c + 1), :]).astype(BF16)


def _ks_call(sinks, q, kn, vn, cache_k, cache_v, layer, bias_s):
    nb = q.shape[0]
    bt = 32
    rows = bt * N_HEADS
    per_b = lambda w: pl.BlockSpec((bt, w), lambda i: (i, 0))
    cache = pl.BlockSpec((None, bt, WINDOW, KV_WIDTH), lambda i: (layer, i, 0, 0))
    return pl.pallas_call(
        functools.partial(_ks_kernel, layer=layer),
        out_shape=jax.ShapeDtypeStruct((nb, ATTN_WIDTH), BF16),
        grid=(nb // bt,),
        in_specs=[_SMEM, per_b(ATTN_WIDTH), per_b(KV_WIDTH), per_b(KV_WIDTH), cache, cache,
                  _full((N_HEADS, 2 * WINDOW))],
        out_specs=per_b(ATTN_WIDTH),
        scratch_shapes=[pltpu.VMEM((rows, KV_WIDTH), F32), pltpu.VMEM((rows, KV_WIDTH), F32),
                        pltpu.VMEM((rows, KV_WIDTH), F32), pltpu.VMEM((rows, WINDOW), F32),
                        pltpu.VMEM((rows, WINDOW), F32), pltpu.VMEM((rows, KV_WIDTH), F32)],
        compiler_params=pltpu.CompilerParams(dimension_semantics=("arbitrary",), vmem_limit_bytes=VMEM_LIMIT),
        name="sample_attention",
    )(sinks, q, kn, vn, cache_k, cache_v, bias_s)


def _kc_kernel(x1_ref, attn_ref, u_ref, hist_ref, mod_ref, gain_ref, poolw_ref, pscale_ref,
               wout_ref, wg_ref, wu_ref, wd_ref, fgain_ref, o_ref, mixed_ref, h_ref, act_ref, *, layer, final):
    mixed_ref[:, 0:ATTN_WIDTH] = attn_ref[...]
    for g, w in enumerate(POOL_SIZES):
        cols = slice(g * POOL_GROUP_WIDTH, (g + 1) * POOL_GROUP_WIDTH)
        ug = u_ref[:, cols]
        acc = ug
        for d in range(1, w):
            r = POOL_STATE - d
            acc = acc + hist_ref[:, r * POOL_WIDTH + g * POOL_GROUP_WIDTH:r * POOL_WIDTH + (g + 1) * POOL_GROUP_WIDTH]
        pooled = (acc / float(w) - ug).astype(BF16)
        pg = jnp.dot(pooled, poolw_ref[g], preferred_element_type=F32) * pscale_ref[layer:layer + 1, cols]
        mixed_ref[:, ATTN_WIDTH + g * POOL_GROUP_WIDTH:ATTN_WIDTH + (g + 1) * POOL_GROUP_WIDTH] = pg.astype(BF16)
    mod = _mod_getter(mod_ref)
    x2, h = _out_projection(x1_ref[...], mixed_ref, wout_ref, mod, gain_ref, layer)
    h_ref[...] = h
    o_ref[...] = _ffn2_out(x2, _swiglu(h_ref, act_ref, wg_ref, wu_ref, wd_ref), mod, fgain_ref, final)


def _kc_call(x1, attn, u, hist, mod_all, mod_spec, gain, poolw, pscale, wout, wg, wu, wd, fgain, layer, final, name):
    r = x1.shape[0]
    return pl.pallas_call(
        functools.partial(_kc_kernel, layer=layer, final=final),
        out_shape=jax.ShapeDtypeStruct((r, D_MODEL), F32),
        grid=(1,),
        in_specs=[_full((r, D_MODEL)), _full((r, ATTN_WIDTH)), _full((r, POOL_WIDTH)),
                  pl.BlockSpec((None, r, POOL_STATE * POOL_WIDTH), lambda *_: (layer, 0, 0)),
                  mod_spec, _full((DEPTH, 3, D_MODEL)),
                  _layer_resident((len(POOL_SIZES), POOL_GROUP_WIDTH, POOL_GROUP_WIDTH), layer),
                  _full((DEPTH, POOL_WIDTH)),
                  _layer_resident((D_MODEL, D_MODEL), layer), _layer_resident((D_MODEL, D_FF), layer),
                  _layer_resident((D_MODEL, D_FF), layer), _layer_resident((D_FF, D_MODEL), layer),
                  _full((1, D_MODEL))],
        out_specs=_full((r, D_MODEL)),
        scratch_shapes=[pltpu.VMEM((r, D_MODEL), BF16), pltpu.VMEM((r, D_MODEL), BF16),
                        pltpu.VMEM((r, D_FF), BF16)],
        compiler_params=pltpu.CompilerParams(dimension_semantics=("arbitrary",), vmem_limit_bytes=VMEM_LIMIT),
        name=name,
    )(x1, attn, u, hist, mod_all, gain, poolw, pscale, wout, wg, wu, wd, fgain)


def _cache_shift_kernel(ck_ref, cv_ref, kn_ref, vn_ref, ok_ref, ov_ref, sems):
    def copies(c_ref, n_ref, o_ref, first_sem):
        return [pltpu.make_async_copy(c_ref.at[:, :, pl.ds(1, WINDOW - 1)], o_ref.at[:, :, pl.ds(0, WINDOW - 1)],
                                      sems.at[first_sem]),
                pltpu.make_async_copy(n_ref, o_ref.at[:, :, pl.ds(WINDOW - 1, 1)], sems.at[first_sem + 1])]

    all_copies = copies(ck_ref, kn_ref, ok_ref, 0) + copies(cv_ref, vn_ref, ov_ref, 2)
    for c in all_copies:
        c.start()
    for c in all_copies:
        c.wait()


def _cache_shift(cache_k, cache_v, k_new, v_new):
    any_spec = pl.BlockSpec(memory_space=pl.ANY)
    return pl.pallas_call(
        _cache_shift_kernel,
        out_shape=(jax.ShapeDtypeStruct(cache_k.shape, cache_k.dtype),
                   jax.ShapeDtypeStruct(cache_v.shape, cache_v.dtype)),
        in_specs=[any_spec] * 4,
        out_specs=(any_spec, any_spec),
        scratch_shapes=[pltpu.SemaphoreType.DMA((4,))],
        name="cache_shift",
    )(cache_k, cache_v, k_new, v_new)


def kernel(x_prompt, x_sample, c_prompt, c_sample, cache_k, cache_v, state_pool, w_ada, b_ada, norm_gain,
           w_in, w_out, sinks, rel_bias, pool_w, pool_scale, ffn1_wg, ffn1_wu, ffn1_wd, ffn2_wg, ffn2_wu,
           ffn2_wd, final_gain):
    n_p, seq, _ = x_prompt.shape
    n_s = x_sample.shape[0]

    mod_all = _modulation(jnp.concatenate([c_sample, c_prompt], axis=0), w_ada, b_ada)
    bias = _bias_table(rel_bias)
    bias_s = bias.reshape(N_HEADS, WINDOW, 2 * WINDOW)[:, 0, :]
    fgain = final_gain.reshape(1, D_MODEL)

    wq = (w_in[:, :, :ATTN_WIDTH].reshape(DEPTH, D_MODEL, 2, 4, HEAD_DIM).transpose(0, 1, 3, 2, 4)
          .reshape(DEPTH, D_MODEL, ATTN_WIDTH)) * (HEAD_DIM ** -0.5)
    win = jnp.concatenate([wq, w_in[:, :, ATTN_WIDTH:]], axis=2).astype(BF16)
    wo_attn = (w_out[:, :ATTN_WIDTH].reshape(DEPTH, 2, 4, HEAD_DIM, D_MODEL).transpose(0, 2, 1, 3, 4)
               .reshape(DEPTH, ATTN_WIDTH, D_MODEL))
    wout = jnp.concatenate([wo_attn, w_out[:, ATTN_WIDTH:]], axis=1).astype(BF16)
    wg1, wu1, wd1 = ffn1_wg.astype(BF16), ffn1_wu.astype(BF16), ffn1_wd.astype(BF16)
    wg2, wu2, wd2 = ffn2_wg.astype(BF16), ffn2_wu.astype(BF16), ffn2_wd.astype(BF16)
    poolw = pool_w.astype(BF16)

    xp = x_prompt
    xs = x_sample.reshape(n_s, D_MODEL)
    ck = cache_k.reshape(DEPTH, n_s, WINDOW, KV_WIDTH)
    cv = cache_v.reshape(DEPTH, n_s, WINDOW, KV_WIDTH)
    hist = state_pool.reshape(DEPTH, n_s, POOL_STATE * POOL_WIDTH)

    new_kp, new_vp, new_pp, new_ks, new_vs, new_ps = [], [], [], [], [], []
    for l in range(DEPTH):
        final = l == DEPTH - 1
        mod_p = pl.BlockSpec((None, n_p, MOD_WIDTH), lambda *_, l=l: (l, n_s // n_p, 0))
        mod_s = pl.BlockSpec((None, n_s, MOD_WIDTH), lambda *_, l=l: (l, 0, 0))

        x1, q, k, v, u, k_last, v_last, u_last = _ka_prompt_call(
            xp, mod_all, mod_p, norm_gain, wg1, wu1, wd1, win, l, f"prompt_ffn1_inproj_l{l}")
        xp = _kb_call(sinks, x1, q, k, v, u, mod_all, mod_p, norm_gain, bias, poolw, pool_scale, wout,
                      wg2, wu2, wd2, fgain, l, final, f"prompt_mixer_ffn2_l{l}")
        new_kp.append(k_last)
        new_vp.append(v_last)
        new_pp.append(u_last)

        x1s, qs, ks, vs, us = _ka_sample_call(xs, mod_all, mod_s, norm_gain, wg1, wu1, wd1, win, l,
                                              f"sample_ffn1_inproj_l{l}")
        attn_s = _ks_call(sinks, qs, ks, vs, ck, cv, l, bias_s)
        xs = _kc_call(x1s, attn_s, us, hist, mod_all, mod_s, norm_gain, poolw, pool_scale, wout, wg2, wu2, wd2,
                      fgain, l, final, f"sample_mixer_ffn2_l{l}")
        new_ks.append(ks)
        new_vs.append(vs)
        new_ps.append(us)

    row = lambda rows, shape: jnp.stack(rows).reshape((DEPTH, n_s, 1) + shape)
    new_k_sample, new_v_sample = _cache_shift(cache_k, cache_v, row(new_ks, (N_KV_HEADS, HEAD_DIM)),
                                              row(new_vs, (N_KV_HEADS, HEAD_DIM)))
    new_pool_sample = jnp.concatenate([state_pool[:, :, 1:], row(new_ps, (POOL_WIDTH,))], axis=2)
    kv_shape = (DEPTH, n_p, WINDOW, N_KV_HEADS, HEAD_DIM)
    return (xp, xs.reshape(n_s, 1, D_MODEL), jnp.stack(new_kp).reshape(kv_shape), jnp.stack(new_vp).reshape(kv_shape),
            jnp.stack(new_pp)[:, :, HIST - POOL_STATE:], new_k_sample, new_v_sample, new_pool_sample)
```

```python
import functools
import math

import numpy as np
import jax
import jax.numpy as jnp
from jax import lax
from jax.experimental import pallas as pl
from jax.experimental.pallas import tpu as pltpu

D_MODEL = 1024
DEPTH = 2
HEAD_DIM = 64
N_HEADS = 8
N_KV_HEADS = 2
ATTN_WIDTH = N_HEADS * HEAD_DIM
KV_WIDTH = N_KV_HEADS * HEAD_DIM
WINDOW = 128
POOL_SIZES = (2, 4, 8, 16)
POOL_GROUP_WIDTH = 128
POOL_WIDTH = len(POOL_SIZES) * POOL_GROUP_WIDTH
POOL_STATE = max(POOL_SIZES) - 1
IN_WIDTH = ATTN_WIDTH + 2 * KV_WIDTH + POOL_WIDTH
D_FF = 2816
N_BUCKETS = 32
MAX_DISTANCE = 128
N_MOD = 9
MOD_WIDTH = N_MOD * D_MODEL
EPS = 1e-6

LANES = 128
SUBLANES = 8
MXU_COLS = 256
FF_CHUNK = 2 * MXU_COLS
FF_CHUNKS = tuple((c, min(FF_CHUNK, D_FF - c)) for c in range(0, D_FF, FF_CHUNK))
TM_PROMPT = 512
HIST = 16
VMEM_LIMIT = 56 * 1024 * 1024

F32 = jnp.float32
BF16 = jnp.bfloat16
NEG_INF = float("-inf")


def _head_of_row(r):
    return (r // 2) + 4 * (r % 2)


def _t5_bucket_np(dist):
    n = np.maximum(dist, 0)
    max_exact = N_BUCKETS // 2
    nf = np.maximum(n, 1).astype(np.float32)
    large = max_exact + (np.log(nf / max_exact) / math.log(MAX_DISTANCE / max_exact)
                         * (N_BUCKETS - max_exact)).astype(np.int32)
    large = np.minimum(large, N_BUCKETS - 1)
    return np.where(n < max_exact, n, large).astype(np.int32)


def _bucket_table():
    dist = np.arange(WINDOW)[:, None] + WINDOW - np.arange(2 * WINDOW)[None, :]
    return _t5_bucket_np(dist)


def _full(shape):
    return pl.BlockSpec(shape, lambda *_: (0,) * len(shape))


def _layer_resident(shape, layer):
    return pl.BlockSpec((None,) + tuple(shape), lambda *_: (layer,) + (0,) * len(shape),
                        pipeline_mode=pl.Buffered(1))


_SMEM = pl.BlockSpec(memory_space=pltpu.SMEM)


def _rms_mod(x, gain, shift, scale):
    y = x * lax.rsqrt(jnp.mean(x * x, axis=-1, keepdims=True) + EPS)
    return (y * gain) * (1.0 + scale) + shift


def _swiglu(h_ref, act_ref, wg_ref, wu_ref, wd_ref, side_work=(), before_down=None):
    n = len(FF_CHUNKS)
    assert len(side_work) <= n
    after_chunk = {((j + 1) * n) // (len(side_work) + 1) - 1: work for j, work in enumerate(side_work)}
    for i, (c0, cw) in enumerate(FF_CHUNKS):
        g = jnp.dot(h_ref[...], wg_ref[:, c0:c0 + cw], preferred_element_type=F32)
        u = jnp.dot(h_ref[...], wu_ref[:, c0:c0 + cw], preferred_element_type=F32)
        act_ref[:, c0:c0 + cw] = ((g / (1.0 + jnp.exp(-g))) * u).astype(BF16)
        if i in after_chunk:
            after_chunk[i]()
    if before_down is not None:
        before_down()
    return jnp.dot(act_ref[...], wd_ref[...], preferred_element_type=F32)


def _mod_getter(mod_ref, row=None):
    rows = slice(None) if row is None else pl.ds(row, 1)
    return lambda k: mod_ref[rows, k * D_MODEL:(k + 1) * D_MODEL]


def _mod_kernel(c_ref, w_ref, b_ref, o_ref):
    c = c_ref[...]
    a = (c / (1.0 + jnp.exp(-c))).astype(BF16)
    o_ref[...] = jnp.dot(a, w_ref[...].astype(BF16), preferred_element_type=F32) + b_ref[...]


def _modulation(c_all, w_ada, b_ada):
    rows = c_all.shape[0]
    n_chunk = 1024
    return pl.pallas_call(
        _mod_kernel,
        out_shape=jax.ShapeDtypeStruct((DEPTH, rows, MOD_WIDTH), F32),
        grid=(DEPTH, MOD_WIDTH // n_chunk),
        in_specs=[
            pl.BlockSpec((rows, D_MODEL), lambda l, j: (0, 0)),
            pl.BlockSpec((None, D_MODEL, n_chunk), lambda l, j: (l, 0, j)),
            pl.BlockSpec((None, 1, n_chunk), lambda l, j: (l, 0, j)),
        ],
        out_specs=pl.BlockSpec((None, rows, n_chunk), lambda l, j: (l, 0, j)),
        compiler_params=pltpu.CompilerParams(
            dimension_semantics=("arbitrary", "arbitrary"), vmem_limit_bytes=VMEM_LIMIT),
        name="adaln_modulation",
    )(c_all, w_ada, b_ada.reshape(DEPTH, 1, MOD_WIDTH))


def _bias_kernel(rb_ref, bucket_ref, o_ref):
    bucket = bucket_ref[...]
    eq = [bucket == b for b in range(N_BUCKETS)]
    for r in range(N_HEADS):
        acc = jnp.zeros((WINDOW, 2 * WINDOW), F32)
        for b in range(N_BUCKETS):
            acc = jnp.where(eq[b], rb_ref[b, _head_of_row(r)], acc)
        o_ref[r * WINDOW:(r + 1) * WINDOW, :] = acc


def _bias_table(rel_bias):
    return pl.pallas_call(
        _bias_kernel,
        out_shape=jax.ShapeDtypeStruct((N_HEADS * WINDOW, 2 * WINDOW), F32),
        in_specs=[_SMEM, pl.BlockSpec(memory_space=pltpu.VMEM)],
        out_specs=pl.BlockSpec(memory_space=pltpu.VMEM),
        name="rel_bias_table",
    )(rel_bias, jnp.asarray(_bucket_table()))


def _project_q(hm, win_ref, q_ref):
    q_ref[...] = jnp.dot(hm, win_ref[:, :ATTN_WIDTH], preferred_element_type=F32).astype(BF16)


def _project_kvu(hm, win_ref, k_ref, v_ref, u_ref):
    z = jnp.dot(hm, win_ref[:, ATTN_WIDTH:], preferred_element_type=F32)
    k_ref[...] = z[:, :KV_WIDTH]
    v_ref[...] = z[:, KV_WIDTH:2 * KV_WIDTH]
    u_ref[...] = z[:, 2 * KV_WIDTH:]
    return z


def _ffn1(x_ref, mod, gain_ref, wg_ref, wu_ref, wd_ref, x1_ref, h_ref, act_ref, layer, before_norm=None):
    x = x_ref[...]
    h_ref[...] = _rms_mod(x, gain_ref[layer, 0:1, :], mod(0), mod(1)).astype(BF16)
    x1 = x + 0.5 * mod(2) * _swiglu(h_ref, act_ref, wg_ref, wu_ref, wd_ref)
    x1_ref[...] = x1
    if before_norm is not None:
        before_norm()
    return _rms_mod(x1, gain_ref[layer, 1:2, :], mod(3), mod(4)).astype(BF16)


def _ka_sample_kernel(x_ref, mod_ref, gain_ref, wg_ref, wu_ref, wd_ref, win_ref,
                      x1_ref, q_ref, k_ref, v_ref, u_ref, h_ref, act_ref, *, layer):
    hm = _ffn1(x_ref, _mod_getter(mod_ref), gain_ref, wg_ref, wu_ref, wd_ref, x1_ref, h_ref, act_ref, layer)
    _project_q(hm, win_ref, q_ref)
    _project_kvu(hm, win_ref, k_ref, v_ref, u_ref)


def _ka_prompt_kernel(x_ref, mod_ref, gain_ref, wg_ref, wu_ref, wd_ref, win_ref,
                      x1_ref, q_ref, k_ref, v_ref, u_ref, kl_ref, vl_ref, ul_ref,
                      h_ref, act_ref, hm_ref, *, layer, tiles_per_seq):
    tm = x_ref.shape[0]
    s = pl.program_id(0)
    main = jnp.minimum(s, pl.num_programs(0) - 2)

    @pl.when(s == 0)
    def _():
        hm_ref[...] = jnp.zeros_like(hm_ref)

    _project_q(hm_ref[...], win_ref, q_ref)

    def project_rest():
        z = _project_kvu(hm_ref[...], win_ref, k_ref, v_ref, u_ref)
        kl_ref[...] = z[tm - WINDOW:, :KV_WIDTH]
        vl_ref[...] = z[tm - WINDOW:, KV_WIDTH:2 * KV_WIDTH]
        ul_ref[...] = z[tm - HIST:, 2 * KV_WIDTH:]

    mod = _mod_getter(mod_ref, main // tiles_per_seq)
    hm = _ffn1(x_ref, mod, gain_ref, wg_ref, wu_ref, wd_ref, x1_ref, h_ref, act_ref, layer,
               before_norm=project_rest)
    hm_ref[...] = hm


def _ka_out_shapes(g, r):
    return (jax.ShapeDtypeStruct((g, r, D_MODEL), F32),
            jax.ShapeDtypeStruct((g, r, ATTN_WIDTH), BF16),
            jax.ShapeDtypeStruct((g, r, KV_WIDTH), F32),
            jax.ShapeDtypeStruct((g, r, KV_WIDTH), F32),
            jax.ShapeDtypeStruct((g, r, POOL_WIDTH), F32))


def _ka_weight_specs(layer):
    return [_full((DEPTH, 3, D_MODEL)),
            _layer_resident((D_MODEL, D_FF), layer), _layer_resident((D_MODEL, D_FF), layer),
            _layer_resident((D_FF, D_MODEL), layer), _layer_resident((D_MODEL, IN_WIDTH), layer)]


def _ka_sample_call(x, mod_all, mod_spec, gain, wg, wu, wd, win, layer, name):
    r = x.shape[0]
    row = lambda w: _full((r, w))
    return pl.pallas_call(
        functools.partial(_ka_sample_kernel, layer=layer),
        out_shape=tuple(jax.ShapeDtypeStruct(s.shape[1:], s.dtype) for s in _ka_out_shapes(1, r)),
        grid=(1,),
        in_specs=[row(D_MODEL), mod_spec] + _ka_weight_specs(layer),
        out_specs=(row(D_MODEL), row(ATTN_WIDTH), row(KV_WIDTH), row(KV_WIDTH), row(POOL_WIDTH)),
        scratch_shapes=[pltpu.VMEM((r, D_MODEL), BF16), pltpu.VMEM((r, D_FF), BF16)],
        compiler_params=pltpu.CompilerParams(dimension_semantics=("arbitrary",), vmem_limit_bytes=VMEM_LIMIT),
        name=name,
    )(x, mod_all, gain, wg, wu, wd, win)


def _ka_prompt_call(x, mod_all, mod_spec, gain, wg, wu, wd, win, layer, name):
    g, r, _ = x.shape
    tm = TM_PROMPT
    tps = r // tm
    n_tiles = g * tps
    main = lambda s: jnp.minimum(s, n_tiles - 1)
    prev = lambda s: jnp.maximum(s - 1, 0)
    main_row = lambda w: pl.BlockSpec((None, tm, w), lambda s: (main(s) // tps, main(s) % tps, 0))
    prev_row = lambda w: pl.BlockSpec((None, tm, w), lambda s: (prev(s) // tps, prev(s) % tps, 0))
    last = lambda rows, w: pl.BlockSpec((None, rows, w), lambda s: (prev(s) // tps, 0, 0))
    return pl.pallas_call(
        functools.partial(_ka_prompt_kernel, layer=layer, tiles_per_seq=tps),
        out_shape=_ka_out_shapes(g, r) + (jax.ShapeDtypeStruct((g, WINDOW, KV_WIDTH), F32),
                                          jax.ShapeDtypeStruct((g, WINDOW, KV_WIDTH), F32),
                                          jax.ShapeDtypeStruct((g, HIST, POOL_WIDTH), F32)),
        grid=(n_tiles + 1,),
        in_specs=[main_row(D_MODEL), mod_spec] + _ka_weight_specs(layer),
        out_specs=(main_row(D_MODEL), prev_row(ATTN_WIDTH), prev_row(KV_WIDTH), prev_row(KV_WIDTH),
                   prev_row(POOL_WIDTH), last(WINDOW, KV_WIDTH), last(WINDOW, KV_WIDTH), last(HIST, POOL_WIDTH)),
        scratch_shapes=[pltpu.VMEM((tm, D_MODEL), BF16), pltpu.VMEM((tm, D_FF), BF16),
                        pltpu.VMEM((tm, D_MODEL), BF16)],
        compiler_params=pltpu.CompilerParams(dimension_semantics=("arbitrary",), vmem_limit_bytes=VMEM_LIMIT),
        name=name,
    )(x, mod_all, gain, wg, wu, wd, win)


def _low_half():
    return lax.broadcasted_iota(jnp.int32, (WINDOW, LANES), 1) < HEAD_DIM


def _attn_probs(qb, kk, bias_ref, sinks_ref, layer, valid, p_ref):
    lo = _low_half()
    zero = jnp.zeros((WINDOW, LANES), BF16)
    pieces = []
    for c in range(4):
        qc = qb[:, c * LANES:(c + 1) * LANES]
        pieces += [jnp.where(lo, qc, zero), jnp.where(lo, zero, qc)]
    qs = jnp.concatenate(pieces, axis=0)
    s = lax.dot_general(qs, kk, (((1,), (1,)), ((), ())), preferred_element_type=F32)
    for r in range(N_HEADS):
        rows = slice(r * WINDOW, (r + 1) * WINDOW)
        sr = jnp.where(valid, s[rows] + bias_ref[rows, :], NEG_INF)
        sink = sinks_ref[layer, _head_of_row(r)]
        m = jnp.maximum(jnp.max(sr, axis=-1, keepdims=True), sink)
        p = jnp.exp(sr - m)
        denom = jnp.sum(p, axis=-1, keepdims=True) + jnp.exp(sink - m)
        p_ref[rows, :] = (p / denom).astype(BF16)


def _attn_values(p_ref, vv):
    lo = _low_half()
    o = jnp.dot(p_ref[...], vv, preferred_element_type=F32)
    cols = [jnp.where(lo, o[(2 * c) * WINDOW:(2 * c + 1) * WINDOW], o[(2 * c + 1) * WINDOW:(2 * c + 2) * WINDOW])
            for c in range(4)]
    return jnp.concatenate(cols, axis=1)


def _pool_sums(u_ref, up_ref, uext_ref, t1_ref, t2_ref, pooled_ref, seq_tile):
    tm = u_ref.shape[0]
    g_w = POOL_GROUP_WIDTH
    top = 2 * HIST
    end = top + tm
    assert POOL_SIZES == (2, 4, 8, 16) and HIST == 2 * SUBLANES
    uext_ref[0:HIST, :] = jnp.zeros((HIST, POOL_WIDTH), F32)
    uext_ref[HIST:top, :] = jnp.where(seq_tile == 0, 0.0, up_ref[...])
    uext_ref[top:, :] = u_ref[...]
    t1_ref[8:end, :] = uext_ref[8:end, :] + uext_ref[7:end - 1, :]
    t2_ref[16:end, :] = t1_ref[16:end, g_w:] + t1_ref[14:end - 2, g_w:]
    t1_ref[24:end, 2 * g_w:] = t2_ref[24:end, g_w:] + t2_ref[20:end - 4, g_w:]
    sums = [t1_ref[top:, 0:g_w], t2_ref[top:, 0:g_w], t1_ref[top:, 2 * g_w:3 * g_w],
            t1_ref[top:, 3 * g_w:] + t1_ref[top - 8:end - 8, 3 * g_w:]]
    pos = seq_tile * tm + lax.broadcasted_iota(jnp.int32, (tm, g_w), 0)
    for g, w in enumerate(POOL_SIZES):
        cols = slice(g * g_w, (g + 1) * g_w)
        count = jnp.minimum(pos + 1, w).astype(F32)
        pooled_ref[:, cols] = (sums[g] / count - uext_ref[top:, cols]).astype(BF16)


def _mix_prompt_steps(q_ref, k_ref, v_ref, kp_ref, vp_ref, bias_ref, sinks_ref, poolw_ref,
                      pscale_ref, mixed_ref, pooled_ref, p_ref, layer, seq_tile):
    tm = q_ref.shape[0]
    nb = tm // WINDOW
    first = seq_tile == 0

    def keys_values(ref, prev_ref, j):
        prev = prev_ref[...] if j == 0 else ref[(j - 1) * WINDOW:j * WINDOW, :]
        return jnp.concatenate([prev, ref[j * WINDOW:(j + 1) * WINDOW, :]], axis=0).astype(BF16)

    def probs(j):
        qi = lax.broadcasted_iota(jnp.int32, (WINDOW, 2 * WINDOW), 0)
        kj = lax.broadcasted_iota(jnp.int32, (WINDOW, 2 * WINDOW), 1)
        dist = qi + WINDOW - kj
        valid = (dist >= 0) & (dist <= WINDOW)
        if j == 0:
            valid = valid & (kj >= jnp.where(first, WINDOW, 0))
        _attn_probs(q_ref[j * WINDOW:(j + 1) * WINDOW, :], keys_values(k_ref, kp_ref, j), bias_ref, sinks_ref,
                    layer, valid, p_ref)

    def values(j):
        a = _attn_values(p_ref, keys_values(v_ref, vp_ref, j))
        mixed_ref[j * WINDOW:(j + 1) * WINDOW, 0:ATTN_WIDTH] = a.astype(BF16)

    def pool_map(g):
        cols = slice(g * POOL_GROUP_WIDTH, (g + 1) * POOL_GROUP_WIDTH)
        pg = jnp.dot(pooled_ref[:, cols], poolw_ref[g], preferred_element_type=F32) * pscale_ref[layer:layer + 1, cols]
        mixed_ref[:, ATTN_WIDTH + g * POOL_GROUP_WIDTH:ATTN_WIDTH + (g + 1) * POOL_GROUP_WIDTH] = pg.astype(BF16)

    assert nb == len(POOL_SIZES)

    def stage(i):
        def run():
            if i > 0:
                values(i - 1)
                pool_map(i - 1)
            if i < nb:
                probs(i)
        return run

    return [stage(i) for i in range(nb + 1)]


def _out_projection(x1, mixed_ref, wout_ref, mod, gain_ref, layer):
    x2 = x1 + mod(5) * jnp.dot(mixed_ref[...], wout_ref[...], preferred_element_type=F32)
    return x2, _rms_mod(x2, gain_ref[layer, 2:3, :], mod(6), mod(7)).astype(BF16)


def _ffn2_out(x2, y, mod, fgain_ref, final):
    x3 = x2 + 0.5 * mod(8) * y
    if final:
        x3 = x3 * lax.rsqrt(jnp.mean(x3 * x3, axis=-1, keepdims=True) + EPS) * fgain_ref[...]
    return x3


def _kb_kernel(sinks_ref, x1_ref, q_ref, k_ref, v_ref, kp_ref, vp_ref, u_ref, up_ref, un_ref, upn_ref,
               mod_ref, gain_ref, bias_ref, poolw_ref, pscale_ref, wout_ref, wg_ref, wu_ref, wd_ref, fgain_ref,
               o_ref, mixed_ref, h_ref, x2_ref, act_ref, uext_ref, t1_ref, t2_ref, pooled_ref, p_ref,
               *, layer, final, tiles_per_seq):
    s = pl.program_id(0)
    n_tiles = pl.num_programs(0) - 1
    mix_tile = jnp.minimum(s, n_tiles - 1)
    next_tile = jnp.minimum(s + 1, n_tiles - 1)
    ffn_tile = jnp.maximum(s - 1, 0)
    slot = s % 2

    @pl.when(s == 0)
    def _():
        h_ref[...] = jnp.zeros_like(h_ref)
        x2_ref[1] = jnp.zeros(x2_ref.shape[1:], F32)
        _pool_sums(u_ref, up_ref, uext_ref, t1_ref, t2_ref, pooled_ref, 0)

    mix_steps = _mix_prompt_steps(q_ref, k_ref, v_ref, kp_ref, vp_ref, bias_ref, sinks_ref, poolw_ref, pscale_ref,
                                  mixed_ref, pooled_ref, p_ref, layer, mix_tile % tiles_per_seq)

    def finish_mix():
        x2, h = _out_projection(x1_ref[...], mixed_ref, wout_ref, _mod_getter(mod_ref, mix_tile // tiles_per_seq),
                                gain_ref, layer)
        x2_ref[slot] = x2
        h_ref[...] = h
        _pool_sums(un_ref, upn_ref, uext_ref, t1_ref, t2_ref, pooled_ref, next_tile % tiles_per_seq)

    y = _swiglu(h_ref, act_ref, wg_ref, wu_ref, wd_ref, side_work=mix_steps, before_down=finish_mix)
    o_ref[...] = _ffn2_out(x2_ref[1 - slot], y, _mod_getter(mod_ref, ffn_tile // tiles_per_seq), fgain_ref, final)


def _kb_call(sinks, x1, q, k, v, u, mod_all, mod_spec, gain, bias, poolw, pscale, wout, wg, wu, wd, fgain,
             layer, final, name):
    g, r, _ = x1.shape
    tm = TM_PROMPT
    tps = r // tm
    n_tiles = g * tps
    nb = tm // WINDOW
    nh = tm // HIST
    mix = lambda s: jnp.minimum(s, n_tiles - 1)
    tail = lambda s: jnp.maximum(s - 1, 0)
    mix_row = lambda w: pl.BlockSpec((None, tm, w), lambda s: (mix(s) // tps, mix(s) % tps, 0))
    tail_row = lambda w: pl.BlockSpec((None, tm, w), lambda s: (tail(s) // tps, tail(s) % tps, 0))
    prev_kv = pl.BlockSpec((None, WINDOW, KV_WIDTH),
                           lambda s: (mix(s) // tps, jnp.maximum((mix(s) % tps) * nb - 1, 0), 0))
    nxt = lambda s: jnp.minimum(s + 1, n_tiles - 1)
    first_u = pl.BlockSpec((None, tm, POOL_WIDTH), lambda s: (0, 0, 0))
    first_hist = pl.BlockSpec((None, HIST, POOL_WIDTH), lambda s: (0, 0, 0))
    next_u = pl.BlockSpec((None, tm, POOL_WIDTH), lambda s: (nxt(s) // tps, nxt(s) % tps, 0))
    next_hist = pl.BlockSpec((None, HIST, POOL_WIDTH),
                             lambda s: (nxt(s) // tps, jnp.maximum((nxt(s) % tps) * nh - 1, 0), 0))
    return pl.pallas_call(
        functools.partial(_kb_kernel, layer=layer, final=final, tiles_per_seq=tps),
        out_shape=jax.ShapeDtypeStruct((g, r, D_MODEL), F32),
        grid=(n_tiles + 1,),
        in_specs=[_SMEM, mix_row(D_MODEL), mix_row(ATTN_WIDTH), mix_row(KV_WIDTH), mix_row(KV_WIDTH),
                  prev_kv, prev_kv, first_u, first_hist, next_u, next_hist, mod_spec, _full((DEPTH, 3, D_MODEL)),
                  _full((N_HEADS * WINDOW, 2 * WINDOW)),
                  _layer_resident((len(POOL_SIZES), POOL_GROUP_WIDTH, POOL_GROUP_WIDTH), layer),
                  _full((DEPTH, POOL_WIDTH)),
                  _layer_resident((D_MODEL, D_MODEL), layer), _layer_resident((D_MODEL, D_FF), layer),
                  _layer_resident((D_MODEL, D_FF), layer), _layer_resident((D_FF, D_MODEL), layer),
                  _full((1, D_MODEL))],
        out_specs=tail_row(D_MODEL),
        scratch_shapes=[pltpu.VMEM((tm, D_MODEL), BF16), pltpu.VMEM((tm, D_MODEL), BF16),
                        pltpu.VMEM((2, tm, D_MODEL), F32), pltpu.VMEM((tm, D_FF), BF16),
                        pltpu.VMEM((2 * HIST + tm, POOL_WIDTH), F32), pltpu.VMEM((2 * HIST + tm, POOL_WIDTH), F32),
                        pltpu.VMEM((2 * HIST + tm, POOL_WIDTH - POOL_GROUP_WIDTH), F32),
                        pltpu.VMEM((tm, POOL_WIDTH), BF16),
                        pltpu.VMEM((N_HEADS * WINDOW, 2 * WINDOW), BF16)],
        compiler_params=pltpu.CompilerParams(
            dimension_semantics=("arbitrary",), vmem_limit_bytes=VMEM_LIMIT),
        name=name,
    )(sinks, x1, q, k, v, k, v, u, u, u, u, mod_all, gain, bias, poolw, pscale, wout, wg, wu, wd, fgain)


def _ks_kernel(sinks_ref, q_ref, kn_ref, vn_ref, ck_ref, cv_ref, bias_ref, o_ref,
               qs_ref, knr_ref, vnr_ref, s_ref, p_ref, acc_ref, *, layer):
    bt = q_ref.shape[0]
    rows = bt * N_HEADS
    nt = (((1,), (1,)), ((), ()))
    head_rows = lambda r: pl.ds(r, bt, stride=N_HEADS)
    lo = lax.broadcasted_iota(jnp.int32, (bt, LANES), 1) < HEAD_DIM
    kn = kn_ref[...].astype(BF16).astype(F32)
    vn = vn_ref[...].astype(BF16).astype(F32)
    for c in range(4):
        qc = q_ref[:, c * LANES:(c + 1) * LANES].astype(F32)
        qs_ref[head_rows(2 * c), :] = jnp.where(lo, qc, 0.0)
        qs_ref[head_rows(2 * c + 1), :] = jnp.where(lo, 0.0, qc)
    for r in range(N_HEADS):
        knr_ref[head_rows(r), :] = kn
        vnr_ref[head_rows(r), :] = vn

    for b in range(bt):
        rb = slice(b * N_HEADS, (b + 1) * N_HEADS)
        s_ref[rb, :] = lax.dot_general(qs_ref[rb, :].astype(BF16), ck_ref[b].astype(BF16), nt,
                                       preferred_element_type=F32)

    head = jnp.bitwise_and(lax.broadcasted_iota(jnp.int32, (rows, 1), 0), N_HEADS - 1)
    sink = jnp.zeros((rows, 1), F32)
    for r in range(N_HEADS):
        sink = jnp.where(head == r, sinks_ref[layer, _head_of_row(r)], sink)
    bias = jnp.concatenate([bias_ref[...]] * bt, axis=0)
    s_c = s_ref[...] + bias[:, 0:WINDOW]
    s_n = jnp.sum(qs_ref[...] * knr_ref[...], axis=-1, keepdims=True) + bias[:, WINDOW:WINDOW + 1]
    m = jnp.maximum(jnp.maximum(jnp.max(s_c, axis=-1, keepdims=True), s_n), sink)
    p_c = jnp.exp(s_c - m)
    p_n = jnp.exp(s_n - m)
    denom = jnp.sum(p_c, axis=-1, keepdims=True) + p_n + jnp.exp(sink - m)
    p_ref[...] = p_c / denom

    for b in range(bt):
        rb = slice(b * N_HEADS, (b + 1) * N_HEADS)
        acc_ref[rb, :] = jnp.dot(p_ref[rb, :].astype(BF16), cv_ref[b].astype(BF16), preferred_element_type=F32)
    acc_ref[...] = acc_ref[...] + (p_n / denom).astype(BF16).astype(F32) * vnr_ref[...]

    for c in range(4):
        o_ref[:, c * LANES:(c + 1) * LANES] = jnp.where(
            lo, acc_ref[head_rows(2 * c), :], acc_ref[head_rows(2 * c + 1), :]).astype(BF16)


def _ks_call(sinks, q, kn, vn, cache_k, cache_v, layer, bias_s):
    nb = q.shape[0]
    bt = 32
    rows = bt * N_HEADS
    per_b = lambda w: pl.BlockSpec((bt, w), lambda i: (i, 0))
    cache = pl.BlockSpec((None, bt, WINDOW, KV_WIDTH), lambda i: (layer, i, 0, 0))
    return pl.pallas_call(
        functools.partial(_ks_kernel, layer=layer),
        out_shape=jax.ShapeDtypeStruct((nb, ATTN_WIDTH), BF16),
        grid=(nb // bt,),
        in_specs=[_SMEM, per_b(ATTN_WIDTH), per_b(KV_WIDTH), per_b(KV_WIDTH), cache, cache,
                  _full((N_HEADS, 2 * WINDOW))],
        out_specs=per_b(ATTN_WIDTH),
        scratch_shapes=[pltpu.VMEM((rows, KV_WIDTH), F32), pltpu.VMEM((rows, KV_WIDTH), F32),
                        pltpu.VMEM((rows, KV_WIDTH), F32), pltpu.VMEM((rows, WINDOW), F32),
                        pltpu.VMEM((rows, WINDOW), F32), pltpu.VMEM((rows, KV_WIDTH), F32)],
        compiler_params=pltpu.CompilerParams(dimension_semantics=("arbitrary",), vmem_limit_bytes=VMEM_LIMIT),
        name="sample_attention",
    )(sinks, q, kn, vn, cache_k, cache_v, bias_s)


def _kc_kernel(x1_ref, attn_ref, u_ref, hist_ref, mod_ref, gain_ref, poolw_ref, pscale_ref,
               wout_ref, wg_ref, wu_ref, wd_ref, fgain_ref, o_ref, mixed_ref, h_ref, act_ref, *, layer, final):
    mixed_ref[:, 0:ATTN_WIDTH] = attn_ref[...]
    for g, w in enumerate(POOL_SIZES):
        cols = slice(g * POOL_GROUP_WIDTH, (g + 1) * POOL_GROUP_WIDTH)
        ug = u_ref[:, cols]
        acc = ug
        for d in range(1, w):
            r = POOL_STATE - d
            acc = acc + hist_ref[:, r * POOL_WIDTH + g * POOL_GROUP_WIDTH:r * POOL_WIDTH + (g + 1) * POOL_GROUP_WIDTH]
        pooled = (acc / float(w) - ug).astype(BF16)
        pg = jnp.dot(pooled, poolw_ref[g], preferred_element_type=F32) * pscale_ref[layer:layer + 1, cols]
        mixed_ref[:, ATTN_WIDTH + g * POOL_GROUP_WIDTH:ATTN_WIDTH + (g + 1) * POOL_GROUP_WIDTH] = pg.astype(BF16)
    mod = _mod_getter(mod_ref)
    x2, h = _out_projection(x1_ref[...], mixed_ref, wout_ref, mod, gain_ref, layer)
    h_ref[...] = h
    o_ref[...] = _ffn2_out(x2, _swiglu(h_ref, act_ref, wg_ref, wu_ref, wd_ref), mod, fgain_ref, final)


def _kc_call(x1, attn, u, hist, mod_all, mod_spec, gain, poolw, pscale, wout, wg, wu, wd, fgain, layer, final, name):
    r = x1.shape[0]
    return pl.pallas_call(
        functools.partial(_kc_kernel, layer=layer, final=final),
        out_shape=jax.ShapeDtypeStruct((r, D_MODEL), F32),
        grid=(1,),
        in_specs=[_full((r, D_MODEL)), _full((r, ATTN_WIDTH)), _full((r, POOL_WIDTH)),
                  pl.BlockSpec((None, r, POOL_STATE * POOL_WIDTH), lambda *_: (layer, 0, 0)),
                  mod_spec, _full((DEPTH, 3, D_MODEL)),
                  _layer_resident((len(POOL_SIZES), POOL_GROUP_WIDTH, POOL_GROUP_WIDTH), layer),
                  _full((DEPTH, POOL_WIDTH)),
                  _layer_resident((D_MODEL, D_MODEL), layer), _layer_resident((D_MODEL, D_FF), layer),
                  _layer_resident((D_MODEL, D_FF), layer), _layer_resident((D_FF, D_MODEL), layer),
                  _full((1, D_MODEL))],
        out_specs=_full((r, D_MODEL)),
        scratch_shapes=[pltpu.VMEM((r, D_MODEL), BF16), pltpu.VMEM((r, D_MODEL), BF16),
                        pltpu.VMEM((r, D_FF), BF16)],
        compiler_params=pltpu.CompilerParams(dimension_semantics=("arbitrary",), vmem_limit_bytes=VMEM_LIMIT),
        name=name,
    )(x1, attn, u, hist, mod_all, gain, poolw, pscale, wout, wg, wu, wd, fgain)


def kernel(x_prompt, x_sample, c_prompt, c_sample, cache_k, cache_v, state_pool, w_ada, b_ada, norm_gain,
           w_in, w_out, sinks, rel_bias, pool_w, pool_scale, ffn1_wg, ffn1_wu, ffn1_wd, ffn2_wg, ffn2_wu,
           ffn2_wd, final_gain):
    n_p, seq, _ = x_prompt.shape
    n_s = x_sample.shape[0]

    mod_all = _modulation(jnp.concatenate([c_sample, c_prompt], axis=0), w_ada, b_ada)
    bias = _bias_table(rel_bias)
    bias_s = bias.reshape(N_HEADS, WINDOW, 2 * WINDOW)[:, 0, :]
    fgain = final_gain.reshape(1, D_MODEL)

    wq = (w_in[:, :, :ATTN_WIDTH].reshape(DEPTH, D_MODEL, 2, 4, HEAD_DIM).transpose(0, 1, 3, 2, 4)
          .reshape(DEPTH, D_MODEL, ATTN_WIDTH)) * (HEAD_DIM ** -0.5)
    win = jnp.concatenate([wq, w_in[:, :, ATTN_WIDTH:]], axis=2).astype(BF16)
    wo_attn = (w_out[:, :ATTN_WIDTH].reshape(DEPTH, 2, 4, HEAD_DIM, D_MODEL).transpose(0, 2, 1, 3, 4)
               .reshape(DEPTH, ATTN_WIDTH, D_MODEL))
    wout = jnp.concatenate([wo_attn, w_out[:, ATTN_WIDTH:]], axis=1).astype(BF16)
    wg1, wu1, wd1 = ffn1_wg.astype(BF16), ffn1_wu.astype(BF16), ffn1_wd.astype(BF16)
    wg2, wu2, wd2 = ffn2_wg.astype(BF16), ffn2_wu.astype(BF16), ffn2_wd.astype(BF16)
    poolw = pool_w.astype(BF16)

    xp = x_prompt
    xs = x_sample.reshape(n_s, D_MODEL)
    ck = cache_k.reshape(DEPTH, n_s, WINDOW, KV_WIDTH)
    cv = cache_v.reshape(DEPTH, n_s, WINDOW, KV_WIDTH)
    hist = state_pool.reshape(DEPTH, n_s, POOL_STATE * POOL_WIDTH)

    new_kp, new_vp, new_pp, new_ks, new_vs, new_ps = [], [], [], [], [], []
    for l in range(DEPTH):
        final = l == DEPTH - 1
        mod_p = pl.BlockSpec((None, n_p, MOD_WIDTH), lambda *_, l=l: (l, n_s // n_p, 0))
        mod_s = pl.BlockSpec((None, n_s, MOD_WIDTH), lambda *_, l=l: (l, 0, 0))

        x1, q, k, v, u, k_last, v_last, u_last = _ka_prompt_call(
            xp, mod_all, mod_p, norm_gain, wg1, wu1, wd1, win, l, f"prompt_ffn1_inproj_l{l}")
        xp = _kb_call(sinks, x1, q, k, v, u, mod_all, mod_p, norm_gain, bias, poolw, pool_scale, wout,
                      wg2, wu2, wd2, fgain, l, final, f"prompt_mixer_ffn2_l{l}")
        new_kp.append(k_last)
        new_vp.append(v_last)
        new_pp.append(u_last)

        x1s, qs, ks, vs, us = _ka_sample_call(xs, mod_all, mod_s, norm_gain, wg1, wu1, wd1, win, l,
                                              f"sample_ffn1_inproj_l{l}")
        attn_s = _ks_call(sinks, qs, ks, vs, ck, cv, l, bias_s)
        xs = _kc_call(x1s, attn_s, us, hist, mod_all, mod_s, norm_gain, poolw, pool_scale, wout, wg2, wu2, wd2,
                      fgain, l, final, f"sample_mixer_ffn2_l{l}")
        new_ks.append(ks)
        new_vs.append(vs)
        new_ps.append(us)

    row = lambda rows, shape: jnp.stack(rows).reshape((DEPTH, n_s, 1) + shape)
    new_k_sample = jnp.concatenate([cache_k[:, :, 1:], row(new_ks, (N_KV_HEADS, HEAD_DIM))], axis=2)
    new_v_sample = jnp.concatenate([cache_v[:, :, 1:], row(new_vs, (N_KV_HEADS, HEAD_DIM))], axis=2)
    new_pool_sample = jnp.concatenate([state_pool[:, :, 1:], row(new_ps, (POOL_WIDTH,))], axis=2)
    kv_shape = (DEPTH, n_p, WINDOW, N_KV_HEADS, HEAD_DIM)
    return (xp, xs.reshape(n_s, 1, D_MODEL), jnp.stack(new_kp).reshape(kv_shape), jnp.stack(new_vp).reshape(kv_shape),
            jnp.stack(new_pp)[:, :, HIST - POOL_STATE:], new_k_sample, new_v_sample, new_pool_sample)
```

```python
import functools
import math

import numpy as np
import jax
import jax.numpy as jnp
from jax import lax
from jax.experimental import pallas as pl
from jax.experimental.pallas import tpu as pltpu

D_MODEL = 1024
DEPTH = 2
HEAD_DIM = 64
N_HEADS = 8
N_KV_HEADS = 2
ATTN_WIDTH = N_HEADS * HEAD_DIM
KV_WIDTH = N_KV_HEADS * HEAD_DIM
WINDOW = 128
POOL_SIZES = (2, 4, 8, 16)
POOL_GROUP_WIDTH = 128
POOL_WIDTH = len(POOL_SIZES) * POOL_GROUP_WIDTH
POOL_STATE = max(POOL_SIZES) - 1
IN_WIDTH = ATTN_WIDTH + 2 * KV_WIDTH + POOL_WIDTH
D_FF = 2816
N_BUCKETS = 32
MAX_DISTANCE = 128
N_MOD = 9
MOD_WIDTH = N_MOD * D_MODEL
EPS = 1e-6

LANES = 128
SUBLANES = 8
MXU_COLS = 256
FF_CHUNK = 2 * MXU_COLS
FF_CHUNKS = tuple((c, min(FF_CHUNK, D_FF - c)) for c in range(0, D_FF, FF_CHUNK))
TM_PROMPT = 512
HIST = 16
VMEM_LIMIT = 56 * 1024 * 1024

F32 = jnp.float32
BF16 = jnp.bfloat16
NEG_INF = float("-inf")


def _head_of_row(r):
    return (r // 2) + 4 * (r % 2)


def _t5_bucket_np(dist):
    n = np.maximum(dist, 0)
    max_exact = N_BUCKETS // 2
    nf = np.maximum(n, 1).astype(np.float32)
    large = max_exact + (np.log(nf / max_exact) / math.log(MAX_DISTANCE / max_exact)
                         * (N_BUCKETS - max_exact)).astype(np.int32)
    large = np.minimum(large, N_BUCKETS - 1)
    return np.where(n < max_exact, n, large).astype(np.int32)


def _bucket_table():
    dist = np.arange(WINDOW)[:, None] + WINDOW - np.arange(2 * WINDOW)[None, :]
    return _t5_bucket_np(dist)


def _full(shape):
    return pl.BlockSpec(shape, lambda *_: (0,) * len(shape))


def _layer_resident(shape, layer):
    return pl.BlockSpec((None,) + tuple(shape), lambda *_: (layer,) + (0,) * len(shape),
                        pipeline_mode=pl.Buffered(1))


_SMEM = pl.BlockSpec(memory_space=pltpu.SMEM)


def _rms_mod(x, gain, shift, scale):
    y = x * lax.rsqrt(jnp.mean(x * x, axis=-1, keepdims=True) + EPS)
    return (y * gain) * (1.0 + scale) + shift


def _dot_by_row_halves(a_ref, w):
    half = a_ref.shape[0] // 2
    if half % (2 * SUBLANES) or half < MXU_COLS:
        return jnp.dot(a_ref[...], w, preferred_element_type=F32)
    return jnp.concatenate([jnp.dot(a_ref[:half, :], w, preferred_element_type=F32),
                            jnp.dot(a_ref[half:, :], w, preferred_element_type=F32)], axis=0)


def _swiglu(h_ref, act_ref, wg_ref, wu_ref, wd_ref, side_work=(), before_down=None):
    n = len(FF_CHUNKS)
    assert len(side_work) <= n
    after_chunk = {((j + 1) * n) // (len(side_work) + 1) - 1: work for j, work in enumerate(side_work)}
    chunk_dot = ((lambda a_ref, w: jnp.dot(a_ref[...], w, preferred_element_type=F32)) if side_work
                 else _dot_by_row_halves)
    for i, (c0, cw) in enumerate(FF_CHUNKS):
        g = chunk_dot(h_ref, wg_ref[:, c0:c0 + cw])
        u = chunk_dot(h_ref, wu_ref[:, c0:c0 + cw])
        act_ref[:, c0:c0 + cw] = ((g / (1.0 + jnp.exp(-g))) * u).astype(BF16)
        if i in after_chunk:
            after_chunk[i]()
    if before_down is not None:
        before_down()
    return _dot_by_row_halves(act_ref, wd_ref[...])


def _mod_getter(mod_ref, row=None):
    rows = slice(None) if row is None else pl.ds(row, 1)
    return lambda k: mod_ref[rows, k * D_MODEL:(k + 1) * D_MODEL]


def _mod_kernel(c_ref, w_ref, b_ref, o_ref):
    c = c_ref[...]
    a = (c / (1.0 + jnp.exp(-c))).astype(BF16)
    o_ref[...] = jnp.dot(a, w_ref[...].astype(BF16), preferred_element_type=F32) + b_ref[...]


def _modulation(c_all, w_ada, b_ada):
    rows = c_all.shape[0]
    n_chunk = 1024
    return pl.pallas_call(
        _mod_kernel,
        out_shape=jax.ShapeDtypeStruct((DEPTH, rows, MOD_WIDTH), F32),
        grid=(DEPTH, MOD_WIDTH // n_chunk),
        in_specs=[
            pl.BlockSpec((rows, D_MODEL), lambda l, j: (0, 0)),
            pl.BlockSpec((None, D_MODEL, n_chunk), lambda l, j: (l, 0, j)),
            pl.BlockSpec((None, 1, n_chunk), lambda l, j: (l, 0, j)),
        ],
        out_specs=pl.BlockSpec((None, rows, n_chunk), lambda l, j: (l, 0, j)),
        compiler_params=pltpu.CompilerParams(
            dimension_semantics=("arbitrary", "arbitrary"), vmem_limit_bytes=VMEM_LIMIT),
        name="adaln_modulation",
    )(c_all, w_ada, b_ada.reshape(DEPTH, 1, MOD_WIDTH))


def _bias_kernel(rb_ref, bucket_ref, o_ref):
    bucket = bucket_ref[...]
    eq = [bucket == b for b in range(N_BUCKETS)]
    for r in range(N_HEADS):
        acc = jnp.zeros((WINDOW, 2 * WINDOW), F32)
        for b in range(N_BUCKETS):
            acc = jnp.where(eq[b], rb_ref[b, _head_of_row(r)], acc)
        o_ref[r * WINDOW:(r + 1) * WINDOW, :] = acc


def _bias_table(rel_bias):
    return pl.pallas_call(
        _bias_kernel,
        out_shape=jax.ShapeDtypeStruct((N_HEADS * WINDOW, 2 * WINDOW), F32),
        in_specs=[_SMEM, pl.BlockSpec(memory_space=pltpu.VMEM)],
        out_specs=pl.BlockSpec(memory_space=pltpu.VMEM),
        name="rel_bias_table",
    )(rel_bias, jnp.asarray(_bucket_table()))


def _project_q(hm_ref, win_ref, q_ref):
    q_ref[...] = jnp.dot(hm_ref[...], win_ref[:, :ATTN_WIDTH], preferred_element_type=F32).astype(BF16)


def _project_kvu(hm_ref, win_ref, k_ref, v_ref, u_ref):
    z = jnp.dot(hm_ref[...], win_ref[:, ATTN_WIDTH:], preferred_element_type=F32)
    k_ref[...] = z[:, :KV_WIDTH]
    v_ref[...] = z[:, KV_WIDTH:2 * KV_WIDTH]
    u_ref[...] = z[:, 2 * KV_WIDTH:]
    return z


def _ffn1(x_ref, mod, gain_ref, wg_ref, wu_ref, wd_ref, x1_ref, h_ref, act_ref, layer, before_norm=None):
    x = x_ref[...]
    h_ref[...] = _rms_mod(x, gain_ref[layer, 0:1, :], mod(0), mod(1)).astype(BF16)
    x1 = x + 0.5 * mod(2) * _swiglu(h_ref, act_ref, wg_ref, wu_ref, wd_ref)
    x1_ref[...] = x1
    if before_norm is not None:
        before_norm()
    return _rms_mod(x1, gain_ref[layer, 1:2, :], mod(3), mod(4)).astype(BF16)


def _ka_sample_kernel(x_ref, mod_ref, gain_ref, wg_ref, wu_ref, wd_ref, win_ref,
                      x1_ref, q_ref, k_ref, v_ref, u_ref, h_ref, act_ref, *, layer):
    h_ref[...] = _ffn1(x_ref, _mod_getter(mod_ref), gain_ref, wg_ref, wu_ref, wd_ref, x1_ref, h_ref, act_ref, layer)
    _project_q(h_ref, win_ref, q_ref)
    _project_kvu(h_ref, win_ref, k_ref, v_ref, u_ref)


def _ka_prompt_kernel(x_ref, mod_ref, gain_ref, wg_ref, wu_ref, wd_ref, win_ref,
                      x1_ref, q_ref, k_ref, v_ref, u_ref, kl_ref, vl_ref, ul_ref,
                      h_ref, act_ref, hm_ref, *, layer, tiles_per_seq):
    tm = x_ref.shape[0]
    s = pl.program_id(0)
    main = jnp.minimum(s, pl.num_programs(0) - 2)

    @pl.when(s == 0)
    def _():
        hm_ref[...] = jnp.zeros_like(hm_ref)

    _project_q(hm_ref, win_ref, q_ref)

    def project_rest():
        z = _project_kvu(hm_ref, win_ref, k_ref, v_ref, u_ref)
        kl_ref[...] = z[tm - WINDOW:, :KV_WIDTH]
        vl_ref[...] = z[tm - WINDOW:, KV_WIDTH:2 * KV_WIDTH]
        ul_ref[...] = z[tm - HIST:, 2 * KV_WIDTH:]

    mod = _mod_getter(mod_ref, main // tiles_per_seq)
    hm = _ffn1(x_ref, mod, gain_ref, wg_ref, wu_ref, wd_ref, x1_ref, h_ref, act_ref, layer,
               before_norm=project_rest)
    hm_ref[...] = hm


def _ka_out_shapes(g, r):
    return (jax.ShapeDtypeStruct((g, r, D_MODEL), F32),
            jax.ShapeDtypeStruct((g, r, ATTN_WIDTH), BF16),
            jax.ShapeDtypeStruct((g, r, KV_WIDTH), F32),
            jax.ShapeDtypeStruct((g, r, KV_WIDTH), F32),
            jax.ShapeDtypeStruct((g, r, POOL_WIDTH), F32))


def _ka_weight_specs(layer):
    return [_full((DEPTH, 3, D_MODEL)),
            _layer_resident((D_MODEL, D_FF), layer), _layer_resident((D_MODEL, D_FF), layer),
            _layer_resident((D_FF, D_MODEL), layer), _layer_resident((D_MODEL, IN_WIDTH), layer)]


def _ka_sample_call(x, mod_all, mod_spec, gain, wg, wu, wd, win, layer, name):
    r = x.shape[0]
    row = lambda w: _full((r, w))
    return pl.pallas_call(
        functools.partial(_ka_sample_kernel, layer=layer),
        out_shape=tuple(jax.ShapeDtypeStruct(s.shape[1:], s.dtype) for s in _ka_out_shapes(1, r)),
        grid=(1,),
        in_specs=[row(D_MODEL), mod_spec] + _ka_weight_specs(layer),
        out_specs=(row(D_MODEL), row(ATTN_WIDTH), row(KV_WIDTH), row(KV_WIDTH), row(POOL_WIDTH)),
        scratch_shapes=[pltpu.VMEM((r, D_MODEL), BF16), pltpu.VMEM((r, D_FF), BF16)],
        compiler_params=pltpu.CompilerParams(dimension_semantics=("arbitrary",), vmem_limit_bytes=VMEM_LIMIT),
        name=name,
    )(x, mod_all, gain, wg, wu, wd, win)


def _ka_prompt_call(x, mod_all, mod_spec, gain, wg, wu, wd, win, layer, name):
    g, r, _ = x.shape
    tm = TM_PROMPT
    tps = r // tm
    n_tiles = g * tps
    main = lambda s: jnp.minimum(s, n_tiles - 1)
    prev = lambda s: jnp.maximum(s - 1, 0)
    main_row = lambda w: pl.BlockSpec((None, tm, w), lambda s: (main(s) // tps, main(s) % tps, 0))
    prev_row = lambda w: pl.BlockSpec((None, tm, w), lambda s: (prev(s) // tps, prev(s) % tps, 0))
    last = lambda rows, w: pl.BlockSpec((None, rows, w), lambda s: (prev(s) // tps, 0, 0))
    return pl.pallas_call(
        functools.partial(_ka_prompt_kernel, layer=layer, tiles_per_seq=tps),
        out_shape=_ka_out_shapes(g, r) + (jax.ShapeDtypeStruct((g, WINDOW, KV_WIDTH), F32),
                                          jax.ShapeDtypeStruct((g, WINDOW, KV_WIDTH), F32),
                                          jax.ShapeDtypeStruct((g, HIST, POOL_WIDTH), F32)),
        grid=(n_tiles + 1,),
        in_specs=[main_row(D_MODEL), mod_spec] + _ka_weight_specs(layer),
        out_specs=(main_row(D_MODEL), prev_row(ATTN_WIDTH), prev_row(KV_WIDTH), prev_row(KV_WIDTH),
                   prev_row(POOL_WIDTH), last(WINDOW, KV_WIDTH), last(WINDOW, KV_WIDTH), last(HIST, POOL_WIDTH)),
        scratch_shapes=[pltpu.VMEM((tm, D_MODEL), BF16), pltpu.VMEM((tm, D_FF), BF16),
                        pltpu.VMEM((tm, D_MODEL), BF16)],
        compiler_params=pltpu.CompilerParams(dimension_semantics=("arbitrary",), vmem_limit_bytes=VMEM_LIMIT),
        name=name,
    )(x, mod_all, gain, wg, wu, wd, win)


def _low_half():
    return lax.broadcasted_iota(jnp.int32, (WINDOW, LANES), 1) < HEAD_DIM


def _attn_probs(qb, kk, bias_ref, sinks_ref, layer, valid, p_ref):
    lo = _low_half()
    zero = jnp.zeros((WINDOW, LANES), BF16)
    pieces = []
    for c in range(4):
        qc = qb[:, c * LANES:(c + 1) * LANES]
        pieces += [jnp.where(lo, qc, zero), jnp.where(lo, zero, qc)]
    qs = jnp.concatenate(pieces, axis=0)
    s = lax.dot_general(qs, kk, (((1,), (1,)), ((), ())), preferred_element_type=F32)
    for r in range(N_HEADS):
        rows = slice(r * WINDOW, (r + 1) * WINDOW)
        sr = jnp.where(valid, s[rows] + bias_ref[rows, :], NEG_INF)
        sink = sinks_ref[layer, _head_of_row(r)]
        m = jnp.maximum(jnp.max(sr, axis=-1, keepdims=True), sink)
        p = jnp.exp(sr - m)
        denom = jnp.sum(p, axis=-1, keepdims=True) + jnp.exp(sink - m)
        p_ref[rows, :] = (p / denom).astype(BF16)


def _attn_values(p_ref, vv):
    lo = _low_half()
    o = jnp.dot(p_ref[...], vv, preferred_element_type=F32)
    cols = [jnp.where(lo, o[(2 * c) * WINDOW:(2 * c + 1) * WINDOW], o[(2 * c + 1) * WINDOW:(2 * c + 2) * WINDOW])
            for c in range(4)]
    return jnp.concatenate(cols, axis=1)


def _pool_sums(u_ref, up_ref, uext_ref, t1_ref, t2_ref, pooled_ref, seq_tile):
    tm = u_ref.shape[0]
    g_w = POOL_GROUP_WIDTH
    top = 2 * HIST
    end = top + tm
    assert POOL_SIZES == (2, 4, 8, 16) and HIST == 2 * SUBLANES
    uext_ref[0:HIST, :] = jnp.zeros((HIST, POOL_WIDTH), F32)
    uext_ref[HIST:top, :] = jnp.where(seq_tile == 0, 0.0, up_ref[...])
    uext_ref[top:, :] = u_ref[...]
    t1_ref[8:end, :] = uext_ref[8:end, :] + uext_ref[7:end - 1, :]
    t2_ref[16:end, :] = t1_ref[16:end, g_w:] + t1_ref[14:end - 2, g_w:]
    t1_ref[24:end, 2 * g_w:] = t2_ref[24:end, g_w:] + t2_ref[20:end - 4, g_w:]
    sums = [t1_ref[top:, 0:g_w], t2_ref[top:, 0:g_w], t1_ref[top:, 2 * g_w:3 * g_w],
            t1_ref[top:, 3 * g_w:] + t1_ref[top - 8:end - 8, 3 * g_w:]]
    pos = seq_tile * tm + lax.broadcasted_iota(jnp.int32, (tm, g_w), 0)
    for g, w in enumerate(POOL_SIZES):
        cols = slice(g * g_w, (g + 1) * g_w)
        count = jnp.minimum(pos + 1, w).astype(F32)
        pooled_ref[:, cols] = (sums[g] / count - uext_ref[top:, cols]).astype(BF16)


def _mix_prompt_steps(q_ref, k_ref, v_ref, kp_ref, vp_ref, bias_ref, sinks_ref, poolw_ref,
                      pscale_ref, mixed_ref, pooled_ref, p_ref, layer, seq_tile):
    tm = q_ref.shape[0]
    nb = tm // WINDOW
    first = seq_tile == 0

    def keys_values(ref, prev_ref, j):
        prev = prev_ref[...] if j == 0 else ref[(j - 1) * WINDOW:j * WINDOW, :]
        return jnp.concatenate([prev, ref[j * WINDOW:(j + 1) * WINDOW, :]], axis=0).astype(BF16)

    def probs(j):
        qi = lax.broadcasted_iota(jnp.int32, (WINDOW, 2 * WINDOW), 0)
        kj = lax.broadcasted_iota(jnp.int32, (WINDOW, 2 * WINDOW), 1)
        dist = qi + WINDOW - kj
        valid = (dist >= 0) & (dist <= WINDOW)
        if j == 0:
            valid = valid & (kj >= jnp.where(first, WINDOW, 0))
        _attn_probs(q_ref[j * WINDOW:(j + 1) * WINDOW, :], keys_values(k_ref, kp_ref, j), bias_ref, sinks_ref,
                    layer, valid, p_ref)

    def values(j):
        a = _attn_values(p_ref, keys_values(v_ref, vp_ref, j))
        mixed_ref[j * WINDOW:(j + 1) * WINDOW, 0:ATTN_WIDTH] = a.astype(BF16)

    def pool_map(g):
        cols = slice(g * POOL_GROUP_WIDTH, (g + 1) * POOL_GROUP_WIDTH)
        pg = jnp.dot(pooled_ref[:, cols], poolw_ref[g], preferred_element_type=F32) * pscale_ref[layer:layer + 1, cols]
        mixed_ref[:, ATTN_WIDTH + g * POOL_GROUP_WIDTH:ATTN_WIDTH + (g + 1) * POOL_GROUP_WIDTH] = pg.astype(BF16)

    assert nb == len(POOL_SIZES)

    def stage(i):
        def run():
            if i > 0:
                values(i - 1)
                pool_map(i - 1)
            if i < nb:
                probs(i)
        return run

    return [stage(i) for i in range(nb + 1)]


def _out_projection(x1, mixed_ref, wout_ref, mod, gain_ref, layer):
    x2 = x1 + mod(5) * _dot_by_row_halves(mixed_ref, wout_ref[...])
    return x2, _rms_mod(x2, gain_ref[layer, 2:3, :], mod(6), mod(7)).astype(BF16)


def _ffn2_out(x2, y, mod, fgain_ref, final):
    x3 = x2 + 0.5 * mod(8) * y
    if final:
        x3 = x3 * lax.rsqrt(jnp.mean(x3 * x3, axis=-1, keepdims=True) + EPS) * fgain_ref[...]
    return x3


def _kb_kernel(sinks_ref, x1_ref, q_ref, k_ref, v_ref, kp_ref, vp_ref, u_ref, up_ref, un_ref, upn_ref,
               mod_ref, gain_ref, bias_ref, poolw_ref, pscale_ref, wout_ref, wg_ref, wu_ref, wd_ref, fgain_ref,
               o_ref, mixed_ref, h_ref, x2_ref, act_ref, uext_ref, t1_ref, t2_ref, pooled_ref, p_ref,
               *, layer, final, tiles_per_seq):
    s = pl.program_id(0)
    n_tiles = pl.num_programs(0) - 1
    mix_tile = jnp.minimum(s, n_tiles - 1)
    next_tile = jnp.minimum(s + 1, n_tiles - 1)
    ffn_tile = jnp.maximum(s - 1, 0)
    slot = s % 2

    @pl.when(s == 0)
    def _():
        h_ref[...] = jnp.zeros_like(h_ref)
        x2_ref[1] = jnp.zeros(x2_ref.shape[1:], F32)
        _pool_sums(u_ref, up_ref, uext_ref, t1_ref, t2_ref, pooled_ref, 0)

    mix_steps = _mix_prompt_steps(q_ref, k_ref, v_ref, kp_ref, vp_ref, bias_ref, sinks_ref, poolw_ref, pscale_ref,
                                  mixed_ref, pooled_ref, p_ref, layer, mix_tile % tiles_per_seq)

    def finish_mix():
        x2, h = _out_projection(x1_ref[...], mixed_ref, wout_ref, _mod_getter(mod_ref, mix_tile // tiles_per_seq),
                                gain_ref, layer)
        x2_ref[slot] = x2
        h_ref[...] = h
        _pool_sums(un_ref, upn_ref, uext_ref, t1_ref, t2_ref, pooled_ref, next_tile % tiles_per_seq)

    y = _swiglu(h_ref, act_ref, wg_ref, wu_ref, wd_ref, side_work=mix_steps, before_down=finish_mix)
    o_ref[...] = _ffn2_out(x2_ref[1 - slot], y, _mod_getter(mod_ref, ffn_tile // tiles_per_seq), fgain_ref, final)


def _kb_call(sinks, x1, q, k, v, u, mod_all, mod_spec, gain, bias, poolw, pscale, wout, wg, wu, wd, fgain,
             layer, final, name):
    g, r, _ = x1.shape
    tm = TM_PROMPT
    tps = r // tm
    n_tiles = g * tps
    nb = tm // WINDOW
    nh = tm // HIST
    mix = lambda s: jnp.minimum(s, n_tiles - 1)
    tail = lambda s: jnp.maximum(s - 1, 0)
    mix_row = lambda w: pl.BlockSpec((None, tm, w), lambda s: (mix(s) // tps, mix(s) % tps, 0))
    tail_row = lambda w: pl.BlockSpec((None, tm, w), lambda s: (tail(s) // tps, tail(s) % tps, 0))
    prev_kv = pl.BlockSpec((None, WINDOW, KV_WIDTH),
                           lambda s: (mix(s) // tps, jnp.maximum((mix(s) % tps) * nb - 1, 0), 0))
    nxt = lambda s: jnp.minimum(s + 1, n_tiles - 1)
    first_u = pl.BlockSpec((None, tm, POOL_WIDTH), lambda s: (0, 0, 0))
    first_hist = pl.BlockSpec((None, HIST, POOL_WIDTH), lambda s: (0, 0, 0))
    next_u = pl.BlockSpec((None, tm, POOL_WIDTH), lambda s: (nxt(s) // tps, nxt(s) % tps, 0))
    next_hist = pl.BlockSpec((None, HIST, POOL_WIDTH),
                             lambda s: (nxt(s) // tps, jnp.maximum((nxt(s) % tps) * nh - 1, 0), 0))
    return pl.pallas_call(
        functools.partial(_kb_kernel, layer=layer, final=final, tiles_per_seq=tps),
        out_shape=jax.ShapeDtypeStruct((g, r, D_MODEL), F32),
        grid=(n_tiles + 1,),
        in_specs=[_SMEM, mix_row(D_MODEL), mix_row(ATTN_WIDTH), mix_row(KV_WIDTH), mix_row(KV_WIDTH),
                  prev_kv, prev_kv, first_u, first_hist, next_u, next_hist, mod_spec, _full((DEPTH, 3, D_MODEL)),
                  _full((N_HEADS * WINDOW, 2 * WINDOW)),
                  _layer_resident((len(POOL_SIZES), POOL_GROUP_WIDTH, POOL_GROUP_WIDTH), layer),
                  _full((DEPTH, POOL_WIDTH)),
                  _layer_resident((D_MODEL, D_MODEL), layer), _layer_resident((D_MODEL, D_FF), layer),
                  _layer_resident((D_MODEL, D_FF), layer), _layer_resident((D_FF, D_MODEL), layer),
                  _full((1, D_MODEL))],
        out_specs=tail_row(D_MODEL),
        scratch_shapes=[pltpu.VMEM((tm, D_MODEL), BF16), pltpu.VMEM((tm, D_MODEL), BF16),
                        pltpu.VMEM((2, tm, D_MODEL), F32), pltpu.VMEM((tm, D_FF), BF16),
                        pltpu.VMEM((2 * HIST + tm, POOL_WIDTH), F32), pltpu.VMEM((2 * HIST + tm, POOL_WIDTH), F32),
                        pltpu.VMEM((2 * HIST + tm, POOL_WIDTH - POOL_GROUP_WIDTH), F32),
                        pltpu.VMEM((tm, POOL_WIDTH), BF16),
                        pltpu.VMEM((N_HEADS * WINDOW, 2 * WINDOW), BF16)],
        compiler_params=pltpu.CompilerParams(
            dimension_semantics=("arbitrary",), vmem_limit_bytes=VMEM_LIMIT),
        name=name,
    )(sinks, x1, q, k, v, k, v, u, u, u, u, mod_all, gain, bias, poolw, pscale, wout, wg, wu, wd, fgain)


def _ks_kernel(sinks_ref, q_ref, kn_ref, vn_ref, ck_ref, cv_ref, bias_ref, o_ref, nk_ref, nv_ref,
               qs_ref, knr_ref, vnr_ref, s_ref, p_ref, acc_ref, *, layer):
    bt = q_ref.shape[0]
    for b in range(bt):
        for cache_ref, new_ref, out_ref in ((ck_ref, kn_ref, nk_ref), (cv_ref, vn_ref, nv_ref)):
            out_ref[b, 0:WINDOW - 1, :] = cache_ref[b, 1:WINDOW, :]
            out_ref[b, WINDOW - 1:WINDOW, :] = new_ref[b:b + 1, :]
    rows = bt * N_HEADS
    nt = (((1,), (1,)), ((), ()))
    head_rows = lambda r: pl.ds(r, bt, stride=N_HEADS)
    lo = lax.broadcasted_iota(jnp.int32, (bt, LANES), 1) < HEAD_DIM
    kn = kn_ref[...].astype(BF16).astype(F32)
    vn = vn_ref[...].astype(BF16).astype(F32)
    for c in range(4):
        qc = q_ref[:, c * LANES:(c + 1) * LANES].astype(F32)
        qs_ref[head_rows(2 * c), :] = jnp.where(lo, qc, 0.0)
        qs_ref[head_rows(2 * c + 1), :] = jnp.where(lo, 0.0, qc)
    for r in range(N_HEADS):
        knr_ref[head_rows(r), :] = kn
        vnr_ref[head_rows(r), :] = vn

    for b in range(bt):
        rb = slice(b * N_HEADS, (b + 1) * N_HEADS)
        s_ref[rb, :] = lax.dot_general(qs_ref[rb, :].astype(BF16), ck_ref[b].astype(BF16), nt,
                                       preferred_element_type=F32)

    head = jnp.bitwise_and(lax.broadcasted_iota(jnp.int32, (rows, 1), 0), N_HEADS - 1)
    sink = jnp.zeros((rows, 1), F32)
    for r in range(N_HEADS):
        sink = jnp.where(head == r, sinks_ref[layer, _head_of_row(r)], sink)
    bias = jnp.concatenate([bias_ref[...]] * bt, axis=0)
    s_c = s_ref[...] + bias[:, 0:WINDOW]
    s_n = jnp.sum(qs_ref[...] * knr_ref[...], axis=-1, keepdims=True) + bias[:, WINDOW:WINDOW + 1]
    m = jnp.maximum(jnp.maximum(jnp.max(s_c, axis=-1, keepdims=True), s_n), sink)
    p_c = jnp.exp(s_c - m)
    p_n = jnp.exp(s_n - m)
    denom = jnp.sum(p_c, axis=-1, keepdims=True) + p_n + jnp.exp(sink - m)
    p_ref[...] = p_c / denom

    for b in range(bt):
        rb = slice(b * N_HEADS, (b + 1) * N_HEADS)
        acc_ref[rb, :] = jnp.dot(p_ref[rb, :].astype(BF16), cv_ref[b].astype(BF16), preferred_element_type=F32)
    acc_ref[...] = acc_ref[...] + (p_n / denom).astype(BF16).astype(F32) * vnr_ref[...]

    for c in range(4):
        o_ref[:, c * LANES:(c + 1) * LANES] = jnp.where(
            lo, acc_ref[head_rows(2 * c), :], acc_ref[head_rows(2 * c + 1), :]).astype(BF16)


def _ks_call(sinks, q, kn, vn, cache_k, cache_v, layer, bias_s):
    nb = q.shape[0]
    bt = 32
    rows = bt * N_HEADS
    per_b = lambda w: pl.BlockSpec((bt, w), lambda i: (i, 0))
    cache = pl.BlockSpec((None, bt, WINDOW, KV_WIDTH), lambda i: (layer, i, 0, 0))
    new_cache = pl.BlockSpec((bt, WINDOW, KV_WIDTH), lambda i: (i, 0, 0))
    return pl.pallas_call(
        functools.partial(_ks_kernel, layer=layer),
        out_shape=(jax.ShapeDtypeStruct((nb, ATTN_WIDTH), BF16),
                   jax.ShapeDtypeStruct((nb, WINDOW, KV_WIDTH), F32),
                   jax.ShapeDtypeStruct((nb, WINDOW, KV_WIDTH), F32)),
        grid=(nb // bt,),
        in_specs=[_SMEM, per_b(ATTN_WIDTH), per_b(KV_WIDTH), per_b(KV_WIDTH), cache, cache,
                  _full((N_HEADS, 2 * WINDOW))],
        out_specs=(per_b(ATTN_WIDTH), new_cache, new_cache),
        scratch_shapes=[pltpu.VMEM((rows, KV_WIDTH), F32), pltpu.VMEM((rows, KV_WIDTH), F32),
                        pltpu.VMEM((rows, KV_WIDTH), F32), pltpu.VMEM((rows, WINDOW), F32),
                        pltpu.VMEM((rows, WINDOW), F32), pltpu.VMEM((rows, KV_WIDTH), F32)],
        compiler_params=pltpu.CompilerParams(dimension_semantics=("arbitrary",), vmem_limit_bytes=VMEM_LIMIT),
        name="sample_attention",
    )(sinks, q, kn, vn, cache_k, cache_v, bias_s)


def _kc_kernel(x1_ref, attn_ref, u_ref, hist_ref, mod_ref, gain_ref, poolw_ref, pscale_ref,
               wout_ref, wg_ref, wu_ref, wd_ref, fgain_ref, o_ref, mixed_ref, h_ref, act_ref, *, layer, final):
    mixed_ref[:, 0:ATTN_WIDTH] = attn_ref[...]
    for g, w in enumerate(POOL_SIZES):
        cols = slice(g * POOL_GROUP_WIDTH, (g + 1) * POOL_GROUP_WIDTH)
        ug = u_ref[:, cols]
        acc = ug
        for d in range(1, w):
            r = POOL_STATE - d
            acc = acc + hist_ref[:, r * POOL_WIDTH + g * POOL_GROUP_WIDTH:r * POOL_WIDTH + (g + 1) * POOL_GROUP_WIDTH]
        pooled = (acc / float(w) - ug).astype(BF16)
        pg = jnp.dot(pooled, poolw_ref[g], preferred_element_type=F32) * pscale_ref[layer:layer + 1, cols]
        mixed_ref[:, ATTN_WIDTH + g * POOL_GROUP_WIDTH:ATTN_WIDTH + (g + 1) * POOL_GROUP_WIDTH] = pg.astype(BF16)
    mod = _mod_getter(mod_ref)
    x2, h = _out_projection(x1_ref[...], mixed_ref, wout_ref, mod, gain_ref, layer)
    h_ref[...] = h
    o_ref[...] = _ffn2_out(x2, _swiglu(h_ref, act_ref, wg_ref, wu_ref, wd_ref), mod, fgain_ref, final)


def _kc_call(x1, attn, u, hist, mod_all, mod_spec, gain, poolw, pscale, wout, wg, wu, wd, fgain, layer, final, name):
    r = x1.shape[0]
    return pl.pallas_call(
        functools.partial(_kc_kernel, layer=layer, final=final),
        out_shape=jax.ShapeDtypeStruct((r, D_MODEL), F32),
        grid=(1,),
        in_specs=[_full((r, D_MODEL)), _full((r, ATTN_WIDTH)), _full((r, POOL_WIDTH)),
                  pl.BlockSpec((None, r, POOL_STATE * POOL_WIDTH), lambda *_: (layer, 0, 0)),
                  mod_spec, _full((DEPTH, 3, D_MODEL)),
                  _layer_resident((len(POOL_SIZES), POOL_GROUP_WIDTH, POOL_GROUP_WIDTH), layer),
                  _full((DEPTH, POOL_WIDTH)),
                  _layer_resident((D_MODEL, D_MODEL), layer), _layer_resident((D_MODEL, D_FF), layer),
                  _layer_resident((D_MODEL, D_FF), layer), _layer_resident((D_FF, D_MODEL), layer),
                  _full((1, D_MODEL))],
        out_specs=_full((r, D_MODEL)),
        scratch_shapes=[pltpu.VMEM((r, D_MODEL), BF16), pltpu.VMEM((r, D_MODEL), BF16),
                        pltpu.VMEM((r, D_FF), BF16)],
        compiler_params=pltpu.CompilerParams(dimension_semantics=("arbitrary",), vmem_limit_bytes=VMEM_LIMIT),
        name=name,
    )(x1, attn, u, hist, mod_all, gain, poolw, pscale, wout, wg, wu, wd, fgain)


def kernel(x_prompt, x_sample, c_prompt, c_sample, cache_k, cache_v, state_pool, w_ada, b_ada, norm_gain,
           w_in, w_out, sinks, rel_bias, pool_w, pool_scale, ffn1_wg, ffn1_wu, ffn1_wd, ffn2_wg, ffn2_wu,
           ffn2_wd, final_gain):
    n_p, seq, _ = x_prompt.shape
    n_s = x_sample.shape[0]

    mod_all = _modulation(jnp.concatenate([c_sample, c_prompt], axis=0), w_ada, b_ada)
    bias = _bias_table(rel_bias)
    bias_s = bias.reshape(N_HEADS, WINDOW, 2 * WINDOW)[:, 0, :]
    fgain = final_gain.reshape(1, D_MODEL)

    wq = (w_in[:, :, :ATTN_WIDTH].reshape(DEPTH, D_MODEL, 2, 4, HEAD_DIM).transpose(0, 1, 3, 2, 4)
          .reshape(DEPTH, D_MODEL, ATTN_WIDTH)) * (HEAD_DIM ** -0.5)
    win = jnp.concatenate([wq, w_in[:, :, ATTN_WIDTH:]], axis=2).astype(BF16)
    wo_attn = (w_out[:, :ATTN_WIDTH].reshape(DEPTH, 2, 4, HEAD_DIM, D_MODEL).transpose(0, 2, 1, 3, 4)
               .reshape(DEPTH, ATTN_WIDTH, D_MODEL))
    wout = jnp.concatenate([wo_attn, w_out[:, ATTN_WIDTH:]], axis=1).astype(BF16)
    wg1, wu1, wd1 = ffn1_wg.astype(BF16), ffn1_wu.astype(BF16), ffn1_wd.astype(BF16)
    wg2, wu2, wd2 = ffn2_wg.astype(BF16), ffn2_wu.astype(BF16), ffn2_wd.astype(BF16)
    poolw = pool_w.astype(BF16)

    xp = x_prompt
    xs = x_sample.reshape(n_s, D_MODEL)
    ck = cache_k.reshape(DEPTH, n_s, WINDOW, KV_WIDTH)
    cv = cache_v.reshape(DEPTH, n_s, WINDOW, KV_WIDTH)
    hist = state_pool.reshape(DEPTH, n_s, POOL_STATE * POOL_WIDTH)

    new_kp, new_vp, new_pp, new_ks, new_vs, new_ps = [], [], [], [], [], []
    for l in range(DEPTH):
        final = l == DEPTH - 1
        mod_p = pl.BlockSpec((None, n_p, MOD_WIDTH), lambda *_, l=l: (l, n_s // n_p, 0))
        mod_s = pl.BlockSpec((None, n_s, MOD_WIDTH), lambda *_, l=l: (l, 0, 0))

        x1, q, k, v, u, k_last, v_last, u_last = _ka_prompt_call(
            xp, mod_all, mod_p, norm_gain, wg1, wu1, wd1, win, l, f"prompt_ffn1_inproj_l{l}")
        xp = _kb_call(sinks, x1, q, k, v, u, mod_all, mod_p, norm_gain, bias, poolw, pool_scale, wout,
                      wg2, wu2, wd2, fgain, l, final, f"prompt_mixer_ffn2_l{l}")
        new_kp.append(k_last)
        new_vp.append(v_last)
        new_pp.append(u_last)

        x1s, qs, ks, vs, us = _ka_sample_call(xs, mod_all, mod_s, norm_gain, wg1, wu1, wd1, win, l,
                                              f"sample_ffn1_inproj_l{l}")
        attn_s, ck_next, cv_next = _ks_call(sinks, qs, ks, vs, ck, cv, l, bias_s)
        xs = _kc_call(x1s, attn_s, us, hist, mod_all, mod_s, norm_gain, poolw, pool_scale, wout, wg2, wu2, wd2,
                      fgain, l, final, f"sample_mixer_ffn2_l{l}")
        new_ks.append(ck_next)
        new_vs.append(cv_next)
        new_ps.append(us)

    new_pool_sample = jnp.concatenate(
        [state_pool[:, :, 1:], jnp.stack(new_ps).reshape(DEPTH, n_s, 1, POOL_WIDTH)], axis=2)
    kv_prompt = (DEPTH, n_p, WINDOW, N_KV_HEADS, HEAD_DIM)
    kv_sample = (DEPTH, n_s, WINDOW, N_KV_HEADS, HEAD_DIM)
    return (xp, xs.reshape(n_s, 1, D_MODEL), jnp.stack(new_kp).reshape(kv_prompt), jnp.stack(new_vp).reshape(kv_prompt),
            jnp.stack(new_pp)[:, :, HIST - POOL_STATE:], jnp.stack(new_ks).reshape(kv_sample),
            jnp.stack(new_vs).reshape(kv_sample), new_pool_sample)
```

```python
import functools
import math

import numpy as np
import jax
import jax.numpy as jnp
from jax import lax
from jax.experimental import pallas as pl
from jax.experimental.pallas import tpu as pltpu

D_MODEL = 1024
DEPTH = 2
HEAD_DIM = 64
N_HEADS = 8
N_KV_HEADS = 2
ATTN_WIDTH = N_HEADS * HEAD_DIM
KV_WIDTH = N_KV_HEADS * HEAD_DIM
WINDOW = 128
POOL_SIZES = (2, 4, 8, 16)
POOL_GROUP_WIDTH = 128
POOL_WIDTH = len(POOL_SIZES) * POOL_GROUP_WIDTH
POOL_STATE = max(POOL_SIZES) - 1
IN_WIDTH = ATTN_WIDTH + 2 * KV_WIDTH + POOL_WIDTH
D_FF = 2816
N_BUCKETS = 32
MAX_DISTANCE = 128
N_MOD = 9
MOD_WIDTH = N_MOD * D_MODEL
EPS = 1e-6

LANES = 128
SUBLANES = 8
MXU_COLS = 256
FF_CHUNK = 2 * MXU_COLS
FF_CHUNKS = tuple((c, min(FF_CHUNK, D_FF - c)) for c in range(0, D_FF, FF_CHUNK))
TM_PROMPT = 512
HIST = 16
VMEM_LIMIT = 56 * 1024 * 1024

F32 = jnp.float32
BF16 = jnp.bfloat16
NEG_INF = float("-inf")


def _head_of_row(r):
    return (r // 2) + 4 * (r % 2)


def _t5_bucket_np(dist):
    n = np.maximum(dist, 0)
    max_exact = N_BUCKETS // 2
    nf = np.maximum(n, 1).astype(np.float32)
    large = max_exact + (np.log(nf / max_exact) / math.log(MAX_DISTANCE / max_exact)
                         * (N_BUCKETS - max_exact)).astype(np.int32)
    large = np.minimum(large, N_BUCKETS - 1)
    return np.where(n < max_exact, n, large).astype(np.int32)


def _bucket_table():
    dist = np.arange(WINDOW)[:, None] + WINDOW - np.arange(2 * WINDOW)[None, :]
    return _t5_bucket_np(dist)


def _full(shape):
    return pl.BlockSpec(shape, lambda *_: (0,) * len(shape))


def _layer_resident(shape, layer):
    return pl.BlockSpec((None,) + tuple(shape), lambda *_: (layer,) + (0,) * len(shape),
                        pipeline_mode=pl.Buffered(1))


_SMEM = pl.BlockSpec(memory_space=pltpu.SMEM)


def _rms_mod(x, gain, shift, scale):
    y = x * lax.rsqrt(jnp.mean(x * x, axis=-1, keepdims=True) + EPS)
    return (y * gain) * (1.0 + scale) + shift


def _dot_by_row_halves(a_ref, w):
    half = a_ref.shape[0] // 2
    if half % (2 * SUBLANES) or half < MXU_COLS:
        return jnp.dot(a_ref[...], w, preferred_element_type=F32)
    return jnp.concatenate([jnp.dot(a_ref[:half, :], w, preferred_element_type=F32),
                            jnp.dot(a_ref[half:, :], w, preferred_element_type=F32)], axis=0)


def _swiglu(h_ref, act_ref, wg_ref, wu_ref, wd_ref, side_work=(), before_down=None):
    n = len(FF_CHUNKS)
    assert len(side_work) <= n
    after_chunk = {((j + 1) * n) // (len(side_work) + 1) - 1: work for j, work in enumerate(side_work)}
    chunk_dot = ((lambda a_ref, w: jnp.dot(a_ref[...], w, preferred_element_type=F32)) if side_work
                 else _dot_by_row_halves)
    for i, (c0, cw) in enumerate(FF_CHUNKS):
        g = chunk_dot(h_ref, wg_ref[:, c0:c0 + cw])
        u = chunk_dot(h_ref, wu_ref[:, c0:c0 + cw])
        act_ref[:, c0:c0 + cw] = ((g / (1.0 + jnp.exp(-g))) * u).astype(BF16)
        if i in after_chunk:
            after_chunk[i]()
    if before_down is not None:
        before_down()
    return _dot_by_row_halves(act_ref, wd_ref[...])


def _mod_getter(mod_ref, row=None):
    rows = slice(None) if row is None else pl.ds(row, 1)
    return lambda k: mod_ref[rows, k * D_MODEL:(k + 1) * D_MODEL]


def _mod_kernel(c_ref, w_ref, b_ref, o_ref):
    c = c_ref[...]
    a = (c / (1.0 + jnp.exp(-c))).astype(BF16)
    o_ref[...] = jnp.dot(a, w_ref[...].astype(BF16), preferred_element_type=F32) + b_ref[...]


def _modulation(c_all, w_ada, b_ada):
    rows = c_all.shape[0]
    n_chunk = 1024
    return pl.pallas_call(
        _mod_kernel,
        out_shape=jax.ShapeDtypeStruct((DEPTH, rows, MOD_WIDTH), F32),
        grid=(DEPTH, MOD_WIDTH // n_chunk),
        in_specs=[
            pl.BlockSpec((rows, D_MODEL), lambda l, j: (0, 0)),
            pl.BlockSpec((None, D_MODEL, n_chunk), lambda l, j: (l, 0, j)),
            pl.BlockSpec((None, 1, n_chunk), lambda l, j: (l, 0, j)),
        ],
        out_specs=pl.BlockSpec((None, rows, n_chunk), lambda l, j: (l, 0, j)),
        compiler_params=pltpu.CompilerParams(
            dimension_semantics=("arbitrary", "arbitrary"), vmem_limit_bytes=VMEM_LIMIT),
        name="adaln_modulation",
    )(c_all, w_ada, b_ada.reshape(DEPTH, 1, MOD_WIDTH))


def _bias_kernel(rb_ref, bucket_ref, o_ref):
    bucket = bucket_ref[...]
    eq = [bucket == b for b in range(N_BUCKETS)]
    for r in range(N_HEADS):
        acc = jnp.zeros((WINDOW, 2 * WINDOW), F32)
        for b in range(N_BUCKETS):
            acc = jnp.where(eq[b], rb_ref[b, _head_of_row(r)], acc)
        o_ref[r * WINDOW:(r + 1) * WINDOW, :] = acc


def _bias_table(rel_bias):
    return pl.pallas_call(
        _bias_kernel,
        out_shape=jax.ShapeDtypeStruct((N_HEADS * WINDOW, 2 * WINDOW), F32),
        in_specs=[_SMEM, pl.BlockSpec(memory_space=pltpu.VMEM)],
        out_specs=pl.BlockSpec(memory_space=pltpu.VMEM),
        name="rel_bias_table",
    )(rel_bias, jnp.asarray(_bucket_table()))


def _project_q(hm_ref, win_ref, q_ref):
    q_ref[...] = jnp.dot(hm_ref[...], win_ref[:, :ATTN_WIDTH], preferred_element_type=F32).astype(BF16)


def _project_kvu(hm_ref, win_ref, k_ref, v_ref, u_ref):
    z = jnp.dot(hm_ref[...], win_ref[:, ATTN_WIDTH:], preferred_element_type=F32)
    k_ref[...] = z[:, :KV_WIDTH]
    v_ref[...] = z[:, KV_WIDTH:2 * KV_WIDTH]
    u_ref[...] = z[:, 2 * KV_WIDTH:]
    return z


def _ffn1(x_ref, mod, gain_ref, wg_ref, wu_ref, wd_ref, x1_ref, h_ref, act_ref, layer, before_norm=None):
    x = x_ref[...]
    h_ref[...] = _rms_mod(x, gain_ref[layer, 0:1, :], mod(0), mod(1)).astype(BF16)
    x1 = x + 0.5 * mod(2) * _swiglu(h_ref, act_ref, wg_ref, wu_ref, wd_ref)
    x1_ref[...] = x1
    if before_norm is not None:
        before_norm()
    return _rms_mod(x1, gain_ref[layer, 1:2, :], mod(3), mod(4)).astype(BF16)


def _ka_sample_kernel(x_ref, mod_ref, gain_ref, wg_ref, wu_ref, wd_ref, win_ref,
                      x1_ref, q_ref, k_ref, v_ref, u_ref, h_ref, act_ref, *, layer):
    h_ref[...] = _ffn1(x_ref, _mod_getter(mod_ref), gain_ref, wg_ref, wu_ref, wd_ref, x1_ref, h_ref, act_ref, layer)
    _project_q(h_ref, win_ref, q_ref)
    _project_kvu(h_ref, win_ref, k_ref, v_ref, u_ref)


def _ka_prompt_kernel(x_ref, mod_ref, gain_ref, wg_ref, wu_ref, wd_ref, win_ref,
                      x1_ref, q_ref, k_ref, v_ref, u_ref, kl_ref, vl_ref, ul_ref,
                      h_ref, act_ref, hm_ref, *, layer, tiles_per_seq):
    tm = x_ref.shape[0]
    s = pl.program_id(0)
    main = jnp.minimum(s, pl.num_programs(0) - 2)

    @pl.when(s == 0)
    def _():
        hm_ref[...] = jnp.zeros_like(hm_ref)

    _project_q(hm_ref, win_ref, q_ref)

    def project_rest():
        z = _project_kvu(hm_ref, win_ref, k_ref, v_ref, u_ref)
        kl_ref[...] = z[tm - WINDOW:, :KV_WIDTH]
        vl_ref[...] = z[tm - WINDOW:, KV_WIDTH:2 * KV_WIDTH]
        ul_ref[...] = z[tm - HIST:, 2 * KV_WIDTH:]

    mod = _mod_getter(mod_ref, main // tiles_per_seq)
    hm = _ffn1(x_ref, mod, gain_ref, wg_ref, wu_ref, wd_ref, x1_ref, h_ref, act_ref, layer,
               before_norm=project_rest)
    hm_ref[...] = hm


def _ka_out_shapes(g, r):
    return (jax.ShapeDtypeStruct((g, r, D_MODEL), F32),
            jax.ShapeDtypeStruct((g, r, ATTN_WIDTH), BF16),
            jax.ShapeDtypeStruct((g, r, KV_WIDTH), F32),
            jax.ShapeDtypeStruct((g, r, KV_WIDTH), F32),
            jax.ShapeDtypeStruct((g, r, POOL_WIDTH), F32))


def _ka_weight_specs(layer):
    return [_full((DEPTH, 3, D_MODEL)),
            _layer_resident((D_MODEL, D_FF), layer), _layer_resident((D_MODEL, D_FF), layer),
            _layer_resident((D_FF, D_MODEL), layer), _layer_resident((D_MODEL, IN_WIDTH), layer)]


def _ka_sample_call(x, mod_all, mod_spec, gain, wg, wu, wd, win, layer, name):
    r = x.shape[0]
    row = lambda w: _full((r, w))
    return pl.pallas_call(
        functools.partial(_ka_sample_kernel, layer=layer),
        out_shape=tuple(jax.ShapeDtypeStruct(s.shape[1:], s.dtype) for s in _ka_out_shapes(1, r)),
        grid=(1,),
        in_specs=[row(D_MODEL), mod_spec] + _ka_weight_specs(layer),
        out_specs=(row(D_MODEL), row(ATTN_WIDTH), row(KV_WIDTH), row(KV_WIDTH), row(POOL_WIDTH)),
        scratch_shapes=[pltpu.VMEM((r, D_MODEL), BF16), pltpu.VMEM((r, D_FF), BF16)],
        compiler_params=pltpu.CompilerParams(dimension_semantics=("arbitrary",), vmem_limit_bytes=VMEM_LIMIT),
        name=name,
    )(x, mod_all, gain, wg, wu, wd, win)


def _ka_prompt_call(x, mod_all, mod_spec, gain, wg, wu, wd, win, layer, name):
    g, r, _ = x.shape
    tm = TM_PROMPT
    tps = r // tm
    n_tiles = g * tps
    main = lambda s: jnp.minimum(s, n_tiles - 1)
    prev = lambda s: jnp.maximum(s - 1, 0)
    main_row = lambda w: pl.BlockSpec((None, tm, w), lambda s: (main(s) // tps, main(s) % tps, 0))
    prev_row = lambda w: pl.BlockSpec((None, tm, w), lambda s: (prev(s) // tps, prev(s) % tps, 0))
    last = lambda rows, w: pl.BlockSpec((None, rows, w), lambda s: (prev(s) // tps, 0, 0))
    return pl.pallas_call(
        functools.partial(_ka_prompt_kernel, layer=layer, tiles_per_seq=tps),
        out_shape=_ka_out_shapes(g, r) + (jax.ShapeDtypeStruct((g, WINDOW, KV_WIDTH), F32),
                                          jax.ShapeDtypeStruct((g, WINDOW, KV_WIDTH), F32),
                                          jax.ShapeDtypeStruct((g, HIST, POOL_WIDTH), F32)),
        grid=(n_tiles + 1,),
        in_specs=[main_row(D_MODEL), mod_spec] + _ka_weight_specs(layer),
        out_specs=(main_row(D_MODEL), prev_row(ATTN_WIDTH), prev_row(KV_WIDTH), prev_row(KV_WIDTH),
                   prev_row(POOL_WIDTH), last(WINDOW, KV_WIDTH), last(WINDOW, KV_WIDTH), last(HIST, POOL_WIDTH)),
        scratch_shapes=[pltpu.VMEM((tm, D_MODEL), BF16), pltpu.VMEM((tm, D_FF), BF16),
                        pltpu.VMEM((tm, D_MODEL), BF16)],
        compiler_params=pltpu.CompilerParams(dimension_semantics=("arbitrary",), vmem_limit_bytes=VMEM_LIMIT),
        name=name,
    )(x, mod_all, gain, wg, wu, wd, win)


def _low_half():
    return lax.broadcasted_iota(jnp.int32, (WINDOW, LANES), 1) < HEAD_DIM


def _attn_probs(qb, kk, bias_ref, sinks_ref, layer, valid, p_ref):
    lo = _low_half()
    zero = jnp.zeros((WINDOW, LANES), BF16)
    pieces = []
    for c in range(4):
        qc = qb[:, c * LANES:(c + 1) * LANES]
        pieces += [jnp.where(lo, qc, zero), jnp.where(lo, zero, qc)]
    qs = jnp.concatenate(pieces, axis=0)
    s = lax.dot_general(qs, kk, (((1,), (1,)), ((), ())), preferred_element_type=F32)
    for r in range(N_HEADS):
        rows = slice(r * WINDOW, (r + 1) * WINDOW)
        sr = jnp.where(valid, s[rows] + bias_ref[rows, :], NEG_INF)
        sink = sinks_ref[layer, _head_of_row(r)]
        m = jnp.maximum(jnp.max(sr, axis=-1, keepdims=True), sink)
        p = jnp.exp(sr - m)
        denom = jnp.sum(p, axis=-1, keepdims=True) + jnp.exp(sink - m)
        p_ref[rows, :] = (p / denom).astype(BF16)


def _attn_values(p_ref, vv):
    lo = _low_half()
    o = jnp.dot(p_ref[...], vv, preferred_element_type=F32)
    cols = [jnp.where(lo, o[(2 * c) * WINDOW:(2 * c + 1) * WINDOW], o[(2 * c + 1) * WINDOW:(2 * c + 2) * WINDOW])
            for c in range(4)]
    return jnp.concatenate(cols, axis=1)


def _pool_sums(u_ref, up_ref, uext_ref, t1_ref, t2_ref, pooled_ref, seq_tile):
    tm = u_ref.shape[0]
    g_w = POOL_GROUP_WIDTH
    top = 2 * HIST
    end = top + tm
    assert POOL_SIZES == (2, 4, 8, 16) and HIST == 2 * SUBLANES
    uext_ref[0:HIST, :] = jnp.zeros((HIST, POOL_WIDTH), F32)
    uext_ref[HIST:top, :] = jnp.where(seq_tile == 0, 0.0, up_ref[...])
    uext_ref[top:, :] = u_ref[...]
    t1_ref[8:end, :] = uext_ref[8:end, :] + uext_ref[7:end - 1, :]
    t2_ref[16:end, :] = t1_ref[16:end, g_w:] + t1_ref[14:end - 2, g_w:]
    t1_ref[24:end, 2 * g_w:] = t2_ref[24:end, g_w:] + t2_ref[20:end - 4, g_w:]
    sums = [t1_ref[top:, 0:g_w], t2_ref[top:, 0:g_w], t1_ref[top:, 2 * g_w:3 * g_w],
            t1_ref[top:, 3 * g_w:] + t1_ref[top - 8:end - 8, 3 * g_w:]]
    pos = seq_tile * tm + lax.broadcasted_iota(jnp.int32, (tm, g_w), 0)
    for g, w in enumerate(POOL_SIZES):
        cols = slice(g * g_w, (g + 1) * g_w)
        count = jnp.minimum(pos + 1, w).astype(F32)
        pooled_ref[:, cols] = (sums[g] / count - uext_ref[top:, cols]).astype(BF16)


def _mix_prompt_steps(q_ref, k_ref, v_ref, kp_ref, vp_ref, bias_ref, sinks_ref, poolw_ref,
                      pscale_ref, mixed_ref, pooled_ref, p_ref, layer, seq_tile):
    tm = q_ref.shape[0]
    nb = tm // WINDOW
    first = seq_tile == 0

    def keys_values(ref, prev_ref, j):
        prev = prev_ref[...] if j == 0 else ref[(j - 1) * WINDOW:j * WINDOW, :]
        return jnp.concatenate([prev, ref[j * WINDOW:(j + 1) * WINDOW, :]], axis=0).astype(BF16)

    def probs(j):
        qi = lax.broadcasted_iota(jnp.int32, (WINDOW, 2 * WINDOW), 0)
        kj = lax.broadcasted_iota(jnp.int32, (WINDOW, 2 * WINDOW), 1)
        dist = qi + WINDOW - kj
        valid = (dist >= 0) & (dist <= WINDOW)
        if j == 0:
            valid = valid & (kj >= jnp.where(first, WINDOW, 0))
        _attn_probs(q_ref[j * WINDOW:(j + 1) * WINDOW, :], keys_values(k_ref, kp_ref, j), bias_ref, sinks_ref,
                    layer, valid, p_ref)

    def values(j):
        a = _attn_values(p_ref, keys_values(v_ref, vp_ref, j))
        mixed_ref[j * WINDOW:(j + 1) * WINDOW, 0:ATTN_WIDTH] = a.astype(BF16)

    def pool_map(g):
        cols = slice(g * POOL_GROUP_WIDTH, (g + 1) * POOL_GROUP_WIDTH)
        pg = jnp.dot(pooled_ref[:, cols], poolw_ref[g], preferred_element_type=F32) * pscale_ref[layer:layer + 1, cols]
        mixed_ref[:, ATTN_WIDTH + g * POOL_GROUP_WIDTH:ATTN_WIDTH + (g + 1) * POOL_GROUP_WIDTH] = pg.astype(BF16)

    assert nb == len(POOL_SIZES)

    def stage(i):
        def run():
            if i > 0:
                values(i - 1)
                pool_map(i - 1)
            if i < nb:
                probs(i)
        return run

    return [stage(i) for i in range(nb + 1)]


def _out_projection(x1, mixed_ref, wout_ref, mod, gain_ref, layer):
    x2 = x1 + mod(5) * _dot_by_row_halves(mixed_ref, wout_ref[...])
    return x2, _rms_mod(x2, gain_ref[layer, 2:3, :], mod(6), mod(7)).astype(BF16)


def _ffn2_out(x2, y, mod, fgain_ref, final):
    x3 = x2 + 0.5 * mod(8) * y
    if final:
        x3 = x3 * lax.rsqrt(jnp.mean(x3 * x3, axis=-1, keepdims=True) + EPS) * fgain_ref[...]
    return x3


def _kb_kernel(sinks_ref, x1_ref, q_ref, k_ref, v_ref, kp_ref, vp_ref, u_ref, up_ref, un_ref, upn_ref,
               mod_ref, gain_ref, bias_ref, poolw_ref, pscale_ref, wout_ref, wg_ref, wu_ref, wd_ref, fgain_ref,
               o_ref, mixed_ref, h_ref, x2_ref, act_ref, uext_ref, t1_ref, t2_ref, pooled_ref, p_ref,
               *, layer, final, tiles_per_seq):
    s = pl.program_id(0)
    n_tiles = pl.num_programs(0) - 1
    mix_tile = jnp.minimum(s, n_tiles - 1)
    next_tile = jnp.minimum(s + 1, n_tiles - 1)
    ffn_tile = jnp.maximum(s - 1, 0)
    slot = s % 2

    @pl.when(s == 0)
    def _():
        h_ref[...] = jnp.zeros_like(h_ref)
        x2_ref[1] = jnp.zeros(x2_ref.shape[1:], F32)
        _pool_sums(u_ref, up_ref, uext_ref, t1_ref, t2_ref, pooled_ref, 0)

    mix_steps = _mix_prompt_steps(q_ref, k_ref, v_ref, kp_ref, vp_ref, bias_ref, sinks_ref, poolw_ref, pscale_ref,
                                  mixed_ref, pooled_ref, p_ref, layer, mix_tile % tiles_per_seq)

    mix_mod = _mod_getter(mod_ref, mix_tile // tiles_per_seq)
    last_mix_step = mix_steps[-1]

    def finish_mix():
        last_mix_step()
        x2_ref[slot] = x1_ref[...] + mix_mod(5) * _dot_by_row_halves(mixed_ref, wout_ref[...])

    def norm_and_pool_ahead():
        h_ref[...] = _rms_mod(x2_ref[slot], gain_ref[layer, 2:3, :], mix_mod(6), mix_mod(7)).astype(BF16)
        _pool_sums(un_ref, upn_ref, uext_ref, t1_ref, t2_ref, pooled_ref, next_tile % tiles_per_seq)

    y = _swiglu(h_ref, act_ref, wg_ref, wu_ref, wd_ref, side_work=mix_steps[:-1] + [finish_mix],
                before_down=norm_and_pool_ahead)
    o_ref[...] = _ffn2_out(x2_ref[1 - slot], y, _mod_getter(mod_ref, ffn_tile // tiles_per_seq), fgain_ref, final)


def _kb_call(sinks, x1, q, k, v, u, mod_all, mod_spec, gain, bias, poolw, pscale, wout, wg, wu, wd, fgain,
             layer, final, name):
    g, r, _ = x1.shape
    tm = TM_PROMPT
    tps = r // tm
    n_tiles = g * tps
    nb = tm // WINDOW
    nh = tm // HIST
    mix = lambda s: jnp.minimum(s, n_tiles - 1)
    tail = lambda s: jnp.maximum(s - 1, 0)
    mix_row = lambda w: pl.BlockSpec((None, tm, w), lambda s: (mix(s) // tps, mix(s) % tps, 0))
    tail_row = lambda w: pl.BlockSpec((None, tm, w), lambda s: (tail(s) // tps, tail(s) % tps, 0))
    prev_kv = pl.BlockSpec((None, WINDOW, KV_WIDTH),
                           lambda s: (mix(s) // tps, jnp.maximum((mix(s) % tps) * nb - 1, 0), 0))
    nxt = lambda s: jnp.minimum(s + 1, n_tiles - 1)
    first_u = pl.BlockSpec((None, tm, POOL_WIDTH), lambda s: (0, 0, 0))
    first_hist = pl.BlockSpec((None, HIST, POOL_WIDTH), lambda s: (0, 0, 0))
    next_u = pl.BlockSpec((None, tm, POOL_WIDTH), lambda s: (nxt(s) // tps, nxt(s) % tps, 0))
    next_hist = pl.BlockSpec((None, HIST, POOL_WIDTH),
                             lambda s: (nxt(s) // tps, jnp.maximum((nxt(s) % tps) * nh - 1, 0), 0))
    return pl.pallas_call(
        functools.partial(_kb_kernel, layer=layer, final=final, tiles_per_seq=tps),
        out_shape=jax.ShapeDtypeStruct((g, r, D_MODEL), F32),
        grid=(n_tiles + 1,),
        in_specs=[_SMEM, mix_row(D_MODEL), mix_row(ATTN_WIDTH), mix_row(KV_WIDTH), mix_row(KV_WIDTH),
                  prev_kv, prev_kv, first_u, first_hist, next_u, next_hist, mod_spec, _full((DEPTH, 3, D_MODEL)),
                  _full((N_HEADS * WINDOW, 2 * WINDOW)),
                  _layer_resident((len(POOL_SIZES), POOL_GROUP_WIDTH, POOL_GROUP_WIDTH), layer),
                  _full((DEPTH, POOL_WIDTH)),
                  _layer_resident((D_MODEL, D_MODEL), layer), _layer_resident((D_MODEL, D_FF), layer),
                  _layer_resident((D_MODEL, D_FF), layer), _layer_resident((D_FF, D_MODEL), layer),
                  _full((1, D_MODEL))],
        out_specs=tail_row(D_MODEL),
        scratch_shapes=[pltpu.VMEM((tm, D_MODEL), BF16), pltpu.VMEM((tm, D_MODEL), BF16),
                        pltpu.VMEM((2, tm, D_MODEL), F32), pltpu.VMEM((tm, D_FF), BF16),
                        pltpu.VMEM((2 * HIST + tm, POOL_WIDTH), F32), pltpu.VMEM((2 * HIST + tm, POOL_WIDTH), F32),
                        pltpu.VMEM((2 * HIST + tm, POOL_WIDTH - POOL_GROUP_WIDTH), F32),
                        pltpu.VMEM((tm, POOL_WIDTH), BF16),
                        pltpu.VMEM((N_HEADS * WINDOW, 2 * WINDOW), BF16)],
        compiler_params=pltpu.CompilerParams(
            dimension_semantics=("arbitrary",), vmem_limit_bytes=VMEM_LIMIT),
        name=name,
    )(sinks, x1, q, k, v, k, v, u, u, u, u, mod_all, gain, bias, poolw, pscale, wout, wg, wu, wd, fgain)


def _ks_kernel(sinks_ref, q_ref, kn_ref, vn_ref, ck_ref, cv_ref, bias_ref, o_ref, nk_ref, nv_ref,
               qs_ref, knr_ref, vnr_ref, s_ref, p_ref, acc_ref, *, layer):
    bt = q_ref.shape[0]
    for b in range(bt):
        for cache_ref, new_ref, out_ref in ((ck_ref, kn_ref, nk_ref), (cv_ref, vn_ref, nv_ref)):
            out_ref[b, 0:WINDOW - 1, :] = cache_ref[b, 1:WINDOW, :]
            out_ref[b, WINDOW - 1:WINDOW, :] = new_ref[b:b + 1, :]
    rows = bt * N_HEADS
    nt = (((1,), (1,)), ((), ()))
    head_rows = lambda r: pl.ds(r, bt, stride=N_HEADS)
    lo = lax.broadcasted_iota(jnp.int32, (bt, LANES), 1) < HEAD_DIM
    kn = kn_ref[...].astype(BF16).astype(F32)
    vn = vn_ref[...].astype(BF16).astype(F32)
    for c in range(4):
        qc = q_ref[:, c * LANES:(c + 1) * LANES].astype(F32)
        qs_ref[head_rows(2 * c), :] = jnp.where(lo, qc, 0.0)
        qs_ref[head_rows(2 * c + 1), :] = jnp.where(lo, 0.0, qc)
    for r in range(N_HEADS):
        knr_ref[head_rows(r), :] = kn
        vnr_ref[head_rows(r), :] = vn

    for b in range(bt):
        rb = slice(b * N_HEADS, (b + 1) * N_HEADS)
        s_ref[rb, :] = lax.dot_general(qs_ref[rb, :].astype(BF16), ck_ref[b].astype(BF16), nt,
                                       preferred_element_type=F32)

    head = jnp.bitwise_and(lax.broadcasted_iota(jnp.int32, (rows, 1), 0), N_HEADS - 1)
    sink = jnp.zeros((rows, 1), F32)
    for r in range(N_HEADS):
        sink = jnp.where(head == r, sinks_ref[layer, _head_of_row(r)], sink)
    bias = jnp.concatenate([bias_ref[...]] * bt, axis=0)
    s_c = s_ref[...] + bias[:, 0:WINDOW]
    s_n = jnp.sum(qs_ref[...] * knr_ref[...], axis=-1, keepdims=True) + bias[:, WINDOW:WINDOW + 1]
    m = jnp.maximum(jnp.maximum(jnp.max(s_c, axis=-1, keepdims=True), s_n), sink)
    p_c = jnp.exp(s_c - m)
    p_n = jnp.exp(s_n - m)
    denom = jnp.sum(p_c, axis=-1, keepdims=True) + p_n + jnp.exp(sink - m)
    p_ref[...] = p_c / denom

    for b in range(bt):
        rb = slice(b * N_HEADS, (b + 1) * N_HEADS)
        acc_ref[rb, :] = jnp.dot(p_ref[rb, :].astype(BF16), cv_ref[b].astype(BF16), preferred_element_type=F32)
    acc_ref[...] = acc_ref[...] + (p_n / denom).astype(BF16).astype(F32) * vnr_ref[...]

    for c in range(4):
        o_ref[:, c * LANES:(c + 1) * LANES] = jnp.where(
            lo, acc_ref[head_rows(2 * c), :], acc_ref[head_rows(2 * c + 1), :]).astype(BF16)


def _ks_call(sinks, q, kn, vn, cache_k, cache_v, layer, bias_s):
    nb = q.shape[0]
    bt = 32
    rows = bt * N_HEADS
    per_b = lambda w: pl.BlockSpec((bt, w), lambda i: (i, 0))
    cache = pl.BlockSpec((None, bt, WINDOW, KV_WIDTH), lambda i: (layer, i, 0, 0))
    new_cache = pl.BlockSpec((bt, WINDOW, KV_WIDTH), lambda i: (i, 0, 0))
    return pl.pallas_call(
        functools.partial(_ks_kernel, layer=layer),
        out_shape=(jax.ShapeDtypeStruct((nb, ATTN_WIDTH), BF16),
                   jax.ShapeDtypeStruct((nb, WINDOW, KV_WIDTH), F32),
                   jax.ShapeDtypeStruct((nb, WINDOW, KV_WIDTH), F32)),
        grid=(nb // bt,),
        in_specs=[_SMEM, per_b(ATTN_WIDTH), per_b(KV_WIDTH), per_b(KV_WIDTH), cache, cache,
                  _full((N_HEADS, 2 * WINDOW))],
        out_specs=(per_b(ATTN_WIDTH), new_cache, new_cache),
        scratch_shapes=[pltpu.VMEM((rows, KV_WIDTH), F32), pltpu.VMEM((rows, KV_WIDTH), F32),
                        pltpu.VMEM((rows, KV_WIDTH), F32), pltpu.VMEM((rows, WINDOW), F32),
                        pltpu.VMEM((rows, WINDOW), F32), pltpu.VMEM((rows, KV_WIDTH), F32)],
        compiler_params=pltpu.CompilerParams(dimension_semantics=("arbitrary",), vmem_limit_bytes=VMEM_LIMIT),
        name="sample_attention",
    )(sinks, q, kn, vn, cache_k, cache_v, bias_s)


def _kc_kernel(x1_ref, attn_ref, u_ref, hist_ref, mod_ref, gain_ref, poolw_ref, pscale_ref,
               wout_ref, wg_ref, wu_ref, wd_ref, fgain_ref, o_ref, mixed_ref, h_ref, act_ref, *, layer, final):
    mixed_ref[:, 0:ATTN_WIDTH] = attn_ref[...]
    for g, w in enumerate(POOL_SIZES):
        cols = slice(g * POOL_GROUP_WIDTH, (g + 1) * POOL_GROUP_WIDTH)
        ug = u_ref[:, cols]
        acc = ug
        for d in range(1, w):
            r = POOL_STATE - d
            acc = acc + hist_ref[:, r * POOL_WIDTH + g * POOL_GROUP_WIDTH:r * POOL_WIDTH + (g + 1) * POOL_GROUP_WIDTH]
        pooled = (acc / float(w) - ug).astype(BF16)
        pg = jnp.dot(pooled, poolw_ref[g], preferred_element_type=F32) * pscale_ref[layer:layer + 1, cols]
        mixed_ref[:, ATTN_WIDTH + g * POOL_GROUP_WIDTH:ATTN_WIDTH + (g + 1) * POOL_GROUP_WIDTH] = pg.astype(BF16)
    mod = _mod_getter(mod_ref)
    x2, h = _out_projection(x1_ref[...], mixed_ref, wout_ref, mod, gain_ref, layer)
    h_ref[...] = h
    o_ref[...] = _ffn2_out(x2, _swiglu(h_ref, act_ref, wg_ref, wu_ref, wd_ref), mod, fgain_ref, final)


def _kc_call(x1, attn, u, hist, mod_all, mod_spec, gain, poolw, pscale, wout, wg, wu, wd, fgain, layer, final, name):
    r = x1.shape[0]
    return pl.pallas_call(
        functools.partial(_kc_kernel, layer=layer, final=final),
        out_shape=jax.ShapeDtypeStruct((r, D_MODEL), F32),
        grid=(1,),
        in_specs=[_full((r, D_MODEL)), _full((r, ATTN_WIDTH)), _full((r, POOL_WIDTH)),
                  pl.BlockSpec((None, r, POOL_STATE * POOL_WIDTH), lambda *_: (layer, 0, 0)),
                  mod_spec, _full((DEPTH, 3, D_MODEL)),
                  _layer_resident((len(POOL_SIZES), POOL_GROUP_WIDTH, POOL_GROUP_WIDTH), layer),
                  _full((DEPTH, POOL_WIDTH)),
                  _layer_resident((D_MODEL, D_MODEL), layer), _layer_resident((D_MODEL, D_FF), layer),
                  _layer_resident((D_MODEL, D_FF), layer), _layer_resident((D_FF, D_MODEL), layer),
                  _full((1, D_MODEL))],
        out_specs=_full((r, D_MODEL)),
        scratch_shapes=[pltpu.VMEM((r, D_MODEL), BF16), pltpu.VMEM((r, D_MODEL), BF16),
                        pltpu.VMEM((r, D_FF), BF16)],
        compiler_params=pltpu.CompilerParams(dimension_semantics=("arbitrary",), vmem_limit_bytes=VMEM_LIMIT),
        name=name,
    )(x1, attn, u, hist, mod_all, gain, poolw, pscale, wout, wg, wu, wd, fgain)


def kernel(x_prompt, x_sample, c_prompt, c_sample, cache_k, cache_v, state_pool, w_ada, b_ada, norm_gain,
           w_in, w_out, sinks, rel_bias, pool_w, pool_scale, ffn1_wg, ffn1_wu, ffn1_wd, ffn2_wg, ffn2_wu,
           ffn2_wd, final_gain):
    n_p, seq, _ = x_prompt.shape
    n_s = x_sample.shape[0]

    mod_all = _modulation(jnp.concatenate([c_sample, c_prompt], axis=0), w_ada, b_ada)
    bias = _bias_table(rel_bias)
    bias_s = bias.reshape(N_HEADS, WINDOW, 2 * WINDOW)[:, 0, :]
    fgain = final_gain.reshape(1, D_MODEL)

    wq = (w_in[:, :, :ATTN_WIDTH].reshape(DEPTH, D_MODEL, 2, 4, HEAD_DIM).transpose(0, 1, 3, 2, 4)
          .reshape(DEPTH, D_MODEL, ATTN_WIDTH)) * (HEAD_DIM ** -0.5)
    win = jnp.concatenate([wq, w_in[:, :, ATTN_WIDTH:]], axis=2).astype(BF16)
    wo_attn = (w_out[:, :ATTN_WIDTH].reshape(DEPTH, 2, 4, HEAD_DIM, D_MODEL).transpose(0, 2, 1, 3, 4)
               .reshape(DEPTH, ATTN_WIDTH, D_MODEL))
    wout = jnp.concatenate([wo_attn, w_out[:, ATTN_WIDTH:]], axis=1).astype(BF16)
    wg1, wu1, wd1 = ffn1_wg.astype(BF16), ffn1_wu.astype(BF16), ffn1_wd.astype(BF16)
    wg2, wu2, wd2 = ffn2_wg.astype(BF16), ffn2_wu.astype(BF16), ffn2_wd.astype(BF16)
    poolw = pool_w.astype(BF16)

    xp = x_prompt
    xs = x_sample.reshape(n_s, D_MODEL)
    ck = cache_k.reshape(DEPTH, n_s, WINDOW, KV_WIDTH)
    cv = cache_v.reshape(DEPTH, n_s, WINDOW, KV_WIDTH)
    hist = state_pool.reshape(DEPTH, n_s, POOL_STATE * POOL_WIDTH)

    new_kp, new_vp, new_pp, new_ks, new_vs, new_ps = [], [], [], [], [], []
    for l in range(DEPTH):
        final = l == DEPTH - 1
        mod_p = pl.BlockSpec((None, n_p, MOD_WIDTH), lambda *_, l=l: (l, n_s // n_p, 0))
        mod_s = pl.BlockSpec((None, n_s, MOD_WIDTH), lambda *_, l=l: (l, 0, 0))

        x1, q, k, v, u, k_last, v_last, u_last = _ka_prompt_call(
            xp, mod_all, mod_p, norm_gain, wg1, wu1, wd1, win, l, f"prompt_ffn1_inproj_l{l}")
        xp = _kb_call(sinks, x1, q, k, v, u, mod_all, mod_p, norm_gain, bias, poolw, pool_scale, wout,
                      wg2, wu2, wd2, fgain, l, final, f"prompt_mixer_ffn2_l{l}")
        new_kp.append(k_last)
        new_vp.append(v_last)
        new_pp.append(u_last)

        x1s, qs, ks, vs, us = _ka_sample_call(xs, mod_all, mod_s, norm_gain, wg1, wu1, wd1, win, l,
                                              f"sample_ffn1_inproj_l{l}")
        attn_s, ck_next, cv_next = _ks_call(sinks, qs, ks, vs, ck, cv, l, bias_s)
        xs = _kc_call(x1s, attn_s, us, hist, mod_all, mod_s, norm_gain, poolw, pool_scale, wout, wg2, wu2, wd2,
                      fgain, l, final, f"sample_mixer_ffn2_l{l}")
        new_ks.append(ck_next)
        new_vs.append(cv_next)
        new_ps.append(us)

    new_pool_sample = jnp.concatenate(
        [state_pool[:, :, 1:], jnp.stack(new_ps).reshape(DEPTH, n_s, 1, POOL_WIDTH)], axis=2)
    kv_prompt = (DEPTH, n_p, WINDOW, N_KV_HEADS, HEAD_DIM)
    kv_sample = (DEPTH, n_s, WINDOW, N_KV_HEADS, HEAD_DIM)
    return (xp, xs.reshape(n_s, 1, D_MODEL), jnp.stack(new_kp).reshape(kv_prompt), jnp.stack(new_vp).reshape(kv_prompt),
            jnp.stack(new_pp)[:, :, HIST - POOL_STATE:], jnp.stack(new_ks).reshape(kv_sample),
            jnp.stack(new_vs).reshape(kv_sample), new_pool_sample)
```

```python
import functools
import math

import numpy as np
import jax
import jax.numpy as jnp
from jax import lax
from jax.experimental import pallas as pl
from jax.experimental.pallas import tpu as pltpu

D_MODEL = 1024
DEPTH = 2
HEAD_DIM = 64
N_HEADS = 8
N_KV_HEADS = 2
ATTN_WIDTH = N_HEADS * HEAD_DIM
KV_WIDTH = N_KV_HEADS * HEAD_DIM
WINDOW = 128
POOL_SIZES = (2, 4, 8, 16)
POOL_GROUP_WIDTH = 128
POOL_WIDTH = len(POOL_SIZES) * POOL_GROUP_WIDTH
POOL_STATE = max(POOL_SIZES) - 1
IN_WIDTH = ATTN_WIDTH + 2 * KV_WIDTH + POOL_WIDTH
D_FF = 2816
N_BUCKETS = 32
MAX_DISTANCE = 128
N_MOD = 9
MOD_WIDTH = N_MOD * D_MODEL
EPS = 1e-6

LANES = 128
SUBLANES = 8
MXU_COLS = 256
FF_CHUNK = 2 * MXU_COLS
FF_CHUNKS = tuple((c, min(FF_CHUNK, D_FF - c)) for c in range(0, D_FF, FF_CHUNK))
TM_PROMPT = 512
HIST = 16
VMEM_LIMIT = 56 * 1024 * 1024

F32 = jnp.float32
BF16 = jnp.bfloat16
NEG_INF = float("-inf")


def _head_of_row(r):
    return (r // 2) + 4 * (r % 2)


def _t5_bucket_np(dist):
    n = np.maximum(dist, 0)
    max_exact = N_BUCKETS // 2
    nf = np.maximum(n, 1).astype(np.float32)
    large = max_exact + (np.log(nf / max_exact) / math.log(MAX_DISTANCE / max_exact)
                         * (N_BUCKETS - max_exact)).astype(np.int32)
    large = np.minimum(large, N_BUCKETS - 1)
    return np.where(n < max_exact, n, large).astype(np.int32)


def _bucket_table():
    dist = np.arange(WINDOW)[:, None] + WINDOW - np.arange(2 * WINDOW)[None, :]
    return _t5_bucket_np(dist)


def _full(shape):
    return pl.BlockSpec(shape, lambda *_: (0,) * len(shape))


def _layer_resident(shape, layer):
    return pl.BlockSpec((None,) + tuple(shape), lambda *_: (layer,) + (0,) * len(shape),
                        pipeline_mode=pl.Buffered(1))


_SMEM = pl.BlockSpec(memory_space=pltpu.SMEM)


def _rms_mod(x, gain, shift, scale):
    y = x * lax.rsqrt(jnp.mean(x * x, axis=-1, keepdims=True) + EPS)
    return (y * gain) * (1.0 + scale) + shift


def _dot_by_row_halves(a_ref, w):
    half = a_ref.shape[0] // 2
    if half % (2 * SUBLANES) or half < MXU_COLS:
        return jnp.dot(a_ref[...], w, preferred_element_type=F32)
    return jnp.concatenate([jnp.dot(a_ref[:half, :], w, preferred_element_type=F32),
                            jnp.dot(a_ref[half:, :], w, preferred_element_type=F32)], axis=0)


def _swiglu(h_ref, act_ref, wg_ref, wu_ref, wd_ref, side_work=(), before_down=None):
    n = len(FF_CHUNKS)
    assert len(side_work) <= n
    after_chunk = {((j + 1) * n) // (len(side_work) + 1) - 1: work for j, work in enumerate(side_work)}
    chunk_dot = ((lambda a_ref, w: jnp.dot(a_ref[...], w, preferred_element_type=F32)) if side_work
                 else _dot_by_row_halves)
    for i, (c0, cw) in enumerate(FF_CHUNKS):
        g = chunk_dot(h_ref, wg_ref[:, c0:c0 + cw])
        u = chunk_dot(h_ref, wu_ref[:, c0:c0 + cw])
        act_ref[:, c0:c0 + cw] = ((g / (1.0 + jnp.exp(-g))) * u).astype(BF16)
        if i in after_chunk:
            after_chunk[i]()
    if before_down is not None:
        before_down()
    return _dot_by_row_halves(act_ref, wd_ref[...])


def _mod_getter(mod_ref, row=None):
    rows = slice(None) if row is None else pl.ds(row, 1)
    return lambda k: mod_ref[rows, k * D_MODEL:(k + 1) * D_MODEL]


def _mod_kernel(c_ref, w_ref, b_ref, o_ref):
    c = c_ref[...]
    a = (c / (1.0 + jnp.exp(-c))).astype(BF16)
    o_ref[...] = jnp.dot(a, w_ref[...].astype(BF16), preferred_element_type=F32) + b_ref[...]


def _modulation(c_all, w_ada, b_ada):
    rows = c_all.shape[0]
    n_chunk = 1024
    return pl.pallas_call(
        _mod_kernel,
        out_shape=jax.ShapeDtypeStruct((DEPTH, rows, MOD_WIDTH), F32),
        grid=(DEPTH, MOD_WIDTH // n_chunk),
        in_specs=[
            pl.BlockSpec((rows, D_MODEL), lambda l, j: (0, 0)),
            pl.BlockSpec((None, D_MODEL, n_chunk), lambda l, j: (l, 0, j)),
            pl.BlockSpec((None, 1, n_chunk), lambda l, j: (l, 0, j)),
        ],
        out_specs=pl.BlockSpec((None, rows, n_chunk), lambda l, j: (l, 0, j)),
        compiler_params=pltpu.CompilerParams(
            dimension_semantics=("arbitrary", "arbitrary"), vmem_limit_bytes=VMEM_LIMIT),
        name="adaln_modulation",
    )(c_all, w_ada, b_ada.reshape(DEPTH, 1, MOD_WIDTH))


def _bias_kernel(rb_ref, bucket_ref, o_ref):
    bucket = bucket_ref[...]
    eq = [bucket == b for b in range(N_BUCKETS)]
    for r in range(N_HEADS):
        acc = jnp.zeros((WINDOW, 2 * WINDOW), F32)
        for b in range(N_BUCKETS):
            acc = jnp.where(eq[b], rb_ref[b, _head_of_row(r)], acc)
        o_ref[r * WINDOW:(r + 1) * WINDOW, :] = acc


def _bias_table(rel_bias):
    return pl.pallas_call(
        _bias_kernel,
        out_shape=jax.ShapeDtypeStruct((N_HEADS * WINDOW, 2 * WINDOW), F32),
        in_specs=[_SMEM, pl.BlockSpec(memory_space=pltpu.VMEM)],
        out_specs=pl.BlockSpec(memory_space=pltpu.VMEM),
        name="rel_bias_table",
    )(rel_bias, jnp.asarray(_bucket_table()))


def _project_q(hm_ref, win_ref, q_ref):
    q_ref[...] = jnp.dot(hm_ref[...], win_ref[:, :ATTN_WIDTH], preferred_element_type=F32).astype(BF16)


def _project_kvu(hm_ref, win_ref, k_ref, v_ref, u_ref):
    z = jnp.dot(hm_ref[...], win_ref[:, ATTN_WIDTH:], preferred_element_type=F32)
    k_ref[...] = z[:, :KV_WIDTH]
    v_ref[...] = z[:, KV_WIDTH:2 * KV_WIDTH]
    u_ref[...] = z[:, 2 * KV_WIDTH:]
    return z


def _ffn1(x_ref, mod, gain_ref, wg_ref, wu_ref, wd_ref, x1_ref, h_ref, act_ref, layer, before_norm=None):
    x = x_ref[...]
    h_ref[...] = _rms_mod(x, gain_ref[layer, 0:1, :], mod(0), mod(1)).astype(BF16)
    x1 = x + 0.5 * mod(2) * _swiglu(h_ref, act_ref, wg_ref, wu_ref, wd_ref)
    x1_ref[...] = x1
    if before_norm is not None:
        before_norm()
    return _rms_mod(x1, gain_ref[layer, 1:2, :], mod(3), mod(4)).astype(BF16)


def _ka_sample_kernel(x_ref, mod_ref, gain_ref, wg_ref, wu_ref, wd_ref, win_ref,
                      x1_ref, q_ref, k_ref, v_ref, u_ref, h_ref, act_ref, *, layer):
    h_ref[...] = _ffn1(x_ref, _mod_getter(mod_ref), gain_ref, wg_ref, wu_ref, wd_ref, x1_ref, h_ref, act_ref, layer)
    _project_q(h_ref, win_ref, q_ref)
    _project_kvu(h_ref, win_ref, k_ref, v_ref, u_ref)


def _ka_prompt_kernel(x_ref, mod_ref, gain_ref, wg_ref, wu_ref, wd_ref, win_ref,
                      x1_ref, q_ref, k_ref, v_ref, u_ref, kl_ref, vl_ref, ul_ref,
                      h_ref, act_ref, hm_ref, *, layer, tiles_per_seq):
    tm = x_ref.shape[0]
    s = pl.program_id(0)
    n_tiles = pl.num_programs(0) - 1

    @pl.when(s == 0)
    def _():
        hm_ref[...] = jnp.zeros_like(hm_ref)

    def project_rest():
        z = _project_kvu(hm_ref, win_ref, k_ref, v_ref, u_ref)
        kl_ref[...] = z[tm - WINDOW:, :KV_WIDTH]
        vl_ref[...] = z[tm - WINDOW:, KV_WIDTH:2 * KV_WIDTH]
        ul_ref[...] = z[tm - HIST:, 2 * KV_WIDTH:]

    @pl.when(s < n_tiles)
    def _():
        _project_q(hm_ref, win_ref, q_ref)
        mod = _mod_getter(mod_ref, s // tiles_per_seq)
        hm = _ffn1(x_ref, mod, gain_ref, wg_ref, wu_ref, wd_ref, x1_ref, h_ref, act_ref, layer,
                   before_norm=project_rest)
        hm_ref[...] = hm

    @pl.when(s == n_tiles)
    def _():
        _project_q(hm_ref, win_ref, q_ref)
        project_rest()


def _ka_out_shapes(g, r):
    return (jax.ShapeDtypeStruct((g, r, D_MODEL), F32),
            jax.ShapeDtypeStruct((g, r, ATTN_WIDTH), BF16),
            jax.ShapeDtypeStruct((g, r, KV_WIDTH), F32),
            jax.ShapeDtypeStruct((g, r, KV_WIDTH), F32),
            jax.ShapeDtypeStruct((g, r, POOL_WIDTH), F32))


def _ka_weight_specs(layer):
    return [_full((DEPTH, 3, D_MODEL)),
            _layer_resident((D_MODEL, D_FF), layer), _layer_resident((D_MODEL, D_FF), layer),
            _layer_resident((D_FF, D_MODEL), layer), _layer_resident((D_MODEL, IN_WIDTH), layer)]


def _ka_sample_call(x, mod_all, mod_spec, gain, wg, wu, wd, win, layer, name):
    r = x.shape[0]
    row = lambda w: _full((r, w))
    return pl.pallas_call(
        functools.partial(_ka_sample_kernel, layer=layer),
        out_shape=tuple(jax.ShapeDtypeStruct(s.shape[1:], s.dtype) for s in _ka_out_shapes(1, r)),
        grid=(1,),
        in_specs=[row(D_MODEL), mod_spec] + _ka_weight_specs(layer),
        out_specs=(row(D_MODEL), row(ATTN_WIDTH), row(KV_WIDTH), row(KV_WIDTH), row(POOL_WIDTH)),
        scratch_shapes=[pltpu.VMEM((r, D_MODEL), BF16), pltpu.VMEM((r, D_FF), BF16)],
        compiler_params=pltpu.CompilerParams(dimension_semantics=("arbitrary",), vmem_limit_bytes=VMEM_LIMIT),
        name=name,
    )(x, mod_all, gain, wg, wu, wd, win)


def _ka_prompt_call(x, mod_all, mod_spec, gain, wg, wu, wd, win, layer, name):
    g, r, _ = x.shape
    tm = TM_PROMPT
    tps = r // tm
    n_tiles = g * tps
    main = lambda s: jnp.minimum(s, n_tiles - 1)
    prev = lambda s: jnp.maximum(s - 1, 0)
    main_row = lambda w: pl.BlockSpec((None, tm, w), lambda s: (main(s) // tps, main(s) % tps, 0))
    prev_row = lambda w: pl.BlockSpec((None, tm, w), lambda s: (prev(s) // tps, prev(s) % tps, 0))
    last = lambda rows, w: pl.BlockSpec((None, rows, w), lambda s: (prev(s) // tps, 0, 0))
    return pl.pallas_call(
        functools.partial(_ka_prompt_kernel, layer=layer, tiles_per_seq=tps),
        out_shape=_ka_out_shapes(g, r) + (jax.ShapeDtypeStruct((g, WINDOW, KV_WIDTH), F32),
                                          jax.ShapeDtypeStruct((g, WINDOW, KV_WIDTH), F32),
                                          jax.ShapeDtypeStruct((g, HIST, POOL_WIDTH), F32)),
        grid=(n_tiles + 1,),
        in_specs=[main_row(D_MODEL), mod_spec] + _ka_weight_specs(layer),
        out_specs=(main_row(D_MODEL), prev_row(ATTN_WIDTH), prev_row(KV_WIDTH), prev_row(KV_WIDTH),
                   prev_row(POOL_WIDTH), last(WINDOW, KV_WIDTH), last(WINDOW, KV_WIDTH), last(HIST, POOL_WIDTH)),
        scratch_shapes=[pltpu.VMEM((tm, D_MODEL), BF16), pltpu.VMEM((tm, D_FF), BF16),
                        pltpu.VMEM((tm, D_MODEL), BF16)],
        compiler_params=pltpu.CompilerParams(dimension_semantics=("arbitrary",), vmem_limit_bytes=VMEM_LIMIT),
        name=name,
    )(x, mod_all, gain, wg, wu, wd, win)


def _low_half():
    return lax.broadcasted_iota(jnp.int32, (WINDOW, LANES), 1) < HEAD_DIM


def _attn_probs(qb, kk, bias_ref, sinks_ref, layer, valid, p_ref):
    lo = _low_half()
    zero = jnp.zeros((WINDOW, LANES), BF16)
    pieces = []
    for c in range(4):
        qc = qb[:, c * LANES:(c + 1) * LANES]
        pieces += [jnp.where(lo, qc, zero), jnp.where(lo, zero, qc)]
    qs = jnp.concatenate(pieces, axis=0)
    s = lax.dot_general(qs, kk, (((1,), (1,)), ((), ())), preferred_element_type=F32)
    for r in range(N_HEADS):
        rows = slice(r * WINDOW, (r + 1) * WINDOW)
        sr = jnp.where(valid, s[rows] + bias_ref[rows, :], NEG_INF)
        sink = sinks_ref[layer, _head_of_row(r)]
        m = jnp.maximum(jnp.max(sr, axis=-1, keepdims=True), sink)
        p = jnp.exp(sr - m)
        denom = jnp.sum(p, axis=-1, keepdims=True) + jnp.exp(sink - m)
        p_ref[rows, :] = (p / denom).astype(BF16)


def _attn_values(p_ref, vv):
    lo = _low_half()
    o = jnp.dot(p_ref[...], vv, preferred_element_type=F32)
    cols = [jnp.where(lo, o[(2 * c) * WINDOW:(2 * c + 1) * WINDOW], o[(2 * c + 1) * WINDOW:(2 * c + 2) * WINDOW])
            for c in range(4)]
    return jnp.concatenate(cols, axis=1)


def _pool_sums(u_ref, up_ref, uext_ref, t1_ref, t2_ref, pooled_ref, seq_tile):
    tm = u_ref.shape[0]
    g_w = POOL_GROUP_WIDTH
    top = 2 * HIST
    end = top + tm
    assert POOL_SIZES == (2, 4, 8, 16) and HIST == 2 * SUBLANES
    uext_ref[0:HIST, :] = jnp.zeros((HIST, POOL_WIDTH), F32)
    uext_ref[HIST:top, :] = jnp.where(seq_tile == 0, 0.0, up_ref[...])
    uext_ref[top:, :] = u_ref[...]
    t1_ref[8:end, :] = uext_ref[8:end, :] + uext_ref[7:end - 1, :]
    t2_ref[16:end, :] = t1_ref[16:end, g_w:] + t1_ref[14:end - 2, g_w:]
    t1_ref[24:end, 2 * g_w:] = t2_ref[24:end, g_w:] + t2_ref[20:end - 4, g_w:]
    sums = [t1_ref[top:, 0:g_w], t2_ref[top:, 0:g_w], t1_ref[top:, 2 * g_w:3 * g_w],
            t1_ref[top:, 3 * g_w:] + t1_ref[top - 8:end - 8, 3 * g_w:]]
    pos = seq_tile * tm + lax.broadcasted_iota(jnp.int32, (tm, g_w), 0)
    for g, w in enumerate(POOL_SIZES):
        cols = slice(g * g_w, (g + 1) * g_w)
        count = jnp.minimum(pos + 1, w).astype(F32)
        pooled_ref[:, cols] = (sums[g] / count - uext_ref[top:, cols]).astype(BF16)


def _mix_prompt_steps(q_ref, k_ref, v_ref, kp_ref, vp_ref, bias_ref, sinks_ref, poolw_ref,
                      pscale_ref, mixed_ref, pooled_ref, p_ref, layer, seq_tile):
    tm = q_ref.shape[0]
    nb = tm // WINDOW
    first = seq_tile == 0

    def keys_values(ref, prev_ref, j):
        prev = prev_ref[...] if j == 0 else ref[(j - 1) * WINDOW:j * WINDOW, :]
        return jnp.concatenate([prev, ref[j * WINDOW:(j + 1) * WINDOW, :]], axis=0).astype(BF16)

    def probs(j):
        qi = lax.broadcasted_iota(jnp.int32, (WINDOW, 2 * WINDOW), 0)
        kj = lax.broadcasted_iota(jnp.int32, (WINDOW, 2 * WINDOW), 1)
        dist = qi + WINDOW - kj
        valid = (dist >= 0) & (dist <= WINDOW)
        if j == 0:
            valid = valid & (kj >= jnp.where(first, WINDOW, 0))
        _attn_probs(q_ref[j * WINDOW:(j + 1) * WINDOW, :], keys_values(k_ref, kp_ref, j), bias_ref, sinks_ref,
                    layer, valid, p_ref)

    def values(j):
        a = _attn_values(p_ref, keys_values(v_ref, vp_ref, j))
        mixed_ref[j * WINDOW:(j + 1) * WINDOW, 0:ATTN_WIDTH] = a.astype(BF16)

    def pool_map(g):
        cols = slice(g * POOL_GROUP_WIDTH, (g + 1) * POOL_GROUP_WIDTH)
        pg = jnp.dot(pooled_ref[:, cols], poolw_ref[g], preferred_element_type=F32) * pscale_ref[layer:layer + 1, cols]
        mixed_ref[:, ATTN_WIDTH + g * POOL_GROUP_WIDTH:ATTN_WIDTH + (g + 1) * POOL_GROUP_WIDTH] = pg.astype(BF16)

    assert nb == len(POOL_SIZES)

    def stage(i):
        def run():
            if i > 0:
                values(i - 1)
                pool_map(i - 1)
            if i < nb:
                probs(i)
        return run

    return [stage(i) for i in range(nb + 1)]


def _out_projection(x1, mixed_ref, wout_ref, mod, gain_ref, layer):
    x2 = x1 + mod(5) * _dot_by_row_halves(mixed_ref, wout_ref[...])
    return x2, _rms_mod(x2, gain_ref[layer, 2:3, :], mod(6), mod(7)).astype(BF16)


def _ffn2_out(x2, y, mod, fgain_ref, final):
    x3 = x2 + 0.5 * mod(8) * y
    if final:
        x3 = x3 * lax.rsqrt(jnp.mean(x3 * x3, axis=-1, keepdims=True) + EPS) * fgain_ref[...]
    return x3


def _kb_kernel(sinks_ref, x1_ref, q_ref, k_ref, v_ref, kp_ref, vp_ref, u_ref, up_ref, un_ref, upn_ref,
               mod_ref, gain_ref, bias_ref, poolw_ref, pscale_ref, wout_ref, wg_ref, wu_ref, wd_ref, fgain_ref,
               o_ref, mixed_ref, h_ref, x2_ref, act_ref, uext_ref, t1_ref, t2_ref, pooled_ref, p_ref,
               *, layer, final, tiles_per_seq):
    s = pl.program_id(0)
    n_tiles = pl.num_programs(0) - 1
    mix_tile = jnp.minimum(s, n_tiles - 1)
    next_tile = jnp.minimum(s + 1, n_tiles - 1)
    ffn_tile = jnp.maximum(s - 1, 0)
    slot = s % 2

    mix_steps = _mix_prompt_steps(q_ref, k_ref, v_ref, kp_ref, vp_ref, bias_ref, sinks_ref, poolw_ref, pscale_ref,
                                  mixed_ref, pooled_ref, p_ref, layer, mix_tile % tiles_per_seq)

    mix_mod = _mod_getter(mod_ref, mix_tile // tiles_per_seq)
    last_mix_step = mix_steps[-1]

    def finish_mix():
        last_mix_step()
        x2_ref[slot] = x1_ref[...] + mix_mod(5) * _dot_by_row_halves(mixed_ref, wout_ref[...])

    def norm_and_pool_ahead():
        h_ref[...] = _rms_mod(x2_ref[slot], gain_ref[layer, 2:3, :], mix_mod(6), mix_mod(7)).astype(BF16)
        _pool_sums(un_ref, upn_ref, uext_ref, t1_ref, t2_ref, pooled_ref, next_tile % tiles_per_seq)

    def ffn2(**mixing):
        y = _swiglu(h_ref, act_ref, wg_ref, wu_ref, wd_ref, **mixing)
        o_ref[...] = _ffn2_out(x2_ref[1 - slot], y, _mod_getter(mod_ref, ffn_tile // tiles_per_seq), fgain_ref, final)

    @pl.when(s == 0)
    def _():
        _pool_sums(u_ref, up_ref, uext_ref, t1_ref, t2_ref, pooled_ref, 0)
        for step in mix_steps[:-1]:
            step()
        finish_mix()
        norm_and_pool_ahead()

    @pl.when(jnp.logical_and(s > 0, s < n_tiles))
    def _():
        ffn2(side_work=mix_steps[:-1] + [finish_mix], before_down=norm_and_pool_ahead)

    @pl.when(s == n_tiles)
    def _():
        ffn2()


def _kb_call(sinks, x1, q, k, v, u, mod_all, mod_spec, gain, bias, poolw, pscale, wout, wg, wu, wd, fgain,
             layer, final, name):
    g, r, _ = x1.shape
    tm = TM_PROMPT
    tps = r // tm
    n_tiles = g * tps
    nb = tm // WINDOW
    nh = tm // HIST
    mix = lambda s: jnp.minimum(s, n_tiles - 1)
    tail = lambda s: jnp.maximum(s - 1, 0)
    mix_row = lambda w: pl.BlockSpec((None, tm, w), lambda s: (mix(s) // tps, mix(s) % tps, 0))
    tail_row = lambda w: pl.BlockSpec((None, tm, w), lambda s: (tail(s) // tps, tail(s) % tps, 0))
    prev_kv = pl.BlockSpec((None, WINDOW, KV_WIDTH),
                           lambda s: (mix(s) // tps, jnp.maximum((mix(s) % tps) * nb - 1, 0), 0))
    nxt = lambda s: jnp.minimum(s + 1, n_tiles - 1)
    first_u = pl.BlockSpec((None, tm, POOL_WIDTH), lambda s: (0, 0, 0))
    first_hist = pl.BlockSpec((None, HIST, POOL_WIDTH), lambda s: (0, 0, 0))
    next_u = pl.BlockSpec((None, tm, POOL_WIDTH), lambda s: (nxt(s) // tps, nxt(s) % tps, 0))
    next_hist = pl.BlockSpec((None, HIST, POOL_WIDTH),
                             lambda s: (nxt(s) // tps, jnp.maximum((nxt(s) % tps) * nh - 1, 0), 0))
    return pl.pallas_call(
        functools.partial(_kb_kernel, layer=layer, final=final, tiles_per_seq=tps),
        out_shape=jax.ShapeDtypeStruct((g, r, D_MODEL), F32),
        grid=(n_tiles + 1,),
        in_specs=[_SMEM, mix_row(D_MODEL), mix_row(ATTN_WIDTH), mix_row(KV_WIDTH), mix_row(KV_WIDTH),
                  prev_kv, prev_kv, first_u, first_hist, next_u, next_hist, mod_spec, _full((DEPTH, 3, D_MODEL)),
                  _full((N_HEADS * WINDOW, 2 * WINDOW)),
                  _layer_resident((len(POOL_SIZES), POOL_GROUP_WIDTH, POOL_GROUP_WIDTH), layer),
                  _full((DEPTH, POOL_WIDTH)),
                  _layer_resident((D_MODEL, D_MODEL), layer), _layer_resident((D_MODEL, D_FF), layer),
                  _layer_resident((D_MODEL, D_FF), layer), _layer_resident((D_FF, D_MODEL), layer),
                  _full((1, D_MODEL))],
        out_specs=tail_row(D_MODEL),
        scratch_shapes=[pltpu.VMEM((tm, D_MODEL), BF16), pltpu.VMEM((tm, D_MODEL), BF16),
                        pltpu.VMEM((2, tm, D_MODEL), F32), pltpu.VMEM((tm, D_FF), BF16),
                        pltpu.VMEM((2 * HIST + tm, POOL_WIDTH), F32), pltpu.VMEM((2 * HIST + tm, POOL_WIDTH), F32),
                        pltpu.VMEM((2 * HIST + tm, POOL_WIDTH - POOL_GROUP_WIDTH), F32),
                        pltpu.VMEM((tm, POOL_WIDTH), BF16),
                        pltpu.VMEM((N_HEADS * WINDOW, 2 * WINDOW), BF16)],
        compiler_params=pltpu.CompilerParams(
            dimension_semantics=("arbitrary",), vmem_limit_bytes=VMEM_LIMIT),
        name=name,
    )(sinks, x1, q, k, v, k, v, u, u, u, u, mod_all, gain, bias, poolw, pscale, wout, wg, wu, wd, fgain)


def _ks_kernel(sinks_ref, q_ref, kn_ref, vn_ref, ck_ref, cv_ref, bias_ref, o_ref, nk_ref, nv_ref,
               qs_ref, knr_ref, vnr_ref, s_ref, p_ref, acc_ref, *, layer):
    bt = q_ref.shape[0]
    for b in range(bt):
        for cache_ref, new_ref, out_ref in ((ck_ref, kn_ref, nk_ref), (cv_ref, vn_ref, nv_ref)):
            out_ref[b, 0:WINDOW - 1, :] = cache_ref[b, 1:WINDOW, :]
            out_ref[b, WINDOW - 1:WINDOW, :] = new_ref[b:b + 1, :]
    rows = bt * N_HEADS
    nt = (((1,), (1,)), ((), ()))
    head_rows = lambda r: pl.ds(r, bt, stride=N_HEADS)
    lo = lax.broadcasted_iota(jnp.int32, (bt, LANES), 1) < HEAD_DIM
    kn = kn_ref[...].astype(BF16).astype(F32)
    vn = vn_ref[...].astype(BF16).astype(F32)
    for c in range(4):
        qc = q_ref[:, c * LANES:(c + 1) * LANES].astype(F32)
        qs_ref[head_rows(2 * c), :] = jnp.where(lo, qc, 0.0)
        qs_ref[head_rows(2 * c + 1), :] = jnp.where(lo, 0.0, qc)
    for r in range(N_HEADS):
        knr_ref[head_rows(r), :] = kn
        vnr_ref[head_rows(r), :] = vn

    for b in range(bt):
        rb = slice(b * N_HEADS, (b + 1) * N_HEADS)
        s_ref[rb, :] = lax.dot_general(qs_ref[rb, :].astype(BF16), ck_ref[b].astype(BF16), nt,
                                       preferred_element_type=F32)

    head = jnp.bitwise_and(lax.broadcasted_iota(jnp.int32, (rows, 1), 0), N_HEADS - 1)
    sink = jnp.zeros((rows, 1), F32)
    for r in range(N_HEADS):
        sink = jnp.where(head == r, sinks_ref[layer, _head_of_row(r)], sink)
    bias = jnp.concatenate([bias_ref[...]] * bt, axis=0)
    s_c = s_ref[...] + bias[:, 0:WINDOW]
    s_n = jnp.sum(qs_ref[...] * knr_ref[...], axis=-1, keepdims=True) + bias[:, WINDOW:WINDOW + 1]
    m = jnp.maximum(jnp.maximum(jnp.max(s_c, axis=-1, keepdims=True), s_n), sink)
    p_c = jnp.exp(s_c - m)
    p_n = jnp.exp(s_n - m)
    denom = jnp.sum(p_c, axis=-1, keepdims=True) + p_n + jnp.exp(sink - m)
    p_ref[...] = p_c / denom

    for b in range(bt):
        rb = slice(b * N_HEADS, (b + 1) * N_HEADS)
        acc_ref[rb, :] = jnp.dot(p_ref[rb, :].astype(BF16), cv_ref[b].astype(BF16), preferred_element_type=F32)
    acc_ref[...] = acc_ref[...] + (p_n / denom).astype(BF16).astype(F32) * vnr_ref[...]

    for c in range(4):
        o_ref[:, c * LANES:(c + 1) * LANES] = jnp.where(
            lo, acc_ref[head_rows(2 * c), :], acc_ref[head_rows(2 * c + 1), :]).astype(BF16)


def _ks_call(sinks, q, kn, vn, cache_k, cache_v, layer, bias_s):
    nb = q.shape[0]
    bt = 32
    rows = bt * N_HEADS
    per_b = lambda w: pl.BlockSpec((bt, w), lambda i: (i, 0))
    cache = pl.BlockSpec((None, bt, WINDOW, KV_WIDTH), lambda i: (layer, i, 0, 0))
    new_cache = pl.BlockSpec((bt, WINDOW, KV_WIDTH), lambda i: (i, 0, 0))
    return pl.pallas_call(
        functools.partial(_ks_kernel, layer=layer),
        out_shape=(jax.ShapeDtypeStruct((nb, ATTN_WIDTH), BF16),
                   jax.ShapeDtypeStruct((nb, WINDOW, KV_WIDTH), F32),
                   jax.ShapeDtypeStruct((nb, WINDOW, KV_WIDTH), F32)),
        grid=(nb // bt,),
        in_specs=[_SMEM, per_b(ATTN_WIDTH), per_b(KV_WIDTH), per_b(KV_WIDTH), cache, cache,
                  _full((N_HEADS, 2 * WINDOW))],
        out_specs=(per_b(ATTN_WIDTH), new_cache, new_cache),
        scratch_shapes=[pltpu.VMEM((rows, KV_WIDTH), F32), pltpu.VMEM((rows, KV_WIDTH), F32),
                        pltpu.VMEM((rows, KV_WIDTH), F32), pltpu.VMEM((rows, WINDOW), F32),
                        pltpu.VMEM((rows, WINDOW), F32), pltpu.VMEM((rows, KV_WIDTH), F32)],
        compiler_params=pltpu.CompilerParams(dimension_semantics=("arbitrary",), vmem_limit_bytes=VMEM_LIMIT),
        name="sample_attention",
    )(sinks, q, kn, vn, cache_k, cache_v, bias_s)


def _kc_kernel(x1_ref, attn_ref, u_ref, hist_ref, mod_ref, gain_ref, poolw_ref, pscale_ref,
               wout_ref, wg_ref, wu_ref, wd_ref, fgain_ref, o_ref, mixed_ref, h_ref, act_ref, *, layer, final):
    mixed_ref[:, 0:ATTN_WIDTH] = attn_ref[...]
    for g, w in enumerate(POOL_SIZES):
        cols = slice(g * POOL_GROUP_WIDTH, (g + 1) * POOL_GROUP_WIDTH)
        ug = u_ref[:, cols]
        acc = ug
        for d in range(1, w):
            r = POOL_STATE - d
            acc = acc + hist_ref[:, r * POOL_WIDTH + g * POOL_GROUP_WIDTH:r * POOL_WIDTH + (g + 1) * POOL_GROUP_WIDTH]
        pooled = (acc / float(w) - ug).astype(BF16)
        pg = jnp.dot(pooled, poolw_ref[g], preferred_element_type=F32) * pscale_ref[layer:layer + 1, cols]
        mixed_ref[:, ATTN_WIDTH + g * POOL_GROUP_WIDTH:ATTN_WIDTH + (g + 1) * POOL_GROUP_WIDTH] = pg.astype(BF16)
    mod = _mod_getter(mod_ref)
    x2, h = _out_projection(x1_ref[...], mixed_ref, wout_ref, mod, gain_ref, layer)
    h_ref[...] = h
    o_ref[...] = _ffn2_out(x2, _swiglu(h_ref, act_ref, wg_ref, wu_ref, wd_ref), mod, fgain_ref, final)


def _kc_call(x1, attn, u, hist, mod_all, mod_spec, gain, poolw, pscale, wout, wg, wu, wd, fgain, layer, final, name):
    r = x1.shape[0]
    return pl.pallas_call(
        functools.partial(_kc_kernel, layer=layer, final=final),
        out_shape=jax.ShapeDtypeStruct((r, D_MODEL), F32),
        grid=(1,),
        in_specs=[_full((r, D_MODEL)), _full((r, ATTN_WIDTH)), _full((r, POOL_WIDTH)),
                  pl.BlockSpec((None, r, POOL_STATE * POOL_WIDTH), lambda *_: (layer, 0, 0)),
                  mod_spec, _full((DEPTH, 3, D_MODEL)),
                  _layer_resident((len(POOL_SIZES), POOL_GROUP_WIDTH, POOL_GROUP_WIDTH), layer),
                  _full((DEPTH, POOL_WIDTH)),
                  _layer_resident((D_MODEL, D_MODEL), layer), _layer_resident((D_MODEL, D_FF), layer),
                  _layer_resident((D_MODEL, D_FF), layer), _layer_resident((D_FF, D_MODEL), layer),
                  _full((1, D_MODEL))],
        out_specs=_full((r, D_MODEL)),
        scratch_shapes=[pltpu.VMEM((r, D_MODEL), BF16), pltpu.VMEM((r, D_MODEL), BF16),
                        pltpu.VMEM((r, D_FF), BF16)],
        compiler_params=pltpu.CompilerParams(dimension_semantics=("arbitrary",), vmem_limit_bytes=VMEM_LIMIT),
        name=name,
    )(x1, attn, u, hist, mod_all, gain, poolw, pscale, wout, wg, wu, wd, fgain)


def kernel(x_prompt, x_sample, c_prompt, c_sample, cache_k, cache_v, state_pool, w_ada, b_ada, norm_gain,
           w_in, w_out, sinks, rel_bias, pool_w, pool_scale, ffn1_wg, ffn1_wu, ffn1_wd, ffn2_wg, ffn2_wu,
           ffn2_wd, final_gain):
    n_p, seq, _ = x_prompt.shape
    n_s = x_sample.shape[0]

    mod_all = _modulation(jnp.concatenate([c_sample, c_prompt], axis=0), w_ada, b_ada)
    bias = _bias_table(rel_bias)
    bias_s = bias.reshape(N_HEADS, WINDOW, 2 * WINDOW)[:, 0, :]
    fgain = final_gain.reshape(1, D_MODEL)

    wq = (w_in[:, :, :ATTN_WIDTH].reshape(DEPTH, D_MODEL, 2, 4, HEAD_DIM).transpose(0, 1, 3, 2, 4)
          .reshape(DEPTH, D_MODEL, ATTN_WIDTH)) * (HEAD_DIM ** -0.5)
    win = jnp.concatenate([wq, w_in[:, :, ATTN_WIDTH:]], axis=2).astype(BF16)
    wo_attn = (w_out[:, :ATTN_WIDTH].reshape(DEPTH, 2, 4, HEAD_DIM, D_MODEL).transpose(0, 2, 1, 3, 4)
               .reshape(DEPTH, ATTN_WIDTH, D_MODEL))
    wout = jnp.concatenate([wo_attn, w_out[:, ATTN_WIDTH:]], axis=1).astype(BF16)
    wg1, wu1, wd1 = ffn1_wg.astype(BF16), ffn1_wu.astype(BF16), ffn1_wd.astype(BF16)
    wg2, wu2, wd2 = ffn2_wg.astype(BF16), ffn2_wu.astype(BF16), ffn2_wd.astype(BF16)
    poolw = pool_w.astype(BF16)

    xp = x_prompt
    xs = x_sample.reshape(n_s, D_MODEL)
    ck = cache_k.reshape(DEPTH, n_s, WINDOW, KV_WIDTH)
    cv = cache_v.reshape(DEPTH, n_s, WINDOW, KV_WIDTH)
    hist = state_pool.reshape(DEPTH, n_s, POOL_STATE * POOL_WIDTH)

    new_kp, new_vp, new_pp, new_ks, new_vs, new_ps = [], [], [], [], [], []
    for l in range(DEPTH):
        final = l == DEPTH - 1
        mod_p = pl.BlockSpec((None, n_p, MOD_WIDTH), lambda *_, l=l: (l, n_s // n_p, 0))
        mod_s = pl.BlockSpec((None, n_s, MOD_WIDTH), lambda *_, l=l: (l, 0, 0))

        x1, q, k, v, u, k_last, v_last, u_last = _ka_prompt_call(
            xp, mod_all, mod_p, norm_gain, wg1, wu1, wd1, win, l, f"prompt_ffn1_inproj_l{l}")
        xp = _kb_call(sinks, x1, q, k, v, u, mod_all, mod_p, norm_gain, bias, poolw, pool_scale, wout,
                      wg2, wu2, wd2, fgain, l, final, f"prompt_mixer_ffn2_l{l}")
        new_kp.append(k_last)
        new_vp.append(v_last)
        new_pp.append(u_last)

        x1s, qs, ks, vs, us = _ka_sample_call(xs, mod_all, mod_s, norm_gain, wg1, wu1, wd1, win, l,
                                              f"sample_ffn1_inproj_l{l}")
        attn_s, ck_next, cv_next = _ks_call(sinks, qs, ks, vs, ck, cv, l, bias_s)
        xs = _kc_call(x1s, attn_s, us, hist, mod_all, mod_s, norm_gain, poolw, pool_scale, wout, wg2, wu2, wd2,
                      fgain, l, final, f"sample_mixer_ffn2_l{l}")
        new_ks.append(ck_next)
        new_vs.append(cv_next)
        new_ps.append(us)

    new_pool_sample = jnp.concatenate(
        [state_pool[:, :, 1:], jnp.stack(new_ps).reshape(DEPTH, n_s, 1, POOL_WIDTH)], axis=2)
    kv_prompt = (DEPTH, n_p, WINDOW, N_KV_HEADS, HEAD_DIM)
    kv_sample = (DEPTH, n_s, WINDOW, N_KV_HEADS, HEAD_DIM)
    return (xp, xs.reshape(n_s, 1, D_MODEL), jnp.stack(new_kp).reshape(kv_prompt), jnp.stack(new_vp).reshape(kv_prompt),
            jnp.stack(new_pp)[:, :, HIST - POOL_STATE:], jnp.stack(new_ks).reshape(kv_sample),
            jnp.stack(new_vs).reshape(kv_sample), new_pool_sample)
```

```python
import functools
import math

import numpy as np
import jax
import jax.numpy as jnp
from jax import lax
from jax.experimental import pallas as pl
from jax.experimental.pallas import tpu as pltpu

D_MODEL = 1024
DEPTH = 2
HEAD_DIM = 64
N_HEADS = 8
N_KV_HEADS = 2
ATTN_WIDTH = N_HEADS * HEAD_DIM
KV_WIDTH = N_KV_HEADS * HEAD_DIM
WINDOW = 128
POOL_SIZES = (2, 4, 8, 16)
POOL_GROUP_WIDTH = 128
POOL_WIDTH = len(POOL_SIZES) * POOL_GROUP_WIDTH
POOL_STATE = max(POOL_SIZES) - 1
IN_WIDTH = ATTN_WIDTH + 2 * KV_WIDTH + POOL_WIDTH
D_FF = 2816
N_BUCKETS = 32
MAX_DISTANCE = 128
N_MOD = 9
MOD_WIDTH = N_MOD * D_MODEL
EPS = 1e-6

LANES = 128
SUBLANES = 8
MXU_COLS = 256
FF_CHUNK = 2 * MXU_COLS
FF_CHUNKS = tuple((c, min(FF_CHUNK, D_FF - c)) for c in range(0, D_FF, FF_CHUNK))
TM_PROMPT = 512
HIST = 16
VMEM_LIMIT = 56 * 1024 * 1024

F32 = jnp.float32
BF16 = jnp.bfloat16
NEG_INF = float("-inf")


def _head_of_row(r):
    return (r // 2) + 4 * (r % 2)


def _t5_bucket_np(dist):
    n = np.maximum(dist, 0)
    max_exact = N_BUCKETS // 2
    nf = np.maximum(n, 1).astype(np.float32)
    large = max_exact + (np.log(nf / max_exact) / math.log(MAX_DISTANCE / max_exact)
                         * (N_BUCKETS - max_exact)).astype(np.int32)
    large = np.minimum(large, N_BUCKETS - 1)
    return np.where(n < max_exact, n, large).astype(np.int32)


def _bucket_table():
    dist = np.arange(WINDOW)[:, None] + WINDOW - np.arange(2 * WINDOW)[None, :]
    return _t5_bucket_np(dist)


def _full(shape):
    return pl.BlockSpec(shape, lambda *_: (0,) * len(shape))


def _layer_resident(shape, layer):
    return pl.BlockSpec((None,) + tuple(shape), lambda *_: (layer,) + (0,) * len(shape),
                        pipeline_mode=pl.Buffered(1))


_SMEM = pl.BlockSpec(memory_space=pltpu.SMEM)


def _rms_mod(x, gain, shift, scale):
    y = x * lax.rsqrt(jnp.mean(x * x, axis=-1, keepdims=True) + EPS)
    return (y * gain) * (1.0 + scale) + shift


def _dot_by_row_halves(a_ref, w):
    half = a_ref.shape[0] // 2
    if half % (2 * SUBLANES) or half < MXU_COLS:
        return jnp.dot(a_ref[...], w, preferred_element_type=F32)
    return jnp.concatenate([jnp.dot(a_ref[:half, :], w, preferred_element_type=F32),
                            jnp.dot(a_ref[half:, :], w, preferred_element_type=F32)], axis=0)


def _swiglu(h_ref, act_ref, wg_ref, wu_ref, wd_ref, side_work=(), before_down=None):
    n = len(FF_CHUNKS)
    assert len(side_work) <= n
    after_chunk = {((j + 1) * n) // (len(side_work) + 1) - 1: work for j, work in enumerate(side_work)}
    chunk_dot = ((lambda a_ref, w: jnp.dot(a_ref[...], w, preferred_element_type=F32)) if side_work
                 else _dot_by_row_halves)
    for i, (c0, cw) in enumerate(FF_CHUNKS):
        g = chunk_dot(h_ref, wg_ref[:, c0:c0 + cw])
        u = chunk_dot(h_ref, wu_ref[:, c0:c0 + cw])
        act_ref[:, c0:c0 + cw] = ((g / (1.0 + jnp.exp(-g))) * u).astype(BF16)
        if i in after_chunk:
            after_chunk[i]()
    if before_down is not None:
        before_down()
    return _dot_by_row_halves(act_ref, wd_ref[...])


def _mod_getter(mod_ref, row=None):
    rows = slice(None) if row is None else pl.ds(row, 1)
    return lambda k: mod_ref[rows, k * D_MODEL:(k + 1) * D_MODEL]


def _mod_kernel(c_ref, w_ref, b_ref, o_ref):
    c = c_ref[...]
    a = (c / (1.0 + jnp.exp(-c))).astype(BF16)
    o_ref[...] = jnp.dot(a, w_ref[...].astype(BF16), preferred_element_type=F32) + b_ref[...]


def _modulation(c_all, w_ada, b_ada):
    rows = c_all.shape[0]
    n_chunk = 3 * D_MODEL
    return pl.pallas_call(
        _mod_kernel,
        out_shape=jax.ShapeDtypeStruct((DEPTH, rows, MOD_WIDTH), F32),
        grid=(DEPTH, MOD_WIDTH // n_chunk),
        in_specs=[
            pl.BlockSpec((rows, D_MODEL), lambda l, j: (0, 0)),
            pl.BlockSpec((None, D_MODEL, n_chunk), lambda l, j: (l, 0, j)),
            pl.BlockSpec((None, 1, n_chunk), lambda l, j: (l, 0, j)),
        ],
        out_specs=pl.BlockSpec((None, rows, n_chunk), lambda l, j: (l, 0, j)),
        compiler_params=pltpu.CompilerParams(
            dimension_semantics=("arbitrary", "arbitrary"), vmem_limit_bytes=VMEM_LIMIT),
        name="adaln_modulation",
    )(c_all, w_ada, b_ada.reshape(DEPTH, 1, MOD_WIDTH))


def _bias_kernel(rb_ref, bucket_ref, o_ref):
    bucket = bucket_ref[...]
    eq = [bucket == b for b in range(N_BUCKETS)]
    for r in range(N_HEADS):
        acc = jnp.zeros((WINDOW, 2 * WINDOW), F32)
        for b in range(N_BUCKETS):
            acc = jnp.where(eq[b], rb_ref[b, _head_of_row(r)], acc)
        o_ref[r * WINDOW:(r + 1) * WINDOW, :] = acc


def _bias_table(rel_bias):
    return pl.pallas_call(
        _bias_kernel,
        out_shape=jax.ShapeDtypeStruct((N_HEADS * WINDOW, 2 * WINDOW), F32),
        in_specs=[_SMEM, pl.BlockSpec(memory_space=pltpu.VMEM)],
        out_specs=pl.BlockSpec(memory_space=pltpu.VMEM),
        name="rel_bias_table",
    )(rel_bias, jnp.asarray(_bucket_table()))


def _project_q(hm_ref, win_ref, q_ref):
    q_ref[...] = jnp.dot(hm_ref[...], win_ref[:, :ATTN_WIDTH], preferred_element_type=F32).astype(BF16)


def _project_kvu(hm_ref, win_ref, k_ref, v_ref, u_ref):
    z = jnp.dot(hm_ref[...], win_ref[:, ATTN_WIDTH:], preferred_element_type=F32)
    k_ref[...] = z[:, :KV_WIDTH]
    v_ref[...] = z[:, KV_WIDTH:2 * KV_WIDTH]
    u_ref[...] = z[:, 2 * KV_WIDTH:]
    return z


def _ffn1(x_ref, mod, gain_ref, wg_ref, wu_ref, wd_ref, x1_ref, h_ref, act_ref, layer, before_norm=None):
    x = x_ref[...]
    h_ref[...] = _rms_mod(x, gain_ref[layer, 0:1, :], mod(0), mod(1)).astype(BF16)
    x1 = x + 0.5 * mod(2) * _swiglu(h_ref, act_ref, wg_ref, wu_ref, wd_ref)
    x1_ref[...] = x1
    if before_norm is not None:
        before_norm()
    return _rms_mod(x1, gain_ref[layer, 1:2, :], mod(3), mod(4)).astype(BF16)


def _ka_sample_kernel(x_ref, mod_ref, gain_ref, wg_ref, wu_ref, wd_ref, win_ref,
                      x1_ref, q_ref, k_ref, v_ref, u_ref, h_ref, act_ref, *, layer):
    h_ref[...] = _ffn1(x_ref, _mod_getter(mod_ref), gain_ref, wg_ref, wu_ref, wd_ref, x1_ref, h_ref, act_ref, layer)
    _project_q(h_ref, win_ref, q_ref)
    _project_kvu(h_ref, win_ref, k_ref, v_ref, u_ref)


def _ka_prompt_kernel(x_ref, mod_ref, gain_ref, wg_ref, wu_ref, wd_ref, win_ref,
                      x1_ref, q_ref, k_ref, v_ref, u_ref, kl_ref, vl_ref, ul_ref,
                      h_ref, act_ref, hm_ref, *, layer, tiles_per_seq):
    tm = x_ref.shape[0]
    s = pl.program_id(0)
    n_tiles = pl.num_programs(0) - 1

    @pl.when(s == 0)
    def _():
        hm_ref[...] = jnp.zeros_like(hm_ref)

    def project_rest():
        z = _project_kvu(hm_ref, win_ref, k_ref, v_ref, u_ref)
        kl_ref[...] = z[tm - WINDOW:, :KV_WIDTH]
        vl_ref[...] = z[tm - WINDOW:, KV_WIDTH:2 * KV_WIDTH]
        ul_ref[...] = z[tm - HIST:, 2 * KV_WIDTH:]

    @pl.when(s < n_tiles)
    def _():
        _project_q(hm_ref, win_ref, q_ref)
        mod = _mod_getter(mod_ref, s // tiles_per_seq)
        hm = _ffn1(x_ref, mod, gain_ref, wg_ref, wu_ref, wd_ref, x1_ref, h_ref, act_ref, layer,
                   before_norm=project_rest)
        hm_ref[...] = hm

    @pl.when(s == n_tiles)
    def _():
        _project_q(hm_ref, win_ref, q_ref)
        project_rest()


def _ka_out_shapes(g, r):
    return (jax.ShapeDtypeStruct((g, r, D_MODEL), F32),
            jax.ShapeDtypeStruct((g, r, ATTN_WIDTH), BF16),
            jax.ShapeDtypeStruct((g, r, KV_WIDTH), F32),
            jax.ShapeDtypeStruct((g, r, KV_WIDTH), F32),
            jax.ShapeDtypeStruct((g, r, POOL_WIDTH), F32))


def _ka_weight_specs(layer):
    return [_full((DEPTH, 3, D_MODEL)),
            _layer_resident((D_MODEL, D_FF), layer), _layer_resident((D_MODEL, D_FF), layer),
            _layer_resident((D_FF, D_MODEL), layer), _layer_resident((D_MODEL, IN_WIDTH), layer)]


def _ka_sample_call(x, mod_all, mod_spec, gain, wg, wu, wd, win, layer, name):
    r = x.shape[0]
    row = lambda w: _full((r, w))
    return pl.pallas_call(
        functools.partial(_ka_sample_kernel, layer=layer),
        out_shape=tuple(jax.ShapeDtypeStruct(s.shape[1:], s.dtype) for s in _ka_out_shapes(1, r)),
        grid=(1,),
        in_specs=[row(D_MODEL), mod_spec] + _ka_weight_specs(layer),
        out_specs=(row(D_MODEL), row(ATTN_WIDTH), row(KV_WIDTH), row(KV_WIDTH), row(POOL_WIDTH)),
        scratch_shapes=[pltpu.VMEM((r, D_MODEL), BF16), pltpu.VMEM((r, D_FF), BF16)],
        compiler_params=pltpu.CompilerParams(dimension_semantics=("arbitrary",), vmem_limit_bytes=VMEM_LIMIT),
        name=name,
    )(x, mod_all, gain, wg, wu, wd, win)


def _ka_prompt_call(x, mod_all, mod_spec, gain, wg, wu, wd, win, layer, name):
    g, r, _ = x.shape
    tm = TM_PROMPT
    tps = r // tm
    n_tiles = g * tps
    main = lambda s: jnp.minimum(s, n_tiles - 1)
    prev = lambda s: jnp.maximum(s - 1, 0)
    main_row = lambda w: pl.BlockSpec((None, tm, w), lambda s: (main(s) // tps, main(s) % tps, 0))
    prev_row = lambda w: pl.BlockSpec((None, tm, w), lambda s: (prev(s) // tps, prev(s) % tps, 0))
    last = lambda rows, w: pl.BlockSpec((None, rows, w), lambda s: (prev(s) // tps, 0, 0))
    return pl.pallas_call(
        functools.partial(_ka_prompt_kernel, layer=layer, tiles_per_seq=tps),
        out_shape=_ka_out_shapes(g, r) + (jax.ShapeDtypeStruct((g, WINDOW, KV_WIDTH), F32),
                                          jax.ShapeDtypeStruct((g, WINDOW, KV_WIDTH), F32),
                                          jax.ShapeDtypeStruct((g, HIST, POOL_WIDTH), F32)),
        grid=(n_tiles + 1,),
        in_specs=[main_row(D_MODEL), mod_spec] + _ka_weight_specs(layer),
        out_specs=(main_row(D_MODEL), prev_row(ATTN_WIDTH), prev_row(KV_WIDTH), prev_row(KV_WIDTH),
                   prev_row(POOL_WIDTH), last(WINDOW, KV_WIDTH), last(WINDOW, KV_WIDTH), last(HIST, POOL_WIDTH)),
        scratch_shapes=[pltpu.VMEM((tm, D_MODEL), BF16), pltpu.VMEM((tm, D_FF), BF16),
                        pltpu.VMEM((tm, D_MODEL), BF16)],
        compiler_params=pltpu.CompilerParams(dimension_semantics=("arbitrary",), vmem_limit_bytes=VMEM_LIMIT),
        name=name,
    )(x, mod_all, gain, wg, wu, wd, win)


def _low_half():
    return lax.broadcasted_iota(jnp.int32, (WINDOW, LANES), 1) < HEAD_DIM


def _attn_probs(qb, kk, bias_ref, sinks_ref, layer, valid, p_ref):
    lo = _low_half()
    zero = jnp.zeros((WINDOW, LANES), BF16)
    pieces = []
    for c in range(4):
        qc = qb[:, c * LANES:(c + 1) * LANES]
        pieces += [jnp.where(lo, qc, zero), jnp.where(lo, zero, qc)]
    qs = jnp.concatenate(pieces, axis=0)
    s = lax.dot_general(qs, kk, (((1,), (1,)), ((), ())), preferred_element_type=F32)
    for r in range(N_HEADS):
        rows = slice(r * WINDOW, (r + 1) * WINDOW)
        sr = jnp.where(valid, s[rows] + bias_ref[rows, :], NEG_INF)
        sink = sinks_ref[layer, _head_of_row(r)]
        m = jnp.maximum(jnp.max(sr, axis=-1, keepdims=True), sink)
        p = jnp.exp(sr - m)
        denom = jnp.sum(p, axis=-1, keepdims=True) + jnp.exp(sink - m)
        p_ref[rows, :] = (p / denom).astype(BF16)


def _attn_values(p_ref, vv):
    lo = _low_half()
    o = jnp.dot(p_ref[...], vv, preferred_element_type=F32)
    cols = [jnp.where(lo, o[(2 * c) * WINDOW:(2 * c + 1) * WINDOW], o[(2 * c + 1) * WINDOW:(2 * c + 2) * WINDOW])
            for c in range(4)]
    return jnp.concatenate(cols, axis=1)


def _pool_sums(u_ref, up_ref, uext_ref, t1_ref, t2_ref, pooled_ref, seq_tile):
    tm = u_ref.shape[0]
    g_w = POOL_GROUP_WIDTH
    top = 2 * HIST
    end = top + tm
    assert POOL_SIZES == (2, 4, 8, 16) and HIST == 2 * SUBLANES
    uext_ref[0:HIST, :] = jnp.zeros((HIST, POOL_WIDTH), F32)
    uext_ref[HIST:top, :] = jnp.where(seq_tile == 0, 0.0, up_ref[...])
    uext_ref[top:, :] = u_ref[...]
    t1_ref[8:end, :] = uext_ref[8:end, :] + uext_ref[7:end - 1, :]
    t2_ref[16:end, :] = t1_ref[16:end, g_w:] + t1_ref[14:end - 2, g_w:]
    t1_ref[24:end, 2 * g_w:] = t2_ref[24:end, g_w:] + t2_ref[20:end - 4, g_w:]
    sums = [t1_ref[top:, 0:g_w], t2_ref[top:, 0:g_w], t1_ref[top:, 2 * g_w:3 * g_w],
            t1_ref[top:, 3 * g_w:] + t1_ref[top - 8:end - 8, 3 * g_w:]]
    pos = seq_tile * tm + lax.broadcasted_iota(jnp.int32, (tm, g_w), 0)
    for g, w in enumerate(POOL_SIZES):
        cols = slice(g * g_w, (g + 1) * g_w)
        count = jnp.minimum(pos + 1, w).astype(F32)
        pooled_ref[:, cols] = (sums[g] / count - uext_ref[top:, cols]).astype(BF16)


def _mix_prompt_steps(q_ref, k_ref, v_ref, kp_ref, vp_ref, bias_ref, sinks_ref, poolw_ref,
                      pscale_ref, mixed_ref, pooled_ref, p_ref, layer, seq_tile):
    tm = q_ref.shape[0]
    nb = tm // WINDOW
    first = seq_tile == 0

    def keys_values(ref, prev_ref, j):
        prev = prev_ref[...] if j == 0 else ref[(j - 1) * WINDOW:j * WINDOW, :]
        return jnp.concatenate([prev, ref[j * WINDOW:(j + 1) * WINDOW, :]], axis=0).astype(BF16)

    def probs(j):
        qi = lax.broadcasted_iota(jnp.int32, (WINDOW, 2 * WINDOW), 0)
        kj = lax.broadcasted_iota(jnp.int32, (WINDOW, 2 * WINDOW), 1)
        dist = qi + WINDOW - kj
        valid = (dist >= 0) & (dist <= WINDOW)
        if j == 0:
            valid = valid & (kj >= jnp.where(first, WINDOW, 0))
        _attn_probs(q_ref[j * WINDOW:(j + 1) * WINDOW, :], keys_values(k_ref, kp_ref, j), bias_ref, sinks_ref,
                    layer, valid, p_ref)

    def values(j):
        a = _attn_values(p_ref, keys_values(v_ref, vp_ref, j))
        mixed_ref[j * WINDOW:(j + 1) * WINDOW, 0:ATTN_WIDTH] = a.astype(BF16)

    def pool_map(g):
        cols = slice(g * POOL_GROUP_WIDTH, (g + 1) * POOL_GROUP_WIDTH)
        pg = jnp.dot(pooled_ref[:, cols], poolw_ref[g], preferred_element_type=F32) * pscale_ref[layer:layer + 1, cols]
        mixed_ref[:, ATTN_WIDTH + g * POOL_GROUP_WIDTH:ATTN_WIDTH + (g + 1) * POOL_GROUP_WIDTH] = pg.astype(BF16)

    assert nb == len(POOL_SIZES)

    def stage(i):
        def run():
            if i > 0:
                values(i - 1)
                pool_map(i - 1)
            if i < nb:
                probs(i)
        return run

    return [stage(i) for i in range(nb + 1)]


def _out_projection(x1, mixed_ref, wout_ref, mod, gain_ref, layer):
    x2 = x1 + mod(5) * _dot_by_row_halves(mixed_ref, wout_ref[...])
    return x2, _rms_mod(x2, gain_ref[layer, 2:3, :], mod(6), mod(7)).astype(BF16)


def _ffn2_out(x2, y, mod, fgain_ref, final):
    x3 = x2 + 0.5 * mod(8) * y
    if final:
        x3 = x3 * lax.rsqrt(jnp.mean(x3 * x3, axis=-1, keepdims=True) + EPS) * fgain_ref[...]
    return x3


def _kb_kernel(sinks_ref, x1_ref, q_ref, k_ref, v_ref, kp_ref, vp_ref, u_ref, up_ref, un_ref, upn_ref,
               mod_ref, gain_ref, bias_ref, poolw_ref, pscale_ref, wout_ref, wg_ref, wu_ref, wd_ref, fgain_ref,
               o_ref, mixed_ref, h_ref, x2_ref, act_ref, uext_ref, t1_ref, t2_ref, pooled_ref, p_ref,
               *, layer, final, tiles_per_seq):
    s = pl.program_id(0)
    n_tiles = pl.num_programs(0) - 1
    mix_tile = jnp.minimum(s, n_tiles - 1)
    next_tile = jnp.minimum(s + 1, n_tiles - 1)
    ffn_tile = jnp.maximum(s - 1, 0)
    slot = s % 2

    mix_steps = _mix_prompt_steps(q_ref, k_ref, v_ref, kp_ref, vp_ref, bias_ref, sinks_ref, poolw_ref, pscale_ref,
                                  mixed_ref, pooled_ref, p_ref, layer, mix_tile % tiles_per_seq)

    mix_mod = _mod_getter(mod_ref, mix_tile // tiles_per_seq)
    last_mix_step = mix_steps[-1]

    def finish_mix():
        last_mix_step()
        x2_ref[slot] = x1_ref[...] + mix_mod(5) * _dot_by_row_halves(mixed_ref, wout_ref[...])

    def norm_and_pool_ahead():
        h_ref[...] = _rms_mod(x2_ref[slot], gain_ref[layer, 2:3, :], mix_mod(6), mix_mod(7)).astype(BF16)
        _pool_sums(un_ref, upn_ref, uext_ref, t1_ref, t2_ref, pooled_ref, next_tile % tiles_per_seq)

    def ffn2(**mixing):
        y = _swiglu(h_ref, act_ref, wg_ref, wu_ref, wd_ref, **mixing)
        o_ref[...] = _ffn2_out(x2_ref[1 - slot], y, _mod_getter(mod_ref, ffn_tile // tiles_per_seq), fgain_ref, final)

    @pl.when(s == 0)
    def _():
        _pool_sums(u_ref, up_ref, uext_ref, t1_ref, t2_ref, pooled_ref, 0)
        for step in mix_steps[:-1]:
            step()
        finish_mix()
        norm_and_pool_ahead()

    @pl.when(jnp.logical_and(s > 0, s < n_tiles))
    def _():
        ffn2(side_work=mix_steps[:-1] + [finish_mix], before_down=norm_and_pool_ahead)

    @pl.when(s == n_tiles)
    def _():
        ffn2()


def _kb_call(sinks, x1, q, k, v, u, mod_all, mod_spec, gain, bias, poolw, pscale, wout, wg, wu, wd, fgain,
             layer, final, name):
    g, r, _ = x1.shape
    tm = TM_PROMPT
    tps = r // tm
    n_tiles = g * tps
    nb = tm // WINDOW
    nh = tm // HIST
    mix = lambda s: jnp.minimum(s, n_tiles - 1)
    tail = lambda s: jnp.maximum(s - 1, 0)
    mix_row = lambda w: pl.BlockSpec((None, tm, w), lambda s: (mix(s) // tps, mix(s) % tps, 0))
    tail_row = lambda w: pl.BlockSpec((None, tm, w), lambda s: (tail(s) // tps, tail(s) % tps, 0))
    prev_kv = pl.BlockSpec((None, WINDOW, KV_WIDTH),
                           lambda s: (mix(s) // tps, jnp.maximum((mix(s) % tps) * nb - 1, 0), 0))
    nxt = lambda s: jnp.minimum(s + 1, n_tiles - 1)
    first_u = pl.BlockSpec((None, tm, POOL_WIDTH), lambda s: (0, 0, 0))
    first_hist = pl.BlockSpec((None, HIST, POOL_WIDTH), lambda s: (0, 0, 0))
    next_u = pl.BlockSpec((None, tm, POOL_WIDTH), lambda s: (nxt(s) // tps, nxt(s) % tps, 0))
    next_hist = pl.BlockSpec((None, HIST, POOL_WIDTH),
                             lambda s: (nxt(s) // tps, jnp.maximum((nxt(s) % tps) * nh - 1, 0), 0))
    return pl.pallas_call(
        functools.partial(_kb_kernel, layer=layer, final=final, tiles_per_seq=tps),
        out_shape=jax.ShapeDtypeStruct((g, r, D_MODEL), F32),
        grid=(n_tiles + 1,),
        in_specs=[_SMEM, mix_row(D_MODEL), mix_row(ATTN_WIDTH), mix_row(KV_WIDTH), mix_row(KV_WIDTH),
                  prev_kv, prev_kv, first_u, first_hist, next_u, next_hist, mod_spec, _full((DEPTH, 3, D_MODEL)),
                  _full((N_HEADS * WINDOW, 2 * WINDOW)),
                  _layer_resident((len(POOL_SIZES), POOL_GROUP_WIDTH, POOL_GROUP_WIDTH), layer),
                  _full((DEPTH, POOL_WIDTH)),
                  _layer_resident((D_MODEL, D_MODEL), layer), _layer_resident((D_MODEL, D_FF), layer),
                  _layer_resident((D_MODEL, D_FF), layer), _layer_resident((D_FF, D_MODEL), layer),
                  _full((1, D_MODEL))],
        out_specs=tail_row(D_MODEL),
        scratch_shapes=[pltpu.VMEM((tm, D_MODEL), BF16), pltpu.VMEM((tm, D_MODEL), BF16),
                        pltpu.VMEM((2, tm, D_MODEL), F32), pltpu.VMEM((tm, D_FF), BF16),
                        pltpu.VMEM((2 * HIST + tm, POOL_WIDTH), F32), pltpu.VMEM((2 * HIST + tm, POOL_WIDTH), F32),
                        pltpu.VMEM((2 * HIST + tm, POOL_WIDTH - POOL_GROUP_WIDTH), F32),
                        pltpu.VMEM((tm, POOL_WIDTH), BF16),
                        pltpu.VMEM((N_HEADS * WINDOW, 2 * WINDOW), BF16)],
        compiler_params=pltpu.CompilerParams(
            dimension_semantics=("arbitrary",), vmem_limit_bytes=VMEM_LIMIT),
        name=name,
    )(sinks, x1, q, k, v, k, v, u, u, u, u, mod_all, gain, bias, poolw, pscale, wout, wg, wu, wd, fgain)


def _ks_kernel(sinks_ref, q_ref, kn_ref, vn_ref, ck_ref, cv_ref, bias_ref, o_ref, nk_ref, nv_ref,
               qs_ref, knr_ref, vnr_ref, s_ref, p_ref, acc_ref, *, layer):
    bt = q_ref.shape[0]
    for b in range(bt):
        for cache_ref, new_ref, out_ref in ((ck_ref, kn_ref, nk_ref), (cv_ref, vn_ref, nv_ref)):
            out_ref[b, 0:WINDOW - 1, :] = cache_ref[b, 1:WINDOW, :]
            out_ref[b, WINDOW - 1:WINDOW, :] = new_ref[b:b + 1, :]
    rows = bt * N_HEADS
    nt = (((1,), (1,)), ((), ()))
    head_rows = lambda r: pl.ds(r, bt, stride=N_HEADS)
    lo = lax.broadcasted_iota(jnp.int32, (bt, LANES), 1) < HEAD_DIM
    kn = kn_ref[...].astype(BF16).astype(F32)
    vn = vn_ref[...].astype(BF16).astype(F32)
    for c in range(4):
        qc = q_ref[:, c * LANES:(c + 1) * LANES].astype(F32)
        qs_ref[head_rows(2 * c), :] = jnp.where(lo, qc, 0.0)
        qs_ref[head_rows(2 * c + 1), :] = jnp.where(lo, 0.0, qc)
    for r in range(N_HEADS):
        knr_ref[head_rows(r), :] = kn
        vnr_ref[head_rows(r), :] = vn

    for b in range(bt):
        rb = slice(b * N_HEADS, (b + 1) * N_HEADS)
        s_ref[rb, :] = lax.dot_general(qs_ref[rb, :].astype(BF16), ck_ref[b].astype(BF16), nt,
                                       preferred_element_type=F32)

    head = jnp.bitwise_and(lax.broadcasted_iota(jnp.int32, (rows, 1), 0), N_HEADS - 1)
    sink = jnp.zeros((rows, 1), F32)
    for r in range(N_HEADS):
        sink = jnp.where(head == r, sinks_ref[layer, _head_of_row(r)], sink)
    bias = jnp.concatenate([bias_ref[...]] * bt, axis=0)
    s_c = s_ref[...] + bias[:, 0:WINDOW]
    s_n = jnp.sum(qs_ref[...] * knr_ref[...], axis=-1, keepdims=True) + bias[:, WINDOW:WINDOW + 1]
    m = jnp.maximum(jnp.maximum(jnp.max(s_c, axis=-1, keepdims=True), s_n), sink)
    p_c = jnp.exp(s_c - m)
    p_n = jnp.exp(s_n - m)
    denom = jnp.sum(p_c, axis=-1, keepdims=True) + p_n + jnp.exp(sink - m)
    p_ref[...] = p_c / denom

    for b in range(bt):
        rb = slice(b * N_HEADS, (b + 1) * N_HEADS)
        acc_ref[rb, :] = jnp.dot(p_ref[rb, :].astype(BF16), cv_ref[b].astype(BF16), preferred_element_type=F32)
    acc_ref[...] = acc_ref[...] + (p_n / denom).astype(BF16).astype(F32) * vnr_ref[...]

    for c in range(4):
        o_ref[:, c * LANES:(c + 1) * LANES] = jnp.where(
            lo, acc_ref[head_rows(2 * c), :], acc_ref[head_rows(2 * c + 1), :]).astype(BF16)


def _ks_call(sinks, q, kn, vn, cache_k, cache_v, layer, bias_s):
    nb = q.shape[0]
    bt = 32
    rows = bt * N_HEADS
    per_b = lambda w: pl.BlockSpec((bt, w), lambda i: (i, 0))
    cache = pl.BlockSpec((None, bt, WINDOW, KV_WIDTH), lambda i: (layer, i, 0, 0))
    new_cache = pl.BlockSpec((bt, WINDOW, KV_WIDTH), lambda i: (i, 0, 0))
    return pl.pallas_call(
        functools.partial(_ks_kernel, layer=layer),
        out_shape=(jax.ShapeDtypeStruct((nb, ATTN_WIDTH), BF16),
                   jax.ShapeDtypeStruct((nb, WINDOW, KV_WIDTH), F32),
                   jax.ShapeDtypeStruct((nb, WINDOW, KV_WIDTH), F32)),
        grid=(nb // bt,),
        in_specs=[_SMEM, per_b(ATTN_WIDTH), per_b(KV_WIDTH), per_b(KV_WIDTH), cache, cache,
                  _full((N_HEADS, 2 * WINDOW))],
        out_specs=(per_b(ATTN_WIDTH), new_cache, new_cache),
        scratch_shapes=[pltpu.VMEM((rows, KV_WIDTH), F32), pltpu.VMEM((rows, KV_WIDTH), F32),
                        pltpu.VMEM((rows, KV_WIDTH), F32), pltpu.VMEM((rows, WINDOW), F32),
                        pltpu.VMEM((rows, WINDOW), F32), pltpu.VMEM((rows, KV_WIDTH), F32)],
        compiler_params=pltpu.CompilerParams(dimension_semantics=("arbitrary",), vmem_limit_bytes=VMEM_LIMIT),
        name="sample_attention",
    )(sinks, q, kn, vn, cache_k, cache_v, bias_s)


def _kc_kernel(x1_ref, attn_ref, u_ref, hist_ref, mod_ref, gain_ref, poolw_ref, pscale_ref,
               wout_ref, wg_ref, wu_ref, wd_ref, fgain_ref, o_ref, newhist_ref, mixed_ref, h_ref, act_ref,
               *, layer, final):
    for r in range(POOL_STATE - 1):
        newhist_ref[:, r, :] = hist_ref[:, r + 1, :]
    newhist_ref[:, POOL_STATE - 1, :] = u_ref[...]
    mixed_ref[:, 0:ATTN_WIDTH] = attn_ref[...]
    for g, w in enumerate(POOL_SIZES):
        cols = slice(g * POOL_GROUP_WIDTH, (g + 1) * POOL_GROUP_WIDTH)
        ug = u_ref[:, cols]
        acc = ug
        for d in range(1, w):
            acc = acc + hist_ref[:, POOL_STATE - d, cols]
        pooled = (acc / float(w) - ug).astype(BF16)
        pg = jnp.dot(pooled, poolw_ref[g], preferred_element_type=F32) * pscale_ref[layer:layer + 1, cols]
        mixed_ref[:, ATTN_WIDTH + g * POOL_GROUP_WIDTH:ATTN_WIDTH + (g + 1) * POOL_GROUP_WIDTH] = pg.astype(BF16)
    mod = _mod_getter(mod_ref)
    x2, h = _out_projection(x1_ref[...], mixed_ref, wout_ref, mod, gain_ref, layer)
    h_ref[...] = h
    o_ref[...] = _ffn2_out(x2, _swiglu(h_ref, act_ref, wg_ref, wu_ref, wd_ref), mod, fgain_ref, final)


def _kc_call(x1, attn, u, hist, mod_all, mod_spec, gain, poolw, pscale, wout, wg, wu, wd, fgain, layer, final, name):
    r = x1.shape[0]
    return pl.pallas_call(
        functools.partial(_kc_kernel, layer=layer, final=final),
        out_shape=(jax.ShapeDtypeStruct((r, D_MODEL), F32),
                   jax.ShapeDtypeStruct((r, POOL_STATE, POOL_WIDTH), F32)),
        grid=(1,),
        in_specs=[_full((r, D_MODEL)), _full((r, ATTN_WIDTH)), _full((r, POOL_WIDTH)),
                  pl.BlockSpec((None, r, POOL_STATE, POOL_WIDTH), lambda *_: (layer, 0, 0, 0)),
                  mod_spec, _full((DEPTH, 3, D_MODEL)),
                  _layer_resident((len(POOL_SIZES), POOL_GROUP_WIDTH, POOL_GROUP_WIDTH), layer),
                  _full((DEPTH, POOL_WIDTH)),
                  _layer_resident((D_MODEL, D_MODEL), layer), _layer_resident((D_MODEL, D_FF), layer),
                  _layer_resident((D_MODEL, D_FF), layer), _layer_resident((D_FF, D_MODEL), layer),
                  _full((1, D_MODEL))],
        out_specs=(_full((r, D_MODEL)), _full((r, POOL_STATE, POOL_WIDTH))),
        scratch_shapes=[pltpu.VMEM((r, D_MODEL), BF16), pltpu.VMEM((r, D_MODEL), BF16),
                        pltpu.VMEM((r, D_FF), BF16)],
        compiler_params=pltpu.CompilerParams(dimension_semantics=("arbitrary",), vmem_limit_bytes=VMEM_LIMIT),
        name=name,
    )(x1, attn, u, hist, mod_all, gain, poolw, pscale, wout, wg, wu, wd, fgain)


def kernel(x_prompt, x_sample, c_prompt, c_sample, cache_k, cache_v, state_pool, w_ada, b_ada, norm_gain,
           w_in, w_out, sinks, rel_bias, pool_w, pool_scale, ffn1_wg, ffn1_wu, ffn1_wd, ffn2_wg, ffn2_wu,
           ffn2_wd, final_gain):
    n_p, seq, _ = x_prompt.shape
    n_s = x_sample.shape[0]

    mod_all = _modulation(jnp.concatenate([c_sample, c_prompt], axis=0), w_ada, b_ada)
    bias = _bias_table(rel_bias)
    bias_s = bias.reshape(N_HEADS, WINDOW, 2 * WINDOW)[:, 0, :]
    fgain = final_gain.reshape(1, D_MODEL)

    wq = (w_in[:, :, :ATTN_WIDTH].reshape(DEPTH, D_MODEL, 2, 4, HEAD_DIM).transpose(0, 1, 3, 2, 4)
          .reshape(DEPTH, D_MODEL, ATTN_WIDTH)) * (HEAD_DIM ** -0.5)
    win = jnp.concatenate([wq, w_in[:, :, ATTN_WIDTH:]], axis=2).astype(BF16)
    wo_attn = (w_out[:, :ATTN_WIDTH].reshape(DEPTH, 2, 4, HEAD_DIM, D_MODEL).transpose(0, 2, 1, 3, 4)
               .reshape(DEPTH, ATTN_WIDTH, D_MODEL))
    wout = jnp.concatenate([wo_attn, w_out[:, ATTN_WIDTH:]], axis=1).astype(BF16)
    wg1, wu1, wd1 = ffn1_wg.astype(BF16), ffn1_wu.astype(BF16), ffn1_wd.astype(BF16)
    wg2, wu2, wd2 = ffn2_wg.astype(BF16), ffn2_wu.astype(BF16), ffn2_wd.astype(BF16)
    poolw = pool_w.astype(BF16)

    xp = x_prompt
    xs = x_sample.reshape(n_s, D_MODEL)
    ck = cache_k.reshape(DEPTH, n_s, WINDOW, KV_WIDTH)
    cv = cache_v.reshape(DEPTH, n_s, WINDOW, KV_WIDTH)

    new_kp, new_vp, new_pp, new_ks, new_vs, new_ps = [], [], [], [], [], []
    for l in range(DEPTH):
        final = l == DEPTH - 1
        mod_p = pl.BlockSpec((None, n_p, MOD_WIDTH), lambda *_, l=l: (l, n_s // n_p, 0))
        mod_s = pl.BlockSpec((None, n_s, MOD_WIDTH), lambda *_, l=l: (l, 0, 0))

        x1, q, k, v, u, k_last, v_last, u_last = _ka_prompt_call(
            xp, mod_all, mod_p, norm_gain, wg1, wu1, wd1, win, l, f"prompt_ffn1_inproj_l{l}")
        xp = _kb_call(sinks, x1, q, k, v, u, mod_all, mod_p, norm_gain, bias, poolw, pool_scale, wout,
                      wg2, wu2, wd2, fgain, l, final, f"prompt_mixer_ffn2_l{l}")
        new_kp.append(k_last)
        new_vp.append(v_last)
        new_pp.append(u_last)

        x1s, qs, ks, vs, us = _ka_sample_call(xs, mod_all, mod_s, norm_gain, wg1, wu1, wd1, win, l,
                                              f"sample_ffn1_inproj_l{l}")
        attn_s, ck_next, cv_next = _ks_call(sinks, qs, ks, vs, ck, cv, l, bias_s)
        xs, pool_next = _kc_call(x1s, attn_s, us, state_pool, mod_all, mod_s, norm_gain, poolw, pool_scale, wout,
                                 wg2, wu2, wd2, fgain, l, final, f"sample_mixer_ffn2_l{l}")
        new_ks.append(ck_next)
        new_vs.append(cv_next)
        new_ps.append(pool_next)

    kv_prompt = (DEPTH, n_p, WINDOW, N_KV_HEADS, HEAD_DIM)
    kv_sample = (DEPTH, n_s, WINDOW, N_KV_HEADS, HEAD_DIM)
    return (xp, xs.reshape(n_s, 1, D_MODEL), jnp.stack(new_kp).reshape(kv_prompt), jnp.stack(new_vp).reshape(kv_prompt),
            jnp.stack(new_pp)[:, :, HIST - POOL_STATE:], jnp.stack(new_ks).reshape(kv_sample),
            jnp.stack(new_vs).reshape(kv_sample), jnp.stack(new_ps))
```

```python
import functools
import math

import numpy as np
import jax
import jax.numpy as jnp
from jax import lax
from jax.experimental import pallas as pl
from jax.experimental.pallas import tpu as pltpu

D_MODEL = 1024
DEPTH = 2
HEAD_DIM = 64
N_HEADS = 8
N_KV_HEADS = 2
ATTN_WIDTH = N_HEADS * HEAD_DIM
KV_WIDTH = N_KV_HEADS * HEAD_DIM
WINDOW = 128
POOL_SIZES = (2, 4, 8, 16)
POOL_GROUP_WIDTH = 128
POOL_WIDTH = len(POOL_SIZES) * POOL_GROUP_WIDTH
POOL_STATE = max(POOL_SIZES) - 1
IN_WIDTH = ATTN_WIDTH + 2 * KV_WIDTH + POOL_WIDTH
D_FF = 2816
N_BUCKETS = 32
MAX_DISTANCE = 128
N_MOD = 9
MOD_WIDTH = N_MOD * D_MODEL
TAIL_MOD_FIRST = 5
EPS = 1e-6

LANES = 128
SUBLANES = 8
MXU_COLS = 256
FF_CHUNK = 2 * MXU_COLS
FF_CHUNKS = tuple((c, min(FF_CHUNK, D_FF - c)) for c in range(0, D_FF, FF_CHUNK))
TM_PROMPT = 512
HIST = 16
VMEM_LIMIT = 56 * 1024 * 1024

F32 = jnp.float32
BF16 = jnp.bfloat16
NEG_INF = float("-inf")


def _head_of_row(r):
    return (r // 2) + 4 * (r % 2)


def _t5_bucket_np(dist):
    n = np.maximum(dist, 0)
    max_exact = N_BUCKETS // 2
    nf = np.maximum(n, 1).astype(np.float32)
    large = max_exact + (np.log(nf / max_exact) / math.log(MAX_DISTANCE / max_exact)
                         * (N_BUCKETS - max_exact)).astype(np.int32)
    large = np.minimum(large, N_BUCKETS - 1)
    return np.where(n < max_exact, n, large).astype(np.int32)


def _bucket_table():
    dist = np.arange(WINDOW)[:, None] + WINDOW - np.arange(2 * WINDOW)[None, :]
    return _t5_bucket_np(dist)


def _full(shape):
    return pl.BlockSpec(shape, lambda *_: (0,) * len(shape))


def _layer_resident(shape, layer):
    return pl.BlockSpec((None,) + tuple(shape), lambda *_: (layer,) + (0,) * len(shape),
                        pipeline_mode=pl.Buffered(1))


_SMEM = pl.BlockSpec(memory_space=pltpu.SMEM)


def _rms_mod(x, gain, shift, scale):
    y = x * lax.rsqrt(jnp.mean(x * x, axis=-1, keepdims=True) + EPS)
    return (y * gain) * (1.0 + scale) + shift


def _dot_by_row_halves(a_ref, w):
    half = a_ref.shape[0] // 2
    if half % (2 * SUBLANES) or half < MXU_COLS:
        return jnp.dot(a_ref[...], w, preferred_element_type=F32)
    return jnp.concatenate([jnp.dot(a_ref[:half, :], w, preferred_element_type=F32),
                            jnp.dot(a_ref[half:, :], w, preferred_element_type=F32)], axis=0)


def _swiglu(h_ref, act_ref, wg_ref, wu_ref, wd_ref, side_work=(), before_down=None):
    n = len(FF_CHUNKS)
    assert len(side_work) <= n
    after_chunk = {((j + 1) * n) // (len(side_work) + 1) - 1: work for j, work in enumerate(side_work)}
    chunk_dot = ((lambda a_ref, w: jnp.dot(a_ref[...], w, preferred_element_type=F32)) if side_work
                 else _dot_by_row_halves)
    for i, (c0, cw) in enumerate(FF_CHUNKS):
        g = chunk_dot(h_ref, wg_ref[:, c0:c0 + cw])
        u = chunk_dot(h_ref, wu_ref[:, c0:c0 + cw])
        act_ref[:, c0:c0 + cw] = ((g / (1.0 + jnp.exp(-g))) * u).astype(BF16)
        if i in after_chunk:
            after_chunk[i]()
    if before_down is not None:
        before_down()
    return _dot_by_row_halves(act_ref, wd_ref[...])


def _mod_getter(mod_ref, row=None, first=0):
    rows = slice(None) if row is None else pl.ds(row, 1)
    return lambda k: mod_ref[rows, (k - first) * D_MODEL:(k - first + 1) * D_MODEL]


def _mod_kernel(c_ref, w_ref, b_ref, o_ref):
    c = c_ref[...]
    a = (c / (1.0 + jnp.exp(-c))).astype(BF16)
    o_ref[...] = jnp.dot(a, w_ref[...].astype(BF16), preferred_element_type=F32) + b_ref[...]


def _modulation(c_all, w_ada, b_ada):
    rows = c_all.shape[0]
    n_chunk = 3 * D_MODEL
    return pl.pallas_call(
        _mod_kernel,
        out_shape=jax.ShapeDtypeStruct((DEPTH, rows, MOD_WIDTH), F32),
        grid=(DEPTH, MOD_WIDTH // n_chunk),
        in_specs=[
            pl.BlockSpec((rows, D_MODEL), lambda l, j: (0, 0)),
            pl.BlockSpec((None, D_MODEL, n_chunk), lambda l, j: (l, 0, j)),
            pl.BlockSpec((None, 1, n_chunk), lambda l, j: (l, 0, j)),
        ],
        out_specs=pl.BlockSpec((None, rows, n_chunk), lambda l, j: (l, 0, j)),
        compiler_params=pltpu.CompilerParams(
            dimension_semantics=("arbitrary", "arbitrary"), vmem_limit_bytes=VMEM_LIMIT),
        name="adaln_modulation",
    )(c_all, w_ada, b_ada.reshape(DEPTH, 1, MOD_WIDTH))


def _bias_kernel(rb_ref, bucket_ref, o_ref):
    bucket = bucket_ref[...]
    eq = [bucket == b for b in range(N_BUCKETS)]
    for r in range(N_HEADS):
        acc = jnp.zeros((WINDOW, 2 * WINDOW), F32)
        for b in range(N_BUCKETS):
            acc = jnp.where(eq[b], rb_ref[b, _head_of_row(r)], acc)
        o_ref[r * WINDOW:(r + 1) * WINDOW, :] = acc


def _bias_table(rel_bias):
    return pl.pallas_call(
        _bias_kernel,
        out_shape=jax.ShapeDtypeStruct((N_HEADS * WINDOW, 2 * WINDOW), F32),
        in_specs=[_SMEM, pl.BlockSpec(memory_space=pltpu.VMEM)],
        out_specs=pl.BlockSpec(memory_space=pltpu.VMEM),
        name="rel_bias_table",
    )(rel_bias, jnp.asarray(_bucket_table()))


def _project_q(hm_ref, win_ref, q_ref):
    q_ref[...] = jnp.dot(hm_ref[...], win_ref[:, :ATTN_WIDTH], preferred_element_type=F32).astype(BF16)


def _project_kvu(hm_ref, win_ref, k_ref, v_ref, u_ref):
    z = jnp.dot(hm_ref[...], win_ref[:, ATTN_WIDTH:], preferred_element_type=F32)
    k_ref[...] = z[:, :KV_WIDTH]
    v_ref[...] = z[:, KV_WIDTH:2 * KV_WIDTH]
    u_ref[...] = z[:, 2 * KV_WIDTH:]
    return z


def _ffn1(x_ref, mod, gain_ref, wg_ref, wu_ref, wd_ref, x1_ref, h_ref, act_ref, layer, before_norm=None):
    x = x_ref[...]
    h_ref[...] = _rms_mod(x, gain_ref[layer, 0:1, :], mod(0), mod(1)).astype(BF16)
    x1 = x + 0.5 * mod(2) * _swiglu(h_ref, act_ref, wg_ref, wu_ref, wd_ref)
    x1_ref[...] = x1
    if before_norm is not None:
        before_norm()
    return _rms_mod(x1, gain_ref[layer, 1:2, :], mod(3), mod(4)).astype(BF16)


def _ka_kernel(x_ref, mod_ref, xs_ref, mods_ref, gain_ref, wg_ref, wu_ref, wd_ref, win_ref,
               x1_ref, q_ref, k_ref, v_ref, u_ref, kl_ref, vl_ref, ul_ref,
               x1s_ref, qs_ref, ks_ref, vs_ref, us_ref,
               h_ref, act_ref, hm_ref, hs_ref, acts_ref, *, layer, tiles_per_seq):
    tm = x_ref.shape[0]
    s = pl.program_id(0)
    n_tiles = pl.num_programs(0) - 2

    @pl.when(s == n_tiles + 1)
    def _():
        hs_ref[...] = _ffn1(xs_ref, _mod_getter(mods_ref), gain_ref, wg_ref, wu_ref, wd_ref, x1s_ref, hs_ref,
                            acts_ref, layer)
        _project_q(hs_ref, win_ref, qs_ref)
        _project_kvu(hs_ref, win_ref, ks_ref, vs_ref, us_ref)

    @pl.when(s == 0)
    def _():
        hm_ref[...] = jnp.zeros_like(hm_ref)

    def project_rest():
        z = _project_kvu(hm_ref, win_ref, k_ref, v_ref, u_ref)
        kl_ref[...] = z[tm - WINDOW:, :KV_WIDTH]
        vl_ref[...] = z[tm - WINDOW:, KV_WIDTH:2 * KV_WIDTH]
        ul_ref[...] = z[tm - HIST:, 2 * KV_WIDTH:]

    @pl.when(s < n_tiles)
    def _():
        _project_q(hm_ref, win_ref, q_ref)
        mod = _mod_getter(mod_ref, s // tiles_per_seq)
        hm = _ffn1(x_ref, mod, gain_ref, wg_ref, wu_ref, wd_ref, x1_ref, h_ref, act_ref, layer,
                   before_norm=project_rest)
        hm_ref[...] = hm

    @pl.when(s == n_tiles)
    def _():
        _project_q(hm_ref, win_ref, q_ref)
        project_rest()


def _ka_out_shapes(g, r):
    return (jax.ShapeDtypeStruct((g, r, D_MODEL), F32),
            jax.ShapeDtypeStruct((g, r, ATTN_WIDTH), BF16),
            jax.ShapeDtypeStruct((g, r, KV_WIDTH), F32),
            jax.ShapeDtypeStruct((g, r, KV_WIDTH), F32),
            jax.ShapeDtypeStruct((g, r, POOL_WIDTH), F32))


def _ka_weight_specs(layer):
    return [_full((DEPTH, 3, D_MODEL)),
            _layer_resident((D_MODEL, D_FF), layer), _layer_resident((D_MODEL, D_FF), layer),
            _layer_resident((D_FF, D_MODEL), layer), _layer_resident((D_MODEL, IN_WIDTH), layer)]


def _ka_call(x, xs, mod_all, mod_spec, mods_spec, gain, wg, wu, wd, win, layer, name):
    g, r, _ = x.shape
    rs = xs.shape[0]
    tm = TM_PROMPT
    tps = r // tm
    n_tiles = g * tps
    main = lambda s: jnp.minimum(s, n_tiles - 1)
    prev = lambda s: jnp.clip(s - 1, 0, n_tiles - 1)
    main_row = lambda w: pl.BlockSpec((None, tm, w), lambda s: (main(s) // tps, main(s) % tps, 0))
    prev_row = lambda w: pl.BlockSpec((None, tm, w), lambda s: (prev(s) // tps, prev(s) % tps, 0))
    last = lambda rows, w: pl.BlockSpec((None, rows, w), lambda s: (prev(s) // tps, 0, 0))
    srow = lambda w: _full((rs, w))
    sample_shapes = tuple(jax.ShapeDtypeStruct(sh.shape[1:], sh.dtype) for sh in _ka_out_shapes(1, rs))
    return pl.pallas_call(
        functools.partial(_ka_kernel, layer=layer, tiles_per_seq=tps),
        out_shape=_ka_out_shapes(g, r) + (jax.ShapeDtypeStruct((g, WINDOW, KV_WIDTH), F32),
                                          jax.ShapeDtypeStruct((g, WINDOW, KV_WIDTH), F32),
                                          jax.ShapeDtypeStruct((g, HIST, POOL_WIDTH), F32)) + sample_shapes,
        grid=(n_tiles + 2,),
        in_specs=[main_row(D_MODEL), mod_spec, srow(D_MODEL), mods_spec] + _ka_weight_specs(layer),
        out_specs=(main_row(D_MODEL), prev_row(ATTN_WIDTH), prev_row(KV_WIDTH), prev_row(KV_WIDTH),
                   prev_row(POOL_WIDTH), last(WINDOW, KV_WIDTH), last(WINDOW, KV_WIDTH), last(HIST, POOL_WIDTH),
                   srow(D_MODEL), srow(ATTN_WIDTH), srow(KV_WIDTH), srow(KV_WIDTH), srow(POOL_WIDTH)),
        scratch_shapes=[pltpu.VMEM((tm, D_MODEL), BF16), pltpu.VMEM((tm, D_FF), BF16),
                        pltpu.VMEM((tm, D_MODEL), BF16),
                        pltpu.VMEM((rs, D_MODEL), BF16), pltpu.VMEM((rs, D_FF), BF16)],
        compiler_params=pltpu.CompilerParams(dimension_semantics=("arbitrary",), vmem_limit_bytes=VMEM_LIMIT),
        name=name,
    )(x, mod_all, xs, mod_all, gain, wg, wu, wd, win)


def _low_half():
    return lax.broadcasted_iota(jnp.int32, (WINDOW, LANES), 1) < HEAD_DIM


def _attn_probs(qb, kk, bias_ref, sinks_ref, layer, valid, p_ref):
    lo = _low_half()
    zero = jnp.zeros((WINDOW, LANES), BF16)
    pieces = []
    for c in range(4):
        qc = qb[:, c * LANES:(c + 1) * LANES]
        pieces += [jnp.where(lo, qc, zero), jnp.where(lo, zero, qc)]
    qs = jnp.concatenate(pieces, axis=0)
    s = lax.dot_general(qs, kk, (((1,), (1,)), ((), ())), preferred_element_type=F32)
    for r in range(N_HEADS):
        rows = slice(r * WINDOW, (r + 1) * WINDOW)
        sr = jnp.where(valid, s[rows] + bias_ref[rows, :], NEG_INF)
        sink = sinks_ref[layer, _head_of_row(r)]
        m = jnp.maximum(jnp.max(sr, axis=-1, keepdims=True), sink)
        p = jnp.exp(sr - m)
        denom = jnp.sum(p, axis=-1, keepdims=True) + jnp.exp(sink - m)
        p_ref[rows, :] = (p / denom).astype(BF16)


def _attn_values(p_ref, vv):
    lo = _low_half()
    o = jnp.dot(p_ref[...], vv, preferred_element_type=F32)
    cols = [jnp.where(lo, o[(2 * c) * WINDOW:(2 * c + 1) * WINDOW], o[(2 * c + 1) * WINDOW:(2 * c + 2) * WINDOW])
            for c in range(4)]
    return jnp.concatenate(cols, axis=1)


def _pool_sums(u_ref, up_ref, uext_ref, t1_ref, t2_ref, pooled_ref, seq_tile):
    tm = u_ref.shape[0]
    g_w = POOL_GROUP_WIDTH
    top = 2 * HIST
    end = top + tm
    assert POOL_SIZES == (2, 4, 8, 16) and HIST == 2 * SUBLANES
    uext_ref[0:HIST, :] = jnp.zeros((HIST, POOL_WIDTH), F32)
    uext_ref[HIST:top, :] = jnp.where(seq_tile == 0, 0.0, up_ref[...])
    uext_ref[top:, :] = u_ref[...]
    t1_ref[8:end, :] = uext_ref[8:end, :] + uext_ref[7:end - 1, :]
    t2_ref[16:end, :] = t1_ref[16:end, g_w:] + t1_ref[14:end - 2, g_w:]
    t1_ref[24:end, 2 * g_w:] = t2_ref[24:end, g_w:] + t2_ref[20:end - 4, g_w:]
    sums = [t1_ref[top:, 0:g_w], t2_ref[top:, 0:g_w], t1_ref[top:, 2 * g_w:3 * g_w],
            t1_ref[top:, 3 * g_w:] + t1_ref[top - 8:end - 8, 3 * g_w:]]
    pos = seq_tile * tm + lax.broadcasted_iota(jnp.int32, (tm, g_w), 0)
    for g, w in enumerate(POOL_SIZES):
        cols = slice(g * g_w, (g + 1) * g_w)
        count = jnp.minimum(pos + 1, w).astype(F32)
        pooled_ref[:, cols] = (sums[g] / count - uext_ref[top:, cols]).astype(BF16)


def _mix_prompt_steps(q_ref, k_ref, v_ref, kp_ref, vp_ref, bias_ref, sinks_ref, poolw_ref,
                      pscale_ref, mixed_ref, pooled_ref, p_ref, layer, seq_tile):
    tm = q_ref.shape[0]
    nb = tm // WINDOW
    first = seq_tile == 0

    def keys_values(ref, prev_ref, j):
        prev = prev_ref[...] if j == 0 else ref[(j - 1) * WINDOW:j * WINDOW, :]
        return jnp.concatenate([prev, ref[j * WINDOW:(j + 1) * WINDOW, :]], axis=0).astype(BF16)

    def probs(j):
        qi = lax.broadcasted_iota(jnp.int32, (WINDOW, 2 * WINDOW), 0)
        kj = lax.broadcasted_iota(jnp.int32, (WINDOW, 2 * WINDOW), 1)
        dist = qi + WINDOW - kj
        valid = (dist >= 0) & (dist <= WINDOW)
        if j == 0:
            valid = valid & (kj >= jnp.where(first, WINDOW, 0))
        _attn_probs(q_ref[j * WINDOW:(j + 1) * WINDOW, :], keys_values(k_ref, kp_ref, j), bias_ref, sinks_ref,
                    layer, valid, p_ref)

    def values(j):
        a = _attn_values(p_ref, keys_values(v_ref, vp_ref, j))
        mixed_ref[j * WINDOW:(j + 1) * WINDOW, 0:ATTN_WIDTH] = a.astype(BF16)

    def pool_map(g):
        cols = slice(g * POOL_GROUP_WIDTH, (g + 1) * POOL_GROUP_WIDTH)
        pg = jnp.dot(pooled_ref[:, cols], poolw_ref[g], preferred_element_type=F32) * pscale_ref[layer:layer + 1, cols]
        mixed_ref[:, ATTN_WIDTH + g * POOL_GROUP_WIDTH:ATTN_WIDTH + (g + 1) * POOL_GROUP_WIDTH] = pg.astype(BF16)

    assert nb == len(POOL_SIZES)

    def stage(i):
        def run():
            if i > 0:
                values(i - 1)
                pool_map(i - 1)
            if i < nb:
                probs(i)
        return run

    return [stage(i) for i in range(nb + 1)]


def _out_projection(x1, mixed_ref, wout_ref, mod, gain_ref, layer):
    x2 = x1 + mod(5) * _dot_by_row_halves(mixed_ref, wout_ref[...])
    return x2, _rms_mod(x2, gain_ref[layer, 2:3, :], mod(6), mod(7)).astype(BF16)


def _ffn2_out(x2, y, mod, fgain_ref, final):
    x3 = x2 + 0.5 * mod(8) * y
    if final:
        x3 = x3 * lax.rsqrt(jnp.mean(x3 * x3, axis=-1, keepdims=True) + EPS) * fgain_ref[...]
    return x3


def _kb_kernel(sinks_ref, x1_ref, q_ref, k_ref, v_ref, kp_ref, vp_ref, u_ref, up_ref, un_ref, upn_ref,
               mod_ref, x1s_ref, attns_ref, pooleds_ref, mods_ref,
               gain_ref, bias_ref, poolw_ref, pscale_ref, wout_ref, wg_ref, wu_ref, wd_ref, fgain_ref,
               o_ref, os_ref, mixed_ref, h_ref, x2_ref, act_ref, uext_ref, t1_ref, t2_ref, pooled_ref, p_ref,
               *, layer, final, tiles_per_seq):
    s = pl.program_id(0)
    n_tiles = pl.num_programs(0) - 2

    @pl.when(s == n_tiles + 1)
    def _():
        rs = x1s_ref.shape[0]
        mixed_s, h_s, act_s = (ref.at[pl.ds(0, rs), :] for ref in (mixed_ref, h_ref, act_ref))
        mixed_s[:, 0:ATTN_WIDTH] = attns_ref[...]
        for g in range(len(POOL_SIZES)):
            cols = slice(g * POOL_GROUP_WIDTH, (g + 1) * POOL_GROUP_WIDTH)
            pg = (jnp.dot(pooleds_ref[:, cols], poolw_ref[g], preferred_element_type=F32)
                  * pscale_ref[layer:layer + 1, cols])
            mixed_s[:, ATTN_WIDTH + g * POOL_GROUP_WIDTH:ATTN_WIDTH + (g + 1) * POOL_GROUP_WIDTH] = pg.astype(BF16)
        mod = _mod_getter(mods_ref, first=TAIL_MOD_FIRST)
        x2, h = _out_projection(x1s_ref[...], mixed_s, wout_ref, mod, gain_ref, layer)
        h_s[...] = h
        os_ref[...] = _ffn2_out(x2, _swiglu(h_s, act_s, wg_ref, wu_ref, wd_ref), mod, fgain_ref, final)
    mix_tile = jnp.minimum(s, n_tiles - 1)
    next_tile = jnp.minimum(s + 1, n_tiles - 1)
    ffn_tile = jnp.maximum(s - 1, 0)
    slot = s % 2

    mix_steps = _mix_prompt_steps(q_ref, k_ref, v_ref, kp_ref, vp_ref, bias_ref, sinks_ref, poolw_ref, pscale_ref,
                                  mixed_ref, pooled_ref, p_ref, layer, mix_tile % tiles_per_seq)

    mix_mod = _mod_getter(mod_ref, mix_tile // tiles_per_seq)
    last_mix_step = mix_steps[-1]

    def finish_mix():
        last_mix_step()
        x2_ref[slot] = x1_ref[...] + mix_mod(5) * _dot_by_row_halves(mixed_ref, wout_ref[...])

    def norm_and_pool_ahead():
        h_ref[...] = _rms_mod(x2_ref[slot], gain_ref[layer, 2:3, :], mix_mod(6), mix_mod(7)).astype(BF16)
        _pool_sums(un_ref, upn_ref, uext_ref, t1_ref, t2_ref, pooled_ref, next_tile % tiles_per_seq)

    def ffn2(**mixing):
        y = _swiglu(h_ref, act_ref, wg_ref, wu_ref, wd_ref, **mixing)
        o_ref[...] = _ffn2_out(x2_ref[1 - slot], y, _mod_getter(mod_ref, ffn_tile // tiles_per_seq), fgain_ref, final)

    @pl.when(s == 0)
    def _():
        _pool_sums(u_ref, up_ref, uext_ref, t1_ref, t2_ref, pooled_ref, 0)
        for step in mix_steps[:-1]:
            step()
        finish_mix()
        norm_and_pool_ahead()

    @pl.when(jnp.logical_and(s > 0, s < n_tiles))
    def _():
        ffn2(side_work=mix_steps[:-1] + [finish_mix], before_down=norm_and_pool_ahead)

    @pl.when(s == n_tiles)
    def _():
        ffn2()


def _kb_call(sinks, x1, q, k, v, u, x1s, attn_s, pooled_s, mod_all, mod_spec, mods_tail, gain, bias, poolw, pscale,
             wout, wg, wu, wd, fgain, layer, final, name):
    mods_spec = pl.BlockSpec((None,) + mods_tail.shape[1:], lambda s: (layer, 0, 0), pipeline_mode=pl.Buffered(1))
    g, r, _ = x1.shape
    rs = x1s.shape[0]
    tm = TM_PROMPT
    tps = r // tm
    n_tiles = g * tps
    nb = tm // WINDOW
    nh = tm // HIST
    mix = lambda s: jnp.minimum(s, n_tiles - 1)
    tail = lambda s: jnp.clip(s - 1, 0, n_tiles - 1)
    mix_row = lambda w: pl.BlockSpec((None, tm, w), lambda s: (mix(s) // tps, mix(s) % tps, 0))
    tail_row = lambda w: pl.BlockSpec((None, tm, w), lambda s: (tail(s) // tps, tail(s) % tps, 0))
    prev_kv = pl.BlockSpec((None, WINDOW, KV_WIDTH),
                           lambda s: (mix(s) // tps, jnp.maximum((mix(s) % tps) * nb - 1, 0), 0))
    nxt = lambda s: jnp.minimum(s + 1, n_tiles - 1)
    first_u = pl.BlockSpec((None, tm, POOL_WIDTH), lambda s: (0, 0, 0))
    first_hist = pl.BlockSpec((None, HIST, POOL_WIDTH), lambda s: (0, 0, 0))
    next_u = pl.BlockSpec((None, tm, POOL_WIDTH), lambda s: (nxt(s) // tps, nxt(s) % tps, 0))
    next_hist = pl.BlockSpec((None, HIST, POOL_WIDTH),
                             lambda s: (nxt(s) // tps, jnp.maximum((nxt(s) % tps) * nh - 1, 0), 0))
    return pl.pallas_call(
        functools.partial(_kb_kernel, layer=layer, final=final, tiles_per_seq=tps),
        out_shape=(jax.ShapeDtypeStruct((g, r, D_MODEL), F32), jax.ShapeDtypeStruct((rs, D_MODEL), F32)),
        grid=(n_tiles + 2,),
        in_specs=[_SMEM, mix_row(D_MODEL), mix_row(ATTN_WIDTH), mix_row(KV_WIDTH), mix_row(KV_WIDTH),
                  prev_kv, prev_kv, first_u, first_hist, next_u, next_hist, mod_spec,
                  _full((rs, D_MODEL)), _full((rs, ATTN_WIDTH)), _full((rs, POOL_WIDTH)), mods_spec,
                  _full((DEPTH, 3, D_MODEL)),
                  _full((N_HEADS * WINDOW, 2 * WINDOW)),
                  _layer_resident((len(POOL_SIZES), POOL_GROUP_WIDTH, POOL_GROUP_WIDTH), layer),
                  _full((DEPTH, POOL_WIDTH)),
                  _layer_resident((D_MODEL, D_MODEL), layer), _layer_resident((D_MODEL, D_FF), layer),
                  _layer_resident((D_MODEL, D_FF), layer), _layer_resident((D_FF, D_MODEL), layer),
                  _full((1, D_MODEL))],
        out_specs=(tail_row(D_MODEL), _full((rs, D_MODEL))),
        scratch_shapes=[pltpu.VMEM((tm, D_MODEL), BF16), pltpu.VMEM((tm, D_MODEL), BF16),
                        pltpu.VMEM((2, tm, D_MODEL), F32), pltpu.VMEM((tm, D_FF), BF16),
                        pltpu.VMEM((2 * HIST + tm, POOL_WIDTH), F32), pltpu.VMEM((2 * HIST + tm, POOL_WIDTH), F32),
                        pltpu.VMEM((2 * HIST + tm, POOL_WIDTH - POOL_GROUP_WIDTH), F32),
                        pltpu.VMEM((tm, POOL_WIDTH), BF16),
                        pltpu.VMEM((N_HEADS * WINDOW, 2 * WINDOW), BF16)],
        compiler_params=pltpu.CompilerParams(
            dimension_semantics=("arbitrary",), vmem_limit_bytes=VMEM_LIMIT),
        name=name,
    )(sinks, x1, q, k, v, k, v, u, u, u, u, mod_all, x1s, attn_s, pooled_s, mods_tail,
      gain, bias, poolw, pscale, wout, wg, wu, wd, fgain)


def _ks_kernel(sinks_ref, q_ref, kn_ref, vn_ref, u_ref, ck_ref, cv_ref, hist_ref, bias_ref,
               o_ref, pooled_ref, nk_ref, nv_ref, newhist_ref,
               qs_ref, knr_ref, vnr_ref, s_ref, p_ref, acc_ref, *, layer):
    bt = q_ref.shape[0]
    for b in range(bt):
        for cache_ref, new_ref, out_ref in ((ck_ref, kn_ref, nk_ref), (cv_ref, vn_ref, nv_ref)):
            out_ref[b, 0:WINDOW - 1, :] = cache_ref[b, 1:WINDOW, :]
            out_ref[b, WINDOW - 1:WINDOW, :] = new_ref[b:b + 1, :]

    for r in range(POOL_STATE - 1):
        newhist_ref[:, r, :] = hist_ref[:, r + 1, :]
    newhist_ref[:, POOL_STATE - 1, :] = u_ref[...]
    for g, w in enumerate(POOL_SIZES):
        cols = slice(g * POOL_GROUP_WIDTH, (g + 1) * POOL_GROUP_WIDTH)
        ug = u_ref[:, cols]
        acc = ug
        for d in range(1, w):
            acc = acc + hist_ref[:, POOL_STATE - d, cols]
        pooled_ref[:, cols] = (acc / float(w) - ug).astype(BF16)

    rows = bt * N_HEADS
    nt = (((1,), (1,)), ((), ()))
    head_rows = lambda r: pl.ds(r, bt, stride=N_HEADS)
    lo = lax.broadcasted_iota(jnp.int32, (bt, LANES), 1) < HEAD_DIM
    kn = kn_ref[...].astype(BF16).astype(F32)
    vn = vn_ref[...].astype(BF16).astype(F32)
    for c in range(4):
        qc = q_ref[:, c * LANES:(c + 1) * LANES].astype(F32)
        qs_ref[head_rows(2 * c), :] = jnp.where(lo, qc, 0.0)
        qs_ref[head_rows(2 * c + 1), :] = jnp.where(lo, 0.0, qc)
    for r in range(N_HEADS):
        knr_ref[head_rows(r), :] = kn
        vnr_ref[head_rows(r), :] = vn

    for b in range(bt):
        rb = slice(b * N_HEADS, (b + 1) * N_HEADS)
        s_ref[rb, :] = lax.dot_general(qs_ref[rb, :].astype(BF16), ck_ref[b].astype(BF16), nt,
                                       preferred_element_type=F32)

    head = jnp.bitwise_and(lax.broadcasted_iota(jnp.int32, (rows, 1), 0), N_HEADS - 1)
    sink = jnp.zeros((rows, 1), F32)
    for r in range(N_HEADS):
        sink = jnp.where(head == r, sinks_ref[layer, _head_of_row(r)], sink)
    bias = jnp.concatenate([bias_ref[...]] * bt, axis=0)
    s_c = s_ref[...] + bias[:, 0:WINDOW]
    s_n = jnp.sum(qs_ref[...] * knr_ref[...], axis=-1, keepdims=True) + bias[:, WINDOW:WINDOW + 1]
    m = jnp.maximum(jnp.maximum(jnp.max(s_c, axis=-1, keepdims=True), s_n), sink)
    p_c = jnp.exp(s_c - m)
    p_n = jnp.exp(s_n - m)
    denom = jnp.sum(p_c, axis=-1, keepdims=True) + p_n + jnp.exp(sink - m)
    p_ref[...] = p_c / denom

    for b in range(bt):
        rb = slice(b * N_HEADS, (b + 1) * N_HEADS)
        acc_ref[rb, :] = jnp.dot(p_ref[rb, :].astype(BF16), cv_ref[b].astype(BF16), preferred_element_type=F32)
    acc_ref[...] = acc_ref[...] + (p_n / denom).astype(BF16).astype(F32) * vnr_ref[...]

    for c in range(4):
        o_ref[:, c * LANES:(c + 1) * LANES] = jnp.where(
            lo, acc_ref[head_rows(2 * c), :], acc_ref[head_rows(2 * c + 1), :]).astype(BF16)


def _ks_call(sinks, q, kn, vn, u, cache_k, cache_v, state_pool, layer, bias_s):
    nb = q.shape[0]
    bt = 32
    rows = bt * N_HEADS
    per_b = lambda w: pl.BlockSpec((bt, w), lambda i: (i, 0))
    cache = pl.BlockSpec((None, bt, WINDOW, KV_WIDTH), lambda i: (layer, i, 0, 0))
    new_cache = pl.BlockSpec((bt, WINDOW, KV_WIDTH), lambda i: (i, 0, 0))
    hist = pl.BlockSpec((None, bt, POOL_STATE, POOL_WIDTH), lambda i: (layer, i, 0, 0))
    new_hist = pl.BlockSpec((bt, POOL_STATE, POOL_WIDTH), lambda i: (i, 0, 0))
    return pl.pallas_call(
        functools.partial(_ks_kernel, layer=layer),
        out_shape=(jax.ShapeDtypeStruct((nb, ATTN_WIDTH), BF16),
                   jax.ShapeDtypeStruct((nb, POOL_WIDTH), BF16),
                   jax.ShapeDtypeStruct((nb, WINDOW, KV_WIDTH), F32),
                   jax.ShapeDtypeStruct((nb, WINDOW, KV_WIDTH), F32),
                   jax.ShapeDtypeStruct((nb, POOL_STATE, POOL_WIDTH), F32)),
        grid=(nb // bt,),
        in_specs=[_SMEM, per_b(ATTN_WIDTH), per_b(KV_WIDTH), per_b(KV_WIDTH), per_b(POOL_WIDTH), cache, cache,
                  hist, _full((N_HEADS, 2 * WINDOW))],
        out_specs=(per_b(ATTN_WIDTH), per_b(POOL_WIDTH), new_cache, new_cache, new_hist),
        scratch_shapes=[pltpu.VMEM((rows, KV_WIDTH), F32), pltpu.VMEM((rows, KV_WIDTH), F32),
                        pltpu.VMEM((rows, KV_WIDTH), F32), pltpu.VMEM((rows, WINDOW), F32),
                        pltpu.VMEM((rows, WINDOW), F32), pltpu.VMEM((rows, KV_WIDTH), F32)],
        compiler_params=pltpu.CompilerParams(dimension_semantics=("arbitrary",), vmem_limit_bytes=VMEM_LIMIT),
        name="sample_mixers",
    )(sinks, q, kn, vn, u, cache_k, cache_v, state_pool, bias_s)


def kernel(x_prompt, x_sample, c_prompt, c_sample, cache_k, cache_v, state_pool, w_ada, b_ada, norm_gain,
           w_in, w_out, sinks, rel_bias, pool_w, pool_scale, ffn1_wg, ffn1_wu, ffn1_wd, ffn2_wg, ffn2_wu,
           ffn2_wd, final_gain):
    n_p, seq, _ = x_prompt.shape
    n_s = x_sample.shape[0]

    mod_all = _modulation(jnp.concatenate([c_sample, c_prompt], axis=0), w_ada, b_ada)
    bias = _bias_table(rel_bias)
    bias_s = bias.reshape(N_HEADS, WINDOW, 2 * WINDOW)[:, 0, :]
    fgain = final_gain.reshape(1, D_MODEL)

    wq = (w_in[:, :, :ATTN_WIDTH].reshape(DEPTH, D_MODEL, 2, 4, HEAD_DIM).transpose(0, 1, 3, 2, 4)
          .reshape(DEPTH, D_MODEL, ATTN_WIDTH)) * (HEAD_DIM ** -0.5)
    win = jnp.concatenate([wq, w_in[:, :, ATTN_WIDTH:]], axis=2).astype(BF16)
    wo_attn = (w_out[:, :ATTN_WIDTH].reshape(DEPTH, 2, 4, HEAD_DIM, D_MODEL).transpose(0, 2, 1, 3, 4)
               .reshape(DEPTH, ATTN_WIDTH, D_MODEL))
    wout = jnp.concatenate([wo_attn, w_out[:, ATTN_WIDTH:]], axis=1).astype(BF16)
    wg1, wu1, wd1 = ffn1_wg.astype(BF16), ffn1_wu.astype(BF16), ffn1_wd.astype(BF16)
    wg2, wu2, wd2 = ffn2_wg.astype(BF16), ffn2_wu.astype(BF16), ffn2_wd.astype(BF16)
    poolw = pool_w.astype(BF16)

    xp = x_prompt
    xs = x_sample.reshape(n_s, D_MODEL)
    ck = cache_k.reshape(DEPTH, n_s, WINDOW, KV_WIDTH)
    cv = cache_v.reshape(DEPTH, n_s, WINDOW, KV_WIDTH)

    mods_tail = mod_all[:, :n_s, TAIL_MOD_FIRST * D_MODEL:]

    new_kp, new_vp, new_pp, new_ks, new_vs, new_ps = [], [], [], [], [], []
    for l in range(DEPTH):
        final = l == DEPTH - 1
        mod_p = pl.BlockSpec((None, n_p, MOD_WIDTH), lambda *_, l=l: (l, n_s // n_p, 0))
        mod_s = pl.BlockSpec((None, n_s, MOD_WIDTH), lambda *_, l=l: (l, 0, 0), pipeline_mode=pl.Buffered(1))

        x1, q, k, v, u, k_last, v_last, u_last, x1s, qs, ks, vs, us = _ka_call(
            xp, xs, mod_all, mod_p, mod_s, norm_gain, wg1, wu1, wd1, win, l, f"ffn1_inproj_l{l}")
        attn_s, pooled_s, ck_next, cv_next, pool_next = _ks_call(sinks, qs, ks, vs, us, ck, cv, state_pool, l, bias_s)
        xp, xs = _kb_call(sinks, x1, q, k, v, u, x1s, attn_s, pooled_s, mod_all, mod_p, mods_tail, norm_gain, bias,
                          poolw, pool_scale, wout, wg2, wu2, wd2, fgain, l, final, f"mixer_ffn2_l{l}")
        new_kp.append(k_last)
        new_vp.append(v_last)
        new_pp.append(u_last)
        new_ks.append(ck_next)
        new_vs.append(cv_next)
        new_ps.append(pool_next)

    kv_prompt = (DEPTH, n_p, WINDOW, N_KV_HEADS, HEAD_DIM)
    kv_sample = (DEPTH, n_s, WINDOW, N_KV_HEADS, HEAD_DIM)
    return (xp, xs.reshape(n_s, 1, D_MODEL), jnp.stack(new_kp).reshape(kv_prompt), jnp.stack(new_vp).reshape(kv_prompt),
            jnp.stack(new_pp)[:, :, HIST - POOL_STATE:], jnp.stack(new_ks).reshape(kv_sample),
            jnp.stack(new_vs).reshape(kv_sample), jnp.stack(new_ps))
```

```python
import functools
import math

import numpy as np
import jax
import jax.numpy as jnp
from jax import lax
from jax.experimental import pallas as pl
from jax.experimental.pallas import tpu as pltpu

D_MODEL = 1024
DEPTH = 2
HEAD_DIM = 64
N_HEADS = 8
N_KV_HEADS = 2
ATTN_WIDTH = N_HEADS * HEAD_DIM
KV_WIDTH = N_KV_HEADS * HEAD_DIM
WINDOW = 128
POOL_SIZES = (2, 4, 8, 16)
POOL_GROUP_WIDTH = 128
POOL_WIDTH = len(POOL_SIZES) * POOL_GROUP_WIDTH
POOL_STATE = max(POOL_SIZES) - 1
IN_WIDTH = ATTN_WIDTH + 2 * KV_WIDTH + POOL_WIDTH
D_FF = 2816
N_BUCKETS = 32
MAX_DISTANCE = 128
N_MOD = 9
MOD_WIDTH = N_MOD * D_MODEL
TAIL_MOD_FIRST = 5
MOD_CHUNK = 3 * D_MODEL
EPS = 1e-6

LANES = 128
SUBLANES = 8
MXU_COLS = 256
FF_CHUNK = 2 * MXU_COLS
FF_CHUNKS = tuple((c, min(FF_CHUNK, D_FF - c)) for c in range(0, D_FF, FF_CHUNK))
TM_PROMPT = 512
SAMPLE_SEQS_PER_STEP = 32
HIST = 16
VMEM_LIMIT = 56 * 1024 * 1024

F32 = jnp.float32
BF16 = jnp.bfloat16
NEG_INF = float("-inf")


def _head_of_row(r):
    return (r // 2) + 4 * (r % 2)


def _t5_bucket_np(dist):
    n = np.maximum(dist, 0)
    max_exact = N_BUCKETS // 2
    nf = np.maximum(n, 1).astype(np.float32)
    large = max_exact + (np.log(nf / max_exact) / math.log(MAX_DISTANCE / max_exact)
                         * (N_BUCKETS - max_exact)).astype(np.int32)
    large = np.minimum(large, N_BUCKETS - 1)
    return np.where(n < max_exact, n, large).astype(np.int32)


def _bucket_table():
    dist = np.arange(WINDOW)[:, None] + WINDOW - np.arange(2 * WINDOW)[None, :]
    return _t5_bucket_np(dist)


def _full(shape):
    return pl.BlockSpec(shape, lambda *_: (0,) * len(shape))


def _layer_resident(shape, layer):
    return pl.BlockSpec((None,) + tuple(shape), lambda *_: (layer,) + (0,) * len(shape),
                        pipeline_mode=pl.Buffered(1))


_SMEM = pl.BlockSpec(memory_space=pltpu.SMEM)


def _rms_mod(x, gain, shift, scale):
    y = x * lax.rsqrt(jnp.mean(x * x, axis=-1, keepdims=True) + EPS)
    return (y * gain) * (1.0 + scale) + shift


def _dot_by_row_halves(a_ref, w):
    half = a_ref.shape[0] // 2
    if half % (2 * SUBLANES) or half < MXU_COLS:
        return jnp.dot(a_ref[...], w, preferred_element_type=F32)
    return jnp.concatenate([jnp.dot(a_ref[:half, :], w, preferred_element_type=F32),
                            jnp.dot(a_ref[half:, :], w, preferred_element_type=F32)], axis=0)


def _swiglu(h_ref, act_ref, wg_ref, wu_ref, wd_ref, side_work=(), before_down=None):
    n = len(FF_CHUNKS)
    assert len(side_work) <= n
    after_chunk = {((j + 1) * n) // (len(side_work) + 1) - 1: work for j, work in enumerate(side_work)}
    chunk_dot = ((lambda a_ref, w: jnp.dot(a_ref[...], w, preferred_element_type=F32)) if side_work
                 else _dot_by_row_halves)
    for i, (c0, cw) in enumerate(FF_CHUNKS):
        g = chunk_dot(h_ref, wg_ref[:, c0:c0 + cw])
        u = chunk_dot(h_ref, wu_ref[:, c0:c0 + cw])
        act_ref[:, c0:c0 + cw] = ((g / (1.0 + jnp.exp(-g))) * u).astype(BF16)
        if i in after_chunk:
            after_chunk[i]()
    if before_down is not None:
        before_down()
    return _dot_by_row_halves(act_ref, wd_ref[...])


def _mod_getter(mod_ref, row=None, first=0):
    rows = slice(None) if row is None else pl.ds(row, 1)
    return lambda k: mod_ref[rows, (k - first) * D_MODEL:(k - first + 1) * D_MODEL]


def _mod_kernel(c_ref, w_ref, b_ref, o_ref, tail_ref):
    c = c_ref[...]
    a = (c / (1.0 + jnp.exp(-c))).astype(BF16)
    m = jnp.dot(a, w_ref[...].astype(BF16), preferred_element_type=F32) + b_ref[...]
    o_ref[...] = m
    rs = tail_ref.shape[0]
    j = pl.program_id(1)
    first_chunk, first_col = divmod(TAIL_MOD_FIRST * D_MODEL, MOD_CHUNK)
    assert first_chunk == MOD_WIDTH // MOD_CHUNK - 2

    @pl.when(j == first_chunk)
    def _():
        tail_ref[:, 0:MOD_CHUNK - first_col] = m[0:rs, first_col:]

    @pl.when(j == first_chunk + 1)
    def _():
        tail_ref[:, MOD_CHUNK - first_col:] = m[0:rs, :]


def _modulation(c_all, w_ada, b_ada, tail_rows):
    rows = c_all.shape[0]
    tail_width = MOD_WIDTH - TAIL_MOD_FIRST * D_MODEL
    return pl.pallas_call(
        _mod_kernel,
        out_shape=(jax.ShapeDtypeStruct((DEPTH, rows, MOD_WIDTH), F32),
                   jax.ShapeDtypeStruct((DEPTH, tail_rows, tail_width), F32)),
        grid=(DEPTH, MOD_WIDTH // MOD_CHUNK),
        in_specs=[
            pl.BlockSpec((rows, D_MODEL), lambda l, j: (0, 0)),
            pl.BlockSpec((None, D_MODEL, MOD_CHUNK), lambda l, j: (l, 0, j)),
            pl.BlockSpec((None, 1, MOD_CHUNK), lambda l, j: (l, 0, j)),
        ],
        out_specs=(pl.BlockSpec((None, rows, MOD_CHUNK), lambda l, j: (l, 0, j)),
                   pl.BlockSpec((None, tail_rows, tail_width), lambda l, j: (l, 0, 0))),
        compiler_params=pltpu.CompilerParams(
            dimension_semantics=("arbitrary", "arbitrary"), vmem_limit_bytes=VMEM_LIMIT),
        name="adaln_modulation",
    )(c_all, w_ada, b_ada.reshape(DEPTH, 1, MOD_WIDTH))


def _bias_kernel(rb_ref, bucket_ref, o_ref):
    bucket = bucket_ref[...]
    eq = [bucket == b for b in range(N_BUCKETS)]
    for r in range(N_HEADS):
        acc = jnp.zeros((WINDOW, 2 * WINDOW), F32)
        for b in range(N_BUCKETS):
            acc = jnp.where(eq[b], rb_ref[b, _head_of_row(r)], acc)
        o_ref[r * WINDOW:(r + 1) * WINDOW, :] = acc


def _bias_table(rel_bias):
    return pl.pallas_call(
        _bias_kernel,
        out_shape=jax.ShapeDtypeStruct((N_HEADS * WINDOW, 2 * WINDOW), F32),
        in_specs=[_SMEM, pl.BlockSpec(memory_space=pltpu.VMEM)],
        out_specs=pl.BlockSpec(memory_space=pltpu.VMEM),
        name="rel_bias_table",
    )(rel_bias, jnp.asarray(_bucket_table()))


def _project_q(hm_ref, win_ref, q_ref):
    q_ref[...] = jnp.dot(hm_ref[...], win_ref[:, :ATTN_WIDTH], preferred_element_type=F32).astype(BF16)


def _project_kvu(hm_ref, win_ref, k_ref, v_ref, u_ref):
    z = jnp.dot(hm_ref[...], win_ref[:, ATTN_WIDTH:], preferred_element_type=F32)
    k_ref[...] = z[:, :KV_WIDTH]
    v_ref[...] = z[:, KV_WIDTH:2 * KV_WIDTH]
    u_ref[...] = z[:, 2 * KV_WIDTH:]
    return z


def _ffn1(x_ref, mod, gain_ref, wg_ref, wu_ref, wd_ref, x1_ref, h_ref, act_ref, layer, before_norm=None):
    x = x_ref[...]
    h_ref[...] = _rms_mod(x, gain_ref[layer, 0:1, :], mod(0), mod(1)).astype(BF16)
    x1 = x + 0.5 * mod(2) * _swiglu(h_ref, act_ref, wg_ref, wu_ref, wd_ref)
    x1_ref[...] = x1
    if before_norm is not None:
        before_norm()
    return _rms_mod(x1, gain_ref[layer, 1:2, :], mod(3), mod(4)).astype(BF16)


def _ka_kernel(x_ref, mod_ref, xs_ref, mods_ref, gain_ref, wg_ref, wu_ref, wd_ref, win_ref,
               x1_ref, q_ref, k_ref, v_ref, u_ref, kl_ref, vl_ref, ul_ref,
               x1s_ref, qs_ref, ks_ref, vs_ref, us_ref,
               h_ref, act_ref, hm_ref, hs_ref, acts_ref, *, layer, tiles_per_seq):
    tm = x_ref.shape[0]
    s = pl.program_id(0)
    n_tiles = pl.num_programs(0) - 2

    @pl.when(s == n_tiles + 1)
    def _():
        hs_ref[...] = _ffn1(xs_ref, _mod_getter(mods_ref), gain_ref, wg_ref, wu_ref, wd_ref, x1s_ref, hs_ref,
                            acts_ref, layer)
        _project_q(hs_ref, win_ref, qs_ref)
        _project_kvu(hs_ref, win_ref, ks_ref, vs_ref, us_ref)

    @pl.when(s == 0)
    def _():
        hm_ref[...] = jnp.zeros_like(hm_ref)

    def project_rest():
        z = _project_kvu(hm_ref, win_ref, k_ref, v_ref, u_ref)
        kl_ref[...] = z[tm - WINDOW:, :KV_WIDTH]
        vl_ref[...] = z[tm - WINDOW:, KV_WIDTH:2 * KV_WIDTH]
        ul_ref[...] = z[tm - HIST:, 2 * KV_WIDTH:]

    @pl.when(s < n_tiles)
    def _():
        _project_q(hm_ref, win_ref, q_ref)
        mod = _mod_getter(mod_ref, s // tiles_per_seq)
        hm = _ffn1(x_ref, mod, gain_ref, wg_ref, wu_ref, wd_ref, x1_ref, h_ref, act_ref, layer,
                   before_norm=project_rest)
        hm_ref[...] = hm

    @pl.when(s == n_tiles)
    def _():
        _project_q(hm_ref, win_ref, q_ref)
        project_rest()


def _ka_out_shapes(g, r):
    return (jax.ShapeDtypeStruct((g, r, D_MODEL), F32),
            jax.ShapeDtypeStruct((g, r, ATTN_WIDTH), BF16),
            jax.ShapeDtypeStruct((g, r, KV_WIDTH), F32),
            jax.ShapeDtypeStruct((g, r, KV_WIDTH), F32),
            jax.ShapeDtypeStruct((g, r, POOL_WIDTH), F32))


def _ka_weight_specs(layer):
    return [_full((DEPTH, 3, D_MODEL)),
            _layer_resident((D_MODEL, D_FF), layer), _layer_resident((D_MODEL, D_FF), layer),
            _layer_resident((D_FF, D_MODEL), layer), _layer_resident((D_MODEL, IN_WIDTH), layer)]


def _ka_call(x, xs, mod_all, mod_spec, mods_spec, gain, wg, wu, wd, win, layer, name):
    g, r, _ = x.shape
    rs = xs.shape[0]
    tm = TM_PROMPT
    tps = r // tm
    n_tiles = g * tps
    main = lambda s: jnp.minimum(s, n_tiles - 1)
    prev = lambda s: jnp.clip(s - 1, 0, n_tiles - 1)
    main_row = lambda w: pl.BlockSpec((None, tm, w), lambda s: (main(s) // tps, main(s) % tps, 0))
    prev_row = lambda w: pl.BlockSpec((None, tm, w), lambda s: (prev(s) // tps, prev(s) % tps, 0))
    last = lambda rows, w: pl.BlockSpec((None, rows, w), lambda s: (prev(s) // tps, 0, 0))
    srow = lambda w: _full((rs, w))
    sample_shapes = tuple(jax.ShapeDtypeStruct(sh.shape[1:], sh.dtype) for sh in _ka_out_shapes(1, rs))
    return pl.pallas_call(
        functools.partial(_ka_kernel, layer=layer, tiles_per_seq=tps),
        out_shape=_ka_out_shapes(g, r) + (jax.ShapeDtypeStruct((g, WINDOW, KV_WIDTH), F32),
                                          jax.ShapeDtypeStruct((g, WINDOW, KV_WIDTH), F32),
                                          jax.ShapeDtypeStruct((g, HIST, POOL_WIDTH), F32)) + sample_shapes,
        grid=(n_tiles + 2,),
        in_specs=[main_row(D_MODEL), mod_spec, srow(D_MODEL), mods_spec] + _ka_weight_specs(layer),
        out_specs=(main_row(D_MODEL), prev_row(ATTN_WIDTH), prev_row(KV_WIDTH), prev_row(KV_WIDTH),
                   prev_row(POOL_WIDTH), last(WINDOW, KV_WIDTH), last(WINDOW, KV_WIDTH), last(HIST, POOL_WIDTH),
                   srow(D_MODEL), srow(ATTN_WIDTH), srow(KV_WIDTH), srow(KV_WIDTH), srow(POOL_WIDTH)),
        scratch_shapes=[pltpu.VMEM((tm, D_MODEL), BF16), pltpu.VMEM((tm, D_FF), BF16),
                        pltpu.VMEM((tm, D_MODEL), BF16),
                        pltpu.VMEM((rs, D_MODEL), BF16), pltpu.VMEM((rs, D_FF), BF16)],
        compiler_params=pltpu.CompilerParams(dimension_semantics=("arbitrary",), vmem_limit_bytes=VMEM_LIMIT),
        name=name,
    )(x, mod_all, xs, mod_all, gain, wg, wu, wd, win)


def _low_half():
    return lax.broadcasted_iota(jnp.int32, (WINDOW, LANES), 1) < HEAD_DIM


def _attn_probs(qb, kk, bias_ref, sinks_ref, layer, valid, p_ref):
    lo = _low_half()
    zero = jnp.zeros((WINDOW, LANES), BF16)
    pieces = []
    for c in range(4):
        qc = qb[:, c * LANES:(c + 1) * LANES]
        pieces += [jnp.where(lo, qc, zero), jnp.where(lo, zero, qc)]
    qs = jnp.concatenate(pieces, axis=0)
    s = lax.dot_general(qs, kk, (((1,), (1,)), ((), ())), preferred_element_type=F32)
    for r in range(N_HEADS):
        rows = slice(r * WINDOW, (r + 1) * WINDOW)
        sr = jnp.where(valid, s[rows] + bias_ref[rows, :], NEG_INF)
        sink = sinks_ref[layer, _head_of_row(r)]
        m = jnp.maximum(jnp.max(sr, axis=-1, keepdims=True), sink)
        p = jnp.exp(sr - m)
        denom = jnp.sum(p, axis=-1, keepdims=True) + jnp.exp(sink - m)
        p_ref[rows, :] = (p / denom).astype(BF16)


def _attn_values(p_ref, vv):
    lo = _low_half()
    o = jnp.dot(p_ref[...], vv, preferred_element_type=F32)
    cols = [jnp.where(lo, o[(2 * c) * WINDOW:(2 * c + 1) * WINDOW], o[(2 * c + 1) * WINDOW:(2 * c + 2) * WINDOW])
            for c in range(4)]
    return jnp.concatenate(cols, axis=1)


def _pool_sums(u_ref, up_ref, uext_ref, t1_ref, t2_ref, pooled_ref, seq_tile):
    tm = u_ref.shape[0]
    g_w = POOL_GROUP_WIDTH
    top = 2 * HIST
    end = top + tm
    s1, s2, s3 = SUBLANES, 2 * SUBLANES, 3 * SUBLANES
    assert POOL_SIZES == (2, 4, 8, 16) and HIST == 2 * SUBLANES and top == 4 * SUBLANES
    uext_ref[0:HIST, :] = jnp.zeros((HIST, POOL_WIDTH), F32)
    uext_ref[HIST:top, :] = jnp.where(seq_tile == 0, 0.0, up_ref[...])
    uext_ref[top:, :] = u_ref[...]
    t1_ref[s1:end, :] = uext_ref[s1:end, :] + uext_ref[s1 - 1:end - 1, :]
    t2_ref[s2:end, :] = t1_ref[s2:end, g_w:] + t1_ref[s2 - 2:end - 2, g_w:]
    t1_ref[s3:end, 2 * g_w:] = t2_ref[s3:end, g_w:] + t2_ref[s3 - 4:end - 4, g_w:]
    sums = [t1_ref[top:, 0:g_w], t2_ref[top:, 0:g_w], t1_ref[top:, 2 * g_w:3 * g_w],
            t1_ref[top:, 3 * g_w:] + t1_ref[top - 8:end - 8, 3 * g_w:]]
    pos = seq_tile * tm + lax.broadcasted_iota(jnp.int32, (tm, g_w), 0)
    for g, w in enumerate(POOL_SIZES):
        cols = slice(g * g_w, (g + 1) * g_w)
        count = jnp.minimum(pos + 1, w).astype(F32)
        pooled_ref[:, cols] = (sums[g] / count - uext_ref[top:, cols]).astype(BF16)


def _mix_prompt_steps(q_ref, k_ref, v_ref, kp_ref, vp_ref, bias_ref, sinks_ref, poolw_ref,
                      pscale_ref, mixed_ref, pooled_ref, p_ref, layer, seq_tile):
    tm = q_ref.shape[0]
    nb = tm // WINDOW
    first = seq_tile == 0

    def keys_values(ref, prev_ref, j):
        prev = prev_ref[...] if j == 0 else ref[(j - 1) * WINDOW:j * WINDOW, :]
        return jnp.concatenate([prev, ref[j * WINDOW:(j + 1) * WINDOW, :]], axis=0).astype(BF16)

    def probs(j):
        qi = lax.broadcasted_iota(jnp.int32, (WINDOW, 2 * WINDOW), 0)
        kj = lax.broadcasted_iota(jnp.int32, (WINDOW, 2 * WINDOW), 1)
        dist = qi + WINDOW - kj
        valid = (dist >= 0) & (dist <= WINDOW)
        if j == 0:
            valid = valid & (kj >= jnp.where(first, WINDOW, 0))
        _attn_probs(q_ref[j * WINDOW:(j + 1) * WINDOW, :], keys_values(k_ref, kp_ref, j), bias_ref, sinks_ref,
                    layer, valid, p_ref)

    def values(j):
        a = _attn_values(p_ref, keys_values(v_ref, vp_ref, j))
        mixed_ref[j * WINDOW:(j + 1) * WINDOW, 0:ATTN_WIDTH] = a.astype(BF16)

    def pool_map(g):
        cols = slice(g * POOL_GROUP_WIDTH, (g + 1) * POOL_GROUP_WIDTH)
        pg = jnp.dot(pooled_ref[:, cols], poolw_ref[g], preferred_element_type=F32) * pscale_ref[layer:layer + 1, cols]
        mixed_ref[:, ATTN_WIDTH + g * POOL_GROUP_WIDTH:ATTN_WIDTH + (g + 1) * POOL_GROUP_WIDTH] = pg.astype(BF16)

    assert nb == len(POOL_SIZES)

    def stage(i):
        def run():
            if i > 0:
                values(i - 1)
                pool_map(i - 1)
            if i < nb:
                probs(i)
        return run

    return [stage(i) for i in range(nb + 1)]


def _out_projection(x1, mixed_ref, wout_ref, mod, gain_ref, layer):
    x2 = x1 + mod(5) * _dot_by_row_halves(mixed_ref, wout_ref[...])
    return x2, _rms_mod(x2, gain_ref[layer, 2:3, :], mod(6), mod(7)).astype(BF16)


def _ffn2_out(x2, y, mod, fgain_ref, final):
    x3 = x2 + 0.5 * mod(8) * y
    if final:
        x3 = x3 * lax.rsqrt(jnp.mean(x3 * x3, axis=-1, keepdims=True) + EPS) * fgain_ref[...]
    return x3


def _kb_kernel(sinks_ref, x1_ref, q_ref, k_ref, v_ref, kp_ref, vp_ref, u_ref, up_ref, un_ref, upn_ref,
               mod_ref, x1s_ref, attns_ref, pooleds_ref, mods_ref,
               gain_ref, bias_ref, poolw_ref, pscale_ref, wout_ref, wg_ref, wu_ref, wd_ref, fgain_ref,
               o_ref, os_ref, mixed_ref, h_ref, x2_ref, act_ref, uext_ref, t1_ref, t2_ref, pooled_ref, p_ref,
               *, layer, final, tiles_per_seq):
    s = pl.program_id(0)
    n_tiles = pl.num_programs(0) - 2

    @pl.when(s == n_tiles + 1)
    def _():
        rs = x1s_ref.shape[0]
        mixed_s, h_s, act_s = (ref.at[pl.ds(0, rs), :] for ref in (mixed_ref, h_ref, act_ref))
        mixed_s[:, 0:ATTN_WIDTH] = attns_ref[...]
        for g in range(len(POOL_SIZES)):
            cols = slice(g * POOL_GROUP_WIDTH, (g + 1) * POOL_GROUP_WIDTH)
            pg = (jnp.dot(pooleds_ref[:, cols], poolw_ref[g], preferred_element_type=F32)
                  * pscale_ref[layer:layer + 1, cols])
            mixed_s[:, ATTN_WIDTH + g * POOL_GROUP_WIDTH:ATTN_WIDTH + (g + 1) * POOL_GROUP_WIDTH] = pg.astype(BF16)
        mod = _mod_getter(mods_ref, first=TAIL_MOD_FIRST)
        x2, h = _out_projection(x1s_ref[...], mixed_s, wout_ref, mod, gain_ref, layer)
        h_s[...] = h
        os_ref[...] = _ffn2_out(x2, _swiglu(h_s, act_s, wg_ref, wu_ref, wd_ref), mod, fgain_ref, final)
    mix_tile = jnp.minimum(s, n_tiles - 1)
    next_tile = jnp.minimum(s + 1, n_tiles - 1)
    ffn_tile = jnp.maximum(s - 1, 0)
    slot = s % 2

    mix_steps = _mix_prompt_steps(q_ref, k_ref, v_ref, kp_ref, vp_ref, bias_ref, sinks_ref, poolw_ref, pscale_ref,
                                  mixed_ref, pooled_ref, p_ref, layer, mix_tile % tiles_per_seq)

    mix_mod = _mod_getter(mod_ref, mix_tile // tiles_per_seq)
    last_mix_step = mix_steps[-1]

    def finish_mix():
        last_mix_step()
        x2_ref[slot] = x1_ref[...] + mix_mod(5) * _dot_by_row_halves(mixed_ref, wout_ref[...])

    def norm_and_pool_ahead():
        h_ref[...] = _rms_mod(x2_ref[slot], gain_ref[layer, 2:3, :], mix_mod(6), mix_mod(7)).astype(BF16)
        _pool_sums(un_ref, upn_ref, uext_ref, t1_ref, t2_ref, pooled_ref, next_tile % tiles_per_seq)

    def ffn2(**mixing):
        y = _swiglu(h_ref, act_ref, wg_ref, wu_ref, wd_ref, **mixing)
        o_ref[...] = _ffn2_out(x2_ref[1 - slot], y, _mod_getter(mod_ref, ffn_tile // tiles_per_seq), fgain_ref, final)

    @pl.when(s == 0)
    def _():
        _pool_sums(u_ref, up_ref, uext_ref, t1_ref, t2_ref, pooled_ref, 0)
        for step in mix_steps[:-1]:
            step()
        finish_mix()
        norm_and_pool_ahead()

    @pl.when(jnp.logical_and(s > 0, s < n_tiles))
    def _():
        ffn2(side_work=mix_steps[:-1] + [finish_mix], before_down=norm_and_pool_ahead)

    @pl.when(s == n_tiles)
    def _():
        ffn2()


def _kb_call(sinks, x1, q, k, v, u, x1s, attn_s, pooled_s, mod_all, mod_spec, mods_tail, gain, bias, poolw, pscale,
             wout, wg, wu, wd, fgain, layer, final, name):
    mods_spec = pl.BlockSpec((None,) + mods_tail.shape[1:], lambda s: (layer, 0, 0), pipeline_mode=pl.Buffered(1))
    g, r, _ = x1.shape
    rs = x1s.shape[0]
    tm = TM_PROMPT
    tps = r // tm
    n_tiles = g * tps
    nb = tm // WINDOW
    nh = tm // HIST
    mix = lambda s: jnp.minimum(s, n_tiles - 1)
    tail = lambda s: jnp.clip(s - 1, 0, n_tiles - 1)
    mix_row = lambda w: pl.BlockSpec((None, tm, w), lambda s: (mix(s) // tps, mix(s) % tps, 0))
    tail_row = lambda w: pl.BlockSpec((None, tm, w), lambda s: (tail(s) // tps, tail(s) % tps, 0))
    prev_kv = pl.BlockSpec((None, WINDOW, KV_WIDTH),
                           lambda s: (mix(s) // tps, jnp.maximum((mix(s) % tps) * nb - 1, 0), 0))
    nxt = lambda s: jnp.minimum(s + 1, n_tiles - 1)
    first_u = pl.BlockSpec((None, tm, POOL_WIDTH), lambda s: (0, 0, 0))
    first_hist = pl.BlockSpec((None, HIST, POOL_WIDTH), lambda s: (0, 0, 0))
    next_u = pl.BlockSpec((None, tm, POOL_WIDTH), lambda s: (nxt(s) // tps, nxt(s) % tps, 0))
    next_hist = pl.BlockSpec((None, HIST, POOL_WIDTH),
                             lambda s: (nxt(s) // tps, jnp.maximum((nxt(s) % tps) * nh - 1, 0), 0))
    return pl.pallas_call(
        functools.partial(_kb_kernel, layer=layer, final=final, tiles_per_seq=tps),
        out_shape=(jax.ShapeDtypeStruct((g, r, D_MODEL), F32), jax.ShapeDtypeStruct((rs, D_MODEL), F32)),
        grid=(n_tiles + 2,),
        in_specs=[_SMEM, mix_row(D_MODEL), mix_row(ATTN_WIDTH), mix_row(KV_WIDTH), mix_row(KV_WIDTH),
                  prev_kv, prev_kv, first_u, first_hist, next_u, next_hist, mod_spec,
                  _full((rs, D_MODEL)), _full((rs, ATTN_WIDTH)), _full((rs, POOL_WIDTH)), mods_spec,
                  _full((DEPTH, 3, D_MODEL)),
                  _full((N_HEADS * WINDOW, 2 * WINDOW)),
                  _layer_resident((len(POOL_SIZES), POOL_GROUP_WIDTH, POOL_GROUP_WIDTH), layer),
                  _full((DEPTH, POOL_WIDTH)),
                  _layer_resident((D_MODEL, D_MODEL), layer), _layer_resident((D_MODEL, D_FF), layer),
                  _layer_resident((D_MODEL, D_FF), layer), _layer_resident((D_FF, D_MODEL), layer),
                  _full((1, D_MODEL))],
        out_specs=(tail_row(D_MODEL), _full((rs, D_MODEL))),
        scratch_shapes=[pltpu.VMEM((tm, D_MODEL), BF16), pltpu.VMEM((tm, D_MODEL), BF16),
                        pltpu.VMEM((2, tm, D_MODEL), F32), pltpu.VMEM((tm, D_FF), BF16),
                        pltpu.VMEM((2 * HIST + tm, POOL_WIDTH), F32), pltpu.VMEM((2 * HIST + tm, POOL_WIDTH), F32),
                        pltpu.VMEM((2 * HIST + tm, POOL_WIDTH - POOL_GROUP_WIDTH), F32),
                        pltpu.VMEM((tm, POOL_WIDTH), BF16),
                        pltpu.VMEM((N_HEADS * WINDOW, 2 * WINDOW), BF16)],
        compiler_params=pltpu.CompilerParams(
            dimension_semantics=("arbitrary",), vmem_limit_bytes=VMEM_LIMIT),
        name=name,
    )(sinks, x1, q, k, v, k, v, u, u, u, u, mod_all, x1s, attn_s, pooled_s, mods_tail,
      gain, bias, poolw, pscale, wout, wg, wu, wd, fgain)


def _ks_kernel(sinks_ref, q_ref, kn_ref, vn_ref, u_ref, ck_ref, cv_ref, hist_ref, bias_ref,
               o_ref, pooled_ref, nk_ref, nv_ref, newhist_ref,
               qs_ref, knr_ref, vnr_ref, s_ref, p_ref, acc_ref, *, layer):
    bt = q_ref.shape[0]
    for b in range(bt):
        for cache_ref, new_ref, out_ref in ((ck_ref, kn_ref, nk_ref), (cv_ref, vn_ref, nv_ref)):
            out_ref[b, 0:WINDOW - 1, :] = cache_ref[b, 1:WINDOW, :]
            out_ref[b, WINDOW - 1:WINDOW, :] = new_ref[b:b + 1, :]

    for r in range(POOL_STATE - 1):
        newhist_ref[:, r, :] = hist_ref[:, r + 1, :]
    newhist_ref[:, POOL_STATE - 1, :] = u_ref[...]
    for g, w in enumerate(POOL_SIZES):
        cols = slice(g * POOL_GROUP_WIDTH, (g + 1) * POOL_GROUP_WIDTH)
        ug = u_ref[:, cols]
        acc = ug
        for d in range(1, w):
            acc = acc + hist_ref[:, POOL_STATE - d, cols]
        pooled_ref[:, cols] = (acc / float(w) - ug).astype(BF16)

    rows = bt * N_HEADS
    nt = (((1,), (1,)), ((), ()))
    head_rows = lambda r: pl.ds(r, bt, stride=N_HEADS)
    lo = lax.broadcasted_iota(jnp.int32, (bt, LANES), 1) < HEAD_DIM
    kn = kn_ref[...].astype(BF16).astype(F32)
    vn = vn_ref[...].astype(BF16).astype(F32)
    for c in range(4):
        qc = q_ref[:, c * LANES:(c + 1) * LANES].astype(F32)
        qs_ref[head_rows(2 * c), :] = jnp.where(lo, qc, 0.0)
        qs_ref[head_rows(2 * c + 1), :] = jnp.where(lo, 0.0, qc)
    for r in range(N_HEADS):
        knr_ref[head_rows(r), :] = kn
        vnr_ref[head_rows(r), :] = vn

    for b in range(bt):
        rb = slice(b * N_HEADS, (b + 1) * N_HEADS)
        s_ref[rb, :] = lax.dot_general(qs_ref[rb, :].astype(BF16), ck_ref[b].astype(BF16), nt,
                                       preferred_element_type=F32)

    head = jnp.bitwise_and(lax.broadcasted_iota(jnp.int32, (rows, 1), 0), N_HEADS - 1)
    sink = jnp.zeros((rows, 1), F32)
    for r in range(N_HEADS):
        sink = jnp.where(head == r, sinks_ref[layer, _head_of_row(r)], sink)
    bias = jnp.concatenate([bias_ref[...]] * bt, axis=0)
    s_c = s_ref[...] + bias[:, 0:WINDOW]
    s_n = jnp.sum(qs_ref[...] * knr_ref[...], axis=-1, keepdims=True) + bias[:, WINDOW:WINDOW + 1]
    m = jnp.maximum(jnp.maximum(jnp.max(s_c, axis=-1, keepdims=True), s_n), sink)
    p_c = jnp.exp(s_c - m)
    p_n = jnp.exp(s_n - m)
    denom = jnp.sum(p_c, axis=-1, keepdims=True) + p_n + jnp.exp(sink - m)
    p_ref[...] = p_c / denom

    for b in range(bt):
        rb = slice(b * N_HEADS, (b + 1) * N_HEADS)
        acc_ref[rb, :] = jnp.dot(p_ref[rb, :].astype(BF16), cv_ref[b].astype(BF16), preferred_element_type=F32)
    acc_ref[...] = acc_ref[...] + (p_n / denom).astype(BF16).astype(F32) * vnr_ref[...]

    for c in range(4):
        o_ref[:, c * LANES:(c + 1) * LANES] = jnp.where(
            lo, acc_ref[head_rows(2 * c), :], acc_ref[head_rows(2 * c + 1), :]).astype(BF16)


def _ks_call(sinks, q, kn, vn, u, cache_k, cache_v, state_pool, layer, bias_s):
    nb = q.shape[0]
    bt = SAMPLE_SEQS_PER_STEP
    rows = bt * N_HEADS
    per_b = lambda w: pl.BlockSpec((bt, w), lambda i: (i, 0))
    cache = pl.BlockSpec((None, bt, WINDOW, KV_WIDTH), lambda i: (layer, i, 0, 0))
    new_cache = pl.BlockSpec((bt, WINDOW, KV_WIDTH), lambda i: (i, 0, 0))
    hist = pl.BlockSpec((None, bt, POOL_STATE, POOL_WIDTH), lambda i: (layer, i, 0, 0))
    new_hist = pl.BlockSpec((bt, POOL_STATE, POOL_WIDTH), lambda i: (i, 0, 0))
    return pl.pallas_call(
        functools.partial(_ks_kernel, layer=layer),
        out_shape=(jax.ShapeDtypeStruct((nb, ATTN_WIDTH), BF16),
                   jax.ShapeDtypeStruct((nb, POOL_WIDTH), BF16),
                   jax.ShapeDtypeStruct((nb, WINDOW, KV_WIDTH), F32),
                   jax.ShapeDtypeStruct((nb, WINDOW, KV_WIDTH), F32),
                   jax.ShapeDtypeStruct((nb, POOL_STATE, POOL_WIDTH), F32)),
        grid=(nb // bt,),
        in_specs=[_SMEM, per_b(ATTN_WIDTH), per_b(KV_WIDTH), per_b(KV_WIDTH), per_b(POOL_WIDTH), cache, cache,
                  hist, _full((N_HEADS, 2 * WINDOW))],
        out_specs=(per_b(ATTN_WIDTH), per_b(POOL_WIDTH), new_cache, new_cache, new_hist),
        scratch_shapes=[pltpu.VMEM((rows, KV_WIDTH), F32), pltpu.VMEM((rows, KV_WIDTH), F32),
                        pltpu.VMEM((rows, KV_WIDTH), F32), pltpu.VMEM((rows, WINDOW), F32),
                        pltpu.VMEM((rows, WINDOW), F32), pltpu.VMEM((rows, KV_WIDTH), F32)],
        compiler_params=pltpu.CompilerParams(dimension_semantics=("arbitrary",), vmem_limit_bytes=VMEM_LIMIT),
        name="sample_mixers",
    )(sinks, q, kn, vn, u, cache_k, cache_v, state_pool, bias_s)


def kernel(x_prompt, x_sample, c_prompt, c_sample, cache_k, cache_v, state_pool, w_ada, b_ada, norm_gain,
           w_in, w_out, sinks, rel_bias, pool_w, pool_scale, ffn1_wg, ffn1_wu, ffn1_wd, ffn2_wg, ffn2_wu,
           ffn2_wd, final_gain):
    n_p, seq, _ = x_prompt.shape
    n_s = x_sample.shape[0]

    mod_all, mods_tail = _modulation(jnp.concatenate([c_sample, c_prompt], axis=0), w_ada, b_ada, n_s)
    bias = _bias_table(rel_bias)
    bias_s = bias.reshape(N_HEADS, WINDOW, 2 * WINDOW)[:, 0, :]
    fgain = final_gain.reshape(1, D_MODEL)

    wq = (w_in[:, :, :ATTN_WIDTH].reshape(DEPTH, D_MODEL, 2, 4, HEAD_DIM).transpose(0, 1, 3, 2, 4)
          .reshape(DEPTH, D_MODEL, ATTN_WIDTH)) * (HEAD_DIM ** -0.5)
    win = jnp.concatenate([wq, w_in[:, :, ATTN_WIDTH:]], axis=2).astype(BF16)
    wo_attn = (w_out[:, :ATTN_WIDTH].reshape(DEPTH, 2, 4, HEAD_DIM, D_MODEL).transpose(0, 2, 1, 3, 4)
               .reshape(DEPTH, ATTN_WIDTH, D_MODEL))
    wout = jnp.concatenate([wo_attn, w_out[:, ATTN_WIDTH:]], axis=1).astype(BF16)
    wg1, wu1, wd1 = ffn1_wg.astype(BF16), ffn1_wu.astype(BF16), ffn1_wd.astype(BF16)
    wg2, wu2, wd2 = ffn2_wg.astype(BF16), ffn2_wu.astype(BF16), ffn2_wd.astype(BF16)
    poolw = pool_w.astype(BF16)

    xp = x_prompt
    xs = x_sample.reshape(n_s, D_MODEL)
    ck = cache_k.reshape(DEPTH, n_s, WINDOW, KV_WIDTH)
    cv = cache_v.reshape(DEPTH, n_s, WINDOW, KV_WIDTH)

    new_kp, new_vp, new_pp, new_ks, new_vs, new_ps = [], [], [], [], [], []
    for l in range(DEPTH):
        final = l == DEPTH - 1
        mod_p = pl.BlockSpec((None, n_p, MOD_WIDTH), lambda *_, l=l: (l, n_s // n_p, 0))
        mod_s = pl.BlockSpec((None, n_s, MOD_WIDTH), lambda *_, l=l: (l, 0, 0), pipeline_mode=pl.Buffered(1))

        x1, q, k, v, u, k_last, v_last, u_last, x1s, qs, ks, vs, us = _ka_call(
            xp, xs, mod_all, mod_p, mod_s, norm_gain, wg1, wu1, wd1, win, l, f"ffn1_inproj_l{l}")
        attn_s, pooled_s, ck_next, cv_next, pool_next = _ks_call(sinks, qs, ks, vs, us, ck, cv, state_pool, l, bias_s)
        xp, xs = _kb_call(sinks, x1, q, k, v, u, x1s, attn_s, pooled_s, mod_all, mod_p, mods_tail, norm_gain, bias,
                          poolw, pool_scale, wout, wg2, wu2, wd2, fgain, l, final, f"mixer_ffn2_l{l}")
        new_kp.append(k_last)
        new_vp.append(v_last)
        new_pp.append(u_last)
        new_ks.append(ck_next)
        new_vs.append(cv_next)
        new_ps.append(pool_next)

    kv_prompt = (DEPTH, n_p, WINDOW, N_KV_HEADS, HEAD_DIM)
    kv_sample = (DEPTH, n_s, WINDOW, N_KV_HEADS, HEAD_DIM)
    return (xp, xs.reshape(n_s, 1, D_MODEL), jnp.stack(new_kp).reshape(kv_prompt), jnp.stack(new_vp).reshape(kv_prompt),
            jnp.stack(new_pp)[:, :, HIST - POOL_STATE:], jnp.stack(new_ks).reshape(kv_sample),
            jnp.stack(new_vs).reshape(kv_sample), jnp.stack(new_ps))
```

```python
import functools
import math

import numpy as np
import jax
import jax.numpy as jnp
from jax import lax
from jax.experimental import pallas as pl
from jax.experimental.pallas import tpu as pltpu

D_MODEL = 1024
DEPTH = 2
HEAD_DIM = 64
N_HEADS = 8
N_KV_HEADS = 2
ATTN_WIDTH = N_HEADS * HEAD_DIM
KV_WIDTH = N_KV_HEADS * HEAD_DIM
WINDOW = 128
POOL_SIZES = (2, 4, 8, 16)
POOL_GROUP_WIDTH = 128
POOL_WIDTH = len(POOL_SIZES) * POOL_GROUP_WIDTH
POOL_STATE = max(POOL_SIZES) - 1
IN_WIDTH = ATTN_WIDTH + 2 * KV_WIDTH + POOL_WIDTH
D_FF = 2816
N_BUCKETS = 32
MAX_DISTANCE = 128
N_MOD = 9
MOD_WIDTH = N_MOD * D_MODEL
TAIL_MOD_FIRST = 5
MOD_CHUNK = 3 * D_MODEL
EPS = 1e-6

LANES = 128
SUBLANES = 8
MXU_COLS = 256
FF_CHUNK = 2 * MXU_COLS
FF_CHUNKS = tuple((c, min(FF_CHUNK, D_FF - c)) for c in range(0, D_FF, FF_CHUNK))
TM_PROMPT = 512
SAMPLE_SEQS_PER_STEP = 32
HIST = 16
VMEM_LIMIT = 56 * 1024 * 1024

F32 = jnp.float32
BF16 = jnp.bfloat16
NEG_INF = float("-inf")


def _head_of_row(r):
    return (r // 2) + 4 * (r % 2)


def _t5_bucket_np(dist):
    n = np.maximum(dist, 0)
    max_exact = N_BUCKETS // 2
    nf = np.maximum(n, 1).astype(np.float32)
    large = max_exact + (np.log(nf / max_exact) / math.log(MAX_DISTANCE / max_exact)
                         * (N_BUCKETS - max_exact)).astype(np.int32)
    large = np.minimum(large, N_BUCKETS - 1)
    return np.where(n < max_exact, n, large).astype(np.int32)


def _bucket_table():
    dist = np.arange(WINDOW)[:, None] + WINDOW - np.arange(2 * WINDOW)[None, :]
    return _t5_bucket_np(dist)


def _full(shape):
    return pl.BlockSpec(shape, lambda *_: (0,) * len(shape))


def _layer_resident(shape, layer):
    return pl.BlockSpec((None,) + tuple(shape), lambda *_: (layer,) + (0,) * len(shape),
                        pipeline_mode=pl.Buffered(1))


_SMEM = pl.BlockSpec(memory_space=pltpu.SMEM)


def _rms_mod(x, gain, shift, scale):
    y = x * lax.rsqrt(jnp.mean(x * x, axis=-1, keepdims=True) + EPS)
    return (y * gain) * (1.0 + scale) + shift


def _dot_by_row_halves(a_ref, w):
    half = a_ref.shape[0] // 2
    if half % (2 * SUBLANES) or half < MXU_COLS:
        return jnp.dot(a_ref[...], w, preferred_element_type=F32)
    return jnp.concatenate([jnp.dot(a_ref[:half, :], w, preferred_element_type=F32),
                            jnp.dot(a_ref[half:, :], w, preferred_element_type=F32)], axis=0)


def _swiglu(h_ref, act_ref, wg_ref, wu_ref, wd_ref, side_work=(), before_down=None):
    n = len(FF_CHUNKS)
    assert len(side_work) <= n
    after_chunk = {((j + 1) * n) // (len(side_work) + 1) - 1: work for j, work in enumerate(side_work)}
    chunk_dot = ((lambda a_ref, w: jnp.dot(a_ref[...], w, preferred_element_type=F32)) if side_work
                 else _dot_by_row_halves)
    for i, (c0, cw) in enumerate(FF_CHUNKS):
        g = chunk_dot(h_ref, wg_ref[:, c0:c0 + cw])
        u = chunk_dot(h_ref, wu_ref[:, c0:c0 + cw])
        half_g = 0.5 * g
        act_ref[:, c0:c0 + cw] = ((half_g * (1.0 + jnp.tanh(half_g))) * u).astype(BF16)
        if i in after_chunk:
            after_chunk[i]()
    if before_down is not None:
        before_down()
    return _dot_by_row_halves(act_ref, wd_ref[...])


def _mod_getter(mod_ref, row=None, first=0):
    rows = slice(None) if row is None else pl.ds(row, 1)
    return lambda k: mod_ref[rows, (k - first) * D_MODEL:(k - first + 1) * D_MODEL]


def _mod_kernel(c_ref, w_ref, b_ref, o_ref, tail_ref):
    c = c_ref[...]
    a = (c / (1.0 + jnp.exp(-c))).astype(BF16)
    m = jnp.dot(a, w_ref[...].astype(BF16), preferred_element_type=F32) + b_ref[...]
    o_ref[...] = m
    rs = tail_ref.shape[0]
    j = pl.program_id(1)
    first_chunk, first_col = divmod(TAIL_MOD_FIRST * D_MODEL, MOD_CHUNK)
    assert first_chunk == MOD_WIDTH // MOD_CHUNK - 2

    @pl.when(j == first_chunk)
    def _():
        tail_ref[:, 0:MOD_CHUNK - first_col] = m[0:rs, first_col:]

    @pl.when(j == first_chunk + 1)
    def _():
        tail_ref[:, MOD_CHUNK - first_col:] = m[0:rs, :]


def _modulation(c_all, w_ada, b_ada, tail_rows):
    rows = c_all.shape[0]
    tail_width = MOD_WIDTH - TAIL_MOD_FIRST * D_MODEL
    return pl.pallas_call(
        _mod_kernel,
        out_shape=(jax.ShapeDtypeStruct((DEPTH, rows, MOD_WIDTH), F32),
                   jax.ShapeDtypeStruct((DEPTH, tail_rows, tail_width), F32)),
        grid=(DEPTH, MOD_WIDTH // MOD_CHUNK),
        in_specs=[
            pl.BlockSpec((rows, D_MODEL), lambda l, j: (0, 0)),
            pl.BlockSpec((None, D_MODEL, MOD_CHUNK), lambda l, j: (l, 0, j)),
            pl.BlockSpec((None, 1, MOD_CHUNK), lambda l, j: (l, 0, j)),
        ],
        out_specs=(pl.BlockSpec((None, rows, MOD_CHUNK), lambda l, j: (l, 0, j)),
                   pl.BlockSpec((None, tail_rows, tail_width), lambda l, j: (l, 0, 0))),
        compiler_params=pltpu.CompilerParams(
            dimension_semantics=("arbitrary", "arbitrary"), vmem_limit_bytes=VMEM_LIMIT),
        name="adaln_modulation",
    )(c_all, w_ada, b_ada.reshape(DEPTH, 1, MOD_WIDTH))


def _bias_kernel(rb_ref, bucket_ref, o_ref):
    bucket = bucket_ref[...]
    eq = [bucket == b for b in range(N_BUCKETS)]
    for r in range(N_HEADS):
        acc = jnp.zeros((WINDOW, 2 * WINDOW), F32)
        for b in range(N_BUCKETS):
            acc = jnp.where(eq[b], rb_ref[b, _head_of_row(r)], acc)
        o_ref[r * WINDOW:(r + 1) * WINDOW, :] = acc


def _bias_table(rel_bias):
    return pl.pallas_call(
        _bias_kernel,
        out_shape=jax.ShapeDtypeStruct((N_HEADS * WINDOW, 2 * WINDOW), F32),
        in_specs=[_SMEM, pl.BlockSpec(memory_space=pltpu.VMEM)],
        out_specs=pl.BlockSpec(memory_space=pltpu.VMEM),
        name="rel_bias_table",
    )(rel_bias, jnp.asarray(_bucket_table()))


def _project_q(hm_ref, win_ref, q_ref):
    q_ref[...] = jnp.dot(hm_ref[...], win_ref[:, :ATTN_WIDTH], preferred_element_type=F32).astype(BF16)


def _project_kvu(hm_ref, win_ref, k_ref, v_ref, u_ref):
    z = jnp.dot(hm_ref[...], win_ref[:, ATTN_WIDTH:], preferred_element_type=F32)
    k_ref[...] = z[:, :KV_WIDTH]
    v_ref[...] = z[:, KV_WIDTH:2 * KV_WIDTH]
    u_ref[...] = z[:, 2 * KV_WIDTH:]
    return z


def _ffn1(x_ref, mod, gain_ref, wg_ref, wu_ref, wd_ref, x1_ref, h_ref, act_ref, layer, before_norm=None):
    x = x_ref[...]
    h_ref[...] = _rms_mod(x, gain_ref[layer, 0:1, :], mod(0), mod(1)).astype(BF16)
    x1 = x + 0.5 * mod(2) * _swiglu(h_ref, act_ref, wg_ref, wu_ref, wd_ref)
    x1_ref[...] = x1
    if before_norm is not None:
        before_norm()
    return _rms_mod(x1, gain_ref[layer, 1:2, :], mod(3), mod(4)).astype(BF16)


def _ka_kernel(x_ref, mod_ref, xs_ref, mods_ref, gain_ref, wg_ref, wu_ref, wd_ref, win_ref,
               x1_ref, q_ref, k_ref, v_ref, u_ref, kl_ref, vl_ref, ul_ref,
               x1s_ref, qs_ref, ks_ref, vs_ref, us_ref,
               h_ref, act_ref, hm_ref, hs_ref, acts_ref, *, layer, tiles_per_seq):
    tm = x_ref.shape[0]
    s = pl.program_id(0)
    n_tiles = pl.num_programs(0) - 2

    @pl.when(s == n_tiles + 1)
    def _():
        hs_ref[...] = _ffn1(xs_ref, _mod_getter(mods_ref), gain_ref, wg_ref, wu_ref, wd_ref, x1s_ref, hs_ref,
                            acts_ref, layer)
        _project_q(hs_ref, win_ref, qs_ref)
        _project_kvu(hs_ref, win_ref, ks_ref, vs_ref, us_ref)

    @pl.when(s == 0)
    def _():
        hm_ref[...] = jnp.zeros_like(hm_ref)

    def project_rest():
        z = _project_kvu(hm_ref, win_ref, k_ref, v_ref, u_ref)
        kl_ref[...] = z[tm - WINDOW:, :KV_WIDTH]
        vl_ref[...] = z[tm - WINDOW:, KV_WIDTH:2 * KV_WIDTH]
        ul_ref[...] = z[tm - HIST:, 2 * KV_WIDTH:]

    @pl.when(s < n_tiles)
    def _():
        _project_q(hm_ref, win_ref, q_ref)
        mod = _mod_getter(mod_ref, s // tiles_per_seq)
        hm = _ffn1(x_ref, mod, gain_ref, wg_ref, wu_ref, wd_ref, x1_ref, h_ref, act_ref, layer,
                   before_norm=project_rest)
        hm_ref[...] = hm

    @pl.when(s == n_tiles)
    def _():
        _project_q(hm_ref, win_ref, q_ref)
        project_rest()


def _ka_out_shapes(g, r):
    return (jax.ShapeDtypeStruct((g, r, D_MODEL), F32),
            jax.ShapeDtypeStruct((g, r, ATTN_WIDTH), BF16),
            jax.ShapeDtypeStruct((g, r, KV_WIDTH), F32),
            jax.ShapeDtypeStruct((g, r, KV_WIDTH), F32),
            jax.ShapeDtypeStruct((g, r, POOL_WIDTH), F32))


def _ka_weight_specs(layer):
    return [_full((DEPTH, 3, D_MODEL)),
            _layer_resident((D_MODEL, D_FF), layer), _layer_resident((D_MODEL, D_FF), layer),
            _layer_resident((D_FF, D_MODEL), layer), _layer_resident((D_MODEL, IN_WIDTH), layer)]


def _ka_call(x, xs, mod_all, mod_spec, mods_spec, gain, wg, wu, wd, win, layer, name):
    g, r, _ = x.shape
    rs = xs.shape[0]
    tm = TM_PROMPT
    tps = r // tm
    n_tiles = g * tps
    main = lambda s: jnp.minimum(s, n_tiles - 1)
    prev = lambda s: jnp.clip(s - 1, 0, n_tiles - 1)
    main_row = lambda w: pl.BlockSpec((None, tm, w), lambda s: (main(s) // tps, main(s) % tps, 0))
    prev_row = lambda w: pl.BlockSpec((None, tm, w), lambda s: (prev(s) // tps, prev(s) % tps, 0))
    last = lambda rows, w: pl.BlockSpec((None, rows, w), lambda s: (prev(s) // tps, 0, 0))
    srow = lambda w: _full((rs, w))
    sample_shapes = tuple(jax.ShapeDtypeStruct(sh.shape[1:], sh.dtype) for sh in _ka_out_shapes(1, rs))
    return pl.pallas_call(
        functools.partial(_ka_kernel, layer=layer, tiles_per_seq=tps),
        out_shape=_ka_out_shapes(g, r) + (jax.ShapeDtypeStruct((g, WINDOW, KV_WIDTH), F32),
                                          jax.ShapeDtypeStruct((g, WINDOW, KV_WIDTH), F32),
                                          jax.ShapeDtypeStruct((g, HIST, POOL_WIDTH), F32)) + sample_shapes,
        grid=(n_tiles + 2,),
        in_specs=[main_row(D_MODEL), mod_spec, srow(D_MODEL), mods_spec] + _ka_weight_specs(layer),
        out_specs=(main_row(D_MODEL), prev_row(ATTN_WIDTH), prev_row(KV_WIDTH), prev_row(KV_WIDTH),
                   prev_row(POOL_WIDTH), last(WINDOW, KV_WIDTH), last(WINDOW, KV_WIDTH), last(HIST, POOL_WIDTH),
                   srow(D_MODEL), srow(ATTN_WIDTH), srow(KV_WIDTH), srow(KV_WIDTH), srow(POOL_WIDTH)),
        scratch_shapes=[pltpu.VMEM((tm, D_MODEL), BF16), pltpu.VMEM((tm, D_FF), BF16),
                        pltpu.VMEM((tm, D_MODEL), BF16),
                        pltpu.VMEM((rs, D_MODEL), BF16), pltpu.VMEM((rs, D_FF), BF16)],
        compiler_params=pltpu.CompilerParams(dimension_semantics=("arbitrary",), vmem_limit_bytes=VMEM_LIMIT),
        name=name,
    )(x, mod_all, xs, mod_all, gain, wg, wu, wd, win)


def _low_half():
    return lax.broadcasted_iota(jnp.int32, (WINDOW, LANES), 1) < HEAD_DIM


def _attn_probs(qb, kk, bias_ref, sinks_ref, layer, valid, p_ref):
    lo = _low_half()
    zero = jnp.zeros((WINDOW, LANES), BF16)
    pieces = []
    for c in range(4):
        qc = qb[:, c * LANES:(c + 1) * LANES]
        pieces += [jnp.where(lo, qc, zero), jnp.where(lo, zero, qc)]
    qs = jnp.concatenate(pieces, axis=0)
    s = lax.dot_general(qs, kk, (((1,), (1,)), ((), ())), preferred_element_type=F32)
    for r in range(N_HEADS):
        rows = slice(r * WINDOW, (r + 1) * WINDOW)
        sr = jnp.where(valid, s[rows] + bias_ref[rows, :], NEG_INF)
        sink = sinks_ref[layer, _head_of_row(r)]
        m = jnp.maximum(jnp.max(sr, axis=-1, keepdims=True), sink)
        p = jnp.exp(sr - m)
        denom = jnp.sum(p, axis=-1, keepdims=True) + jnp.exp(sink - m)
        p_ref[rows, :] = (p / denom).astype(BF16)


def _attn_values(p_ref, vv):
    lo = _low_half()
    o = jnp.dot(p_ref[...], vv, preferred_element_type=F32)
    cols = [jnp.where(lo, o[(2 * c) * WINDOW:(2 * c + 1) * WINDOW], o[(2 * c + 1) * WINDOW:(2 * c + 2) * WINDOW])
            for c in range(4)]
    return jnp.concatenate(cols, axis=1)


def _pool_sums(u_ref, up_ref, uext_ref, t1_ref, t2_ref, pooled_ref, seq_tile):
    tm = u_ref.shape[0]
    g_w = POOL_GROUP_WIDTH
    top = 2 * HIST
    end = top + tm
    s1, s2, s3 = SUBLANES, 2 * SUBLANES, 3 * SUBLANES
    assert POOL_SIZES == (2, 4, 8, 16) and HIST == 2 * SUBLANES and top == 4 * SUBLANES
    uext_ref[0:HIST, :] = jnp.zeros((HIST, POOL_WIDTH), F32)
    uext_ref[HIST:top, :] = jnp.where(seq_tile == 0, 0.0, up_ref[...])
    uext_ref[top:, :] = u_ref[...]
    t1_ref[s1:end, :] = uext_ref[s1:end, :] + uext_ref[s1 - 1:end - 1, :]
    t2_ref[s2:end, :] = t1_ref[s2:end, g_w:] + t1_ref[s2 - 2:end - 2, g_w:]
    t1_ref[s3:end, 2 * g_w:] = t2_ref[s3:end, g_w:] + t2_ref[s3 - 4:end - 4, g_w:]
    sums = [t1_ref[top:, 0:g_w], t2_ref[top:, 0:g_w], t1_ref[top:, 2 * g_w:3 * g_w],
            t1_ref[top:, 3 * g_w:] + t1_ref[top - 8:end - 8, 3 * g_w:]]
    pos = seq_tile * tm + lax.broadcasted_iota(jnp.int32, (tm, g_w), 0)
    for g, w in enumerate(POOL_SIZES):
        cols = slice(g * g_w, (g + 1) * g_w)
        count = jnp.minimum(pos + 1, w).astype(F32)
        pooled_ref[:, cols] = (sums[g] / count - uext_ref[top:, cols]).astype(BF16)


def _mix_prompt_steps(q_ref, k_ref, v_ref, kp_ref, vp_ref, bias_ref, sinks_ref, poolw_ref,
                      pscale_ref, mixed_ref, pooled_ref, p_ref, layer, seq_tile):
    tm = q_ref.shape[0]
    nb = tm // WINDOW
    first = seq_tile == 0

    def keys_values(ref, prev_ref, j):
        prev = prev_ref[...] if j == 0 else ref[(j - 1) * WINDOW:j * WINDOW, :]
        return jnp.concatenate([prev, ref[j * WINDOW:(j + 1) * WINDOW, :]], axis=0).astype(BF16)

    def probs(j):
        qi = lax.broadcasted_iota(jnp.int32, (WINDOW, 2 * WINDOW), 0)
        kj = lax.broadcasted_iota(jnp.int32, (WINDOW, 2 * WINDOW), 1)
        dist = qi + WINDOW - kj
        valid = (dist >= 0) & (dist <= WINDOW)
        if j == 0:
            valid = valid & (kj >= jnp.where(first, WINDOW, 0))
        _attn_probs(q_ref[j * WINDOW:(j + 1) * WINDOW, :], keys_values(k_ref, kp_ref, j), bias_ref, sinks_ref,
                    layer, valid, p_ref)

    def values(j):
        a = _attn_values(p_ref, keys_values(v_ref, vp_ref, j))
        mixed_ref[j * WINDOW:(j + 1) * WINDOW, 0:ATTN_WIDTH] = a.astype(BF16)

    def pool_map(g):
        cols = slice(g * POOL_GROUP_WIDTH, (g + 1) * POOL_GROUP_WIDTH)
        pg = jnp.dot(pooled_ref[:, cols], poolw_ref[g], preferred_element_type=F32) * pscale_ref[layer:layer + 1, cols]
        mixed_ref[:, ATTN_WIDTH + g * POOL_GROUP_WIDTH:ATTN_WIDTH + (g + 1) * POOL_GROUP_WIDTH] = pg.astype(BF16)

    assert nb == len(POOL_SIZES)

    def stage(i):
        def run():
            if i > 0:
                values(i - 1)
                pool_map(i - 1)
            if i < nb:
                probs(i)
        return run

    return [stage(i) for i in range(nb + 1)]


def _out_projection(x1, mixed_ref, wout_ref, mod, gain_ref, layer):
    x2 = x1 + mod(5) * _dot_by_row_halves(mixed_ref, wout_ref[...])
    return x2, _rms_mod(x2, gain_ref[layer, 2:3, :], mod(6), mod(7)).astype(BF16)


def _ffn2_out(x2, y, mod, fgain_ref, final):
    x3 = x2 + 0.5 * mod(8) * y
    if final:
        x3 = x3 * lax.rsqrt(jnp.mean(x3 * x3, axis=-1, keepdims=True) + EPS) * fgain_ref[...]
    return x3


def _kb_kernel(sinks_ref, x1_ref, q_ref, k_ref, v_ref, kp_ref, vp_ref, u_ref, up_ref, un_ref, upn_ref,
               mod_ref, x1s_ref, attns_ref, pooleds_ref, mods_ref,
               gain_ref, bias_ref, poolw_ref, pscale_ref, wout_ref, wg_ref, wu_ref, wd_ref, fgain_ref,
               o_ref, os_ref, mixed_ref, h_ref, x2_ref, act_ref, uext_ref, t1_ref, t2_ref, pooled_ref, p_ref,
               *, layer, final, tiles_per_seq):
    s = pl.program_id(0)
    n_tiles = pl.num_programs(0) - 2

    @pl.when(s == n_tiles + 1)
    def _():
        rs = x1s_ref.shape[0]
        mixed_s, h_s, act_s = (ref.at[pl.ds(0, rs), :] for ref in (mixed_ref, h_ref, act_ref))
        mixed_s[:, 0:ATTN_WIDTH] = attns_ref[...]
        for g in range(len(POOL_SIZES)):
            cols = slice(g * POOL_GROUP_WIDTH, (g + 1) * POOL_GROUP_WIDTH)
            pg = (jnp.dot(pooleds_ref[:, cols], poolw_ref[g], preferred_element_type=F32)
                  * pscale_ref[layer:layer + 1, cols])
            mixed_s[:, ATTN_WIDTH + g * POOL_GROUP_WIDTH:ATTN_WIDTH + (g + 1) * POOL_GROUP_WIDTH] = pg.astype(BF16)
        mod = _mod_getter(mods_ref, first=TAIL_MOD_FIRST)
        x2, h = _out_projection(x1s_ref[...], mixed_s, wout_ref, mod, gain_ref, layer)
        h_s[...] = h
        os_ref[...] = _ffn2_out(x2, _swiglu(h_s, act_s, wg_ref, wu_ref, wd_ref), mod, fgain_ref, final)
    mix_tile = jnp.minimum(s, n_tiles - 1)
    next_tile = jnp.minimum(s + 1, n_tiles - 1)
    ffn_tile = jnp.maximum(s - 1, 0)
    slot = s % 2

    mix_steps = _mix_prompt_steps(q_ref, k_ref, v_ref, kp_ref, vp_ref, bias_ref, sinks_ref, poolw_ref, pscale_ref,
                                  mixed_ref, pooled_ref, p_ref, layer, mix_tile % tiles_per_seq)

    mix_mod = _mod_getter(mod_ref, mix_tile // tiles_per_seq)
    last_mix_step = mix_steps[-1]

    def finish_mix():
        last_mix_step()
        x2_ref[slot] = x1_ref[...] + mix_mod(5) * _dot_by_row_halves(mixed_ref, wout_ref[...])

    def norm_and_pool_ahead():
        h_ref[...] = _rms_mod(x2_ref[slot], gain_ref[layer, 2:3, :], mix_mod(6), mix_mod(7)).astype(BF16)
        _pool_sums(un_ref, upn_ref, uext_ref, t1_ref, t2_ref, pooled_ref, next_tile % tiles_per_seq)

    def ffn2(**mixing):
        y = _swiglu(h_ref, act_ref, wg_ref, wu_ref, wd_ref, **mixing)
        o_ref[...] = _ffn2_out(x2_ref[1 - slot], y, _mod_getter(mod_ref, ffn_tile // tiles_per_seq), fgain_ref, final)

    @pl.when(s == 0)
    def _():
        _pool_sums(u_ref, up_ref, uext_ref, t1_ref, t2_ref, pooled_ref, 0)
        for step in mix_steps[:-1]:
            step()
        finish_mix()
        norm_and_pool_ahead()

    @pl.when(jnp.logical_and(s > 0, s < n_tiles))
    def _():
        ffn2(side_work=mix_steps[:-1] + [finish_mix], before_down=norm_and_pool_ahead)

    @pl.when(s == n_tiles)
    def _():
        ffn2()


def _kb_call(sinks, x1, q, k, v, u, x1s, attn_s, pooled_s, mod_all, mod_spec, mods_tail, gain, bias, poolw, pscale,
             wout, wg, wu, wd, fgain, layer, final, name):
    mods_spec = pl.BlockSpec((None,) + mods_tail.shape[1:], lambda s: (layer, 0, 0), pipeline_mode=pl.Buffered(1))
    g, r, _ = x1.shape
    rs = x1s.shape[0]
    tm = TM_PROMPT
    tps = r // tm
    n_tiles = g * tps
    nb = tm // WINDOW
    nh = tm // HIST
    mix = lambda s: jnp.minimum(s, n_tiles - 1)
    tail = lambda s: jnp.clip(s - 1, 0, n_tiles - 1)
    mix_row = lambda w: pl.BlockSpec((None, tm, w), lambda s: (mix(s) // tps, mix(s) % tps, 0))
    tail_row = lambda w: pl.BlockSpec((None, tm, w), lambda s: (tail(s) // tps, tail(s) % tps, 0))
    prev_kv = pl.BlockSpec((None, WINDOW, KV_WIDTH),
                           lambda s: (mix(s) // tps, jnp.maximum((mix(s) % tps) * nb - 1, 0), 0))
    nxt = lambda s: jnp.minimum(s + 1, n_tiles - 1)
    first_u = pl.BlockSpec((None, tm, POOL_WIDTH), lambda s: (0, 0, 0))
    first_hist = pl.BlockSpec((None, HIST, POOL_WIDTH), lambda s: (0, 0, 0))
    next_u = pl.BlockSpec((None, tm, POOL_WIDTH), lambda s: (nxt(s) // tps, nxt(s) % tps, 0))
    next_hist = pl.BlockSpec((None, HIST, POOL_WIDTH),
                             lambda s: (nxt(s) // tps, jnp.maximum((nxt(s) % tps) * nh - 1, 0), 0))
    return pl.pallas_call(
        functools.partial(_kb_kernel, layer=layer, final=final, tiles_per_seq=tps),
        out_shape=(jax.ShapeDtypeStruct((g, r, D_MODEL), F32), jax.ShapeDtypeStruct((rs, D_MODEL), F32)),
        grid=(n_tiles + 2,),
        in_specs=[_SMEM, mix_row(D_MODEL), mix_row(ATTN_WIDTH), mix_row(KV_WIDTH), mix_row(KV_WIDTH),
                  prev_kv, prev_kv, first_u, first_hist, next_u, next_hist, mod_spec,
                  _full((rs, D_MODEL)), _full((rs, ATTN_WIDTH)), _full((rs, POOL_WIDTH)), mods_spec,
                  _full((DEPTH, 3, D_MODEL)),
                  _full((N_HEADS * WINDOW, 2 * WINDOW)),
                  _layer_resident((len(POOL_SIZES), POOL_GROUP_WIDTH, POOL_GROUP_WIDTH), layer),
                  _full((DEPTH, POOL_WIDTH)),
                  _layer_resident((D_MODEL, D_MODEL), layer), _layer_resident((D_MODEL, D_FF), layer),
                  _layer_resident((D_MODEL, D_FF), layer), _layer_resident((D_FF, D_MODEL), layer),
                  _full((1, D_MODEL))],
        out_specs=(tail_row(D_MODEL), _full((rs, D_MODEL))),
        scratch_shapes=[pltpu.VMEM((tm, D_MODEL), BF16), pltpu.VMEM((tm, D_MODEL), BF16),
                        pltpu.VMEM((2, tm, D_MODEL), F32), pltpu.VMEM((tm, D_FF), BF16),
                        pltpu.VMEM((2 * HIST + tm, POOL_WIDTH), F32), pltpu.VMEM((2 * HIST + tm, POOL_WIDTH), F32),
                        pltpu.VMEM((2 * HIST + tm, POOL_WIDTH - POOL_GROUP_WIDTH), F32),
                        pltpu.VMEM((tm, POOL_WIDTH), BF16),
                        pltpu.VMEM((N_HEADS * WINDOW, 2 * WINDOW), BF16)],
        compiler_params=pltpu.CompilerParams(
            dimension_semantics=("arbitrary",), vmem_limit_bytes=VMEM_LIMIT),
        name=name,
    )(sinks, x1, q, k, v, k, v, u, u, u, u, mod_all, x1s, attn_s, pooled_s, mods_tail,
      gain, bias, poolw, pscale, wout, wg, wu, wd, fgain)


def _ks_kernel(sinks_ref, q_ref, kn_ref, vn_ref, u_ref, ck_ref, cv_ref, hist_ref, bias_ref,
               o_ref, pooled_ref, nk_ref, nv_ref, newhist_ref,
               qs_ref, knr_ref, vnr_ref, s_ref, p_ref, acc_ref, *, layer):
    bt = q_ref.shape[0]
    for b in range(bt):
        for cache_ref, new_ref, out_ref in ((ck_ref, kn_ref, nk_ref), (cv_ref, vn_ref, nv_ref)):
            out_ref[b, 0:WINDOW - 1, :] = cache_ref[b, 1:WINDOW, :]
            out_ref[b, WINDOW - 1:WINDOW, :] = new_ref[b:b + 1, :]

    for r in range(POOL_STATE - 1):
        newhist_ref[:, r, :] = hist_ref[:, r + 1, :]
    newhist_ref[:, POOL_STATE - 1, :] = u_ref[...]
    for g, w in enumerate(POOL_SIZES):
        cols = slice(g * POOL_GROUP_WIDTH, (g + 1) * POOL_GROUP_WIDTH)
        ug = u_ref[:, cols]
        acc = ug
        for d in range(1, w):
            acc = acc + hist_ref[:, POOL_STATE - d, cols]
        pooled_ref[:, cols] = (acc / float(w) - ug).astype(BF16)

    rows = bt * N_HEADS
    nt = (((1,), (1,)), ((), ()))
    head_rows = lambda r: pl.ds(r, bt, stride=N_HEADS)
    lo = lax.broadcasted_iota(jnp.int32, (bt, LANES), 1) < HEAD_DIM
    kn = kn_ref[...].astype(BF16).astype(F32)
    vn = vn_ref[...].astype(BF16).astype(F32)
    for c in range(4):
        qc = q_ref[:, c * LANES:(c + 1) * LANES].astype(F32)
        qs_ref[head_rows(2 * c), :] = jnp.where(lo, qc, 0.0)
        qs_ref[head_rows(2 * c + 1), :] = jnp.where(lo, 0.0, qc)
    for r in range(N_HEADS):
        knr_ref[head_rows(r), :] = kn
        vnr_ref[head_rows(r), :] = vn

    for b in range(bt):
        rb = slice(b * N_HEADS, (b + 1) * N_HEADS)
        s_ref[rb, :] = lax.dot_general(qs_ref[rb, :].astype(BF16), ck_ref[b].astype(BF16), nt,
                                       preferred_element_type=F32)

    head = jnp.bitwise_and(lax.broadcasted_iota(jnp.int32, (rows, 1), 0), N_HEADS - 1)
    sink = jnp.zeros((rows, 1), F32)
    for r in range(N_HEADS):
        sink = jnp.where(head == r, sinks_ref[layer, _head_of_row(r)], sink)
    bias = jnp.concatenate([bias_ref[...]] * bt, axis=0)
    s_c = s_ref[...] + bias[:, 0:WINDOW]
    s_n = jnp.sum(qs_ref[...] * knr_ref[...], axis=-1, keepdims=True) + bias[:, WINDOW:WINDOW + 1]
    m = jnp.maximum(jnp.maximum(jnp.max(s_c, axis=-1, keepdims=True), s_n), sink)
    p_c = jnp.exp(s_c - m)
    p_n = jnp.exp(s_n - m)
    denom = jnp.sum(p_c, axis=-1, keepdims=True) + p_n + jnp.exp(sink - m)
    p_ref[...] = p_c / denom

    for b in range(bt):
        rb = slice(b * N_HEADS, (b + 1) * N_HEADS)
        acc_ref[rb, :] = jnp.dot(p_ref[rb, :].astype(BF16), cv_ref[b].astype(BF16), preferred_element_type=F32)
    acc_ref[...] = acc_ref[...] + (p_n / denom).astype(BF16).astype(F32) * vnr_ref[...]

    for c in range(4):
        o_ref[:, c * LANES:(c + 1) * LANES] = jnp.where(
            lo, acc_ref[head_rows(2 * c), :], acc_ref[head_rows(2 * c + 1), :]).astype(BF16)


def _ks_call(sinks, q, kn, vn, u, cache_k, cache_v, state_pool, layer, bias_s):
    nb = q.shape[0]
    bt = SAMPLE_SEQS_PER_STEP
    rows = bt * N_HEADS
    per_b = lambda w: pl.BlockSpec((bt, w), lambda i: (i, 0))
    cache = pl.BlockSpec((None, bt, WINDOW, KV_WIDTH), lambda i: (layer, i, 0, 0))
    new_cache = pl.BlockSpec((bt, WINDOW, KV_WIDTH), lambda i: (i, 0, 0))
    hist = pl.BlockSpec((None, bt, POOL_STATE, POOL_WIDTH), lambda i: (layer, i, 0, 0))
    new_hist = pl.BlockSpec((bt, POOL_STATE, POOL_WIDTH), lambda i: (i, 0, 0))
    return pl.pallas_call(
        functools.partial(_ks_kernel, layer=layer),
        out_shape=(jax.ShapeDtypeStruct((nb, ATTN_WIDTH), BF16),
                   jax.ShapeDtypeStruct((nb, POOL_WIDTH), BF16),
                   jax.ShapeDtypeStruct((nb, WINDOW, KV_WIDTH), F32),
                   jax.ShapeDtypeStruct((nb, WINDOW, KV_WIDTH), F32),
                   jax.ShapeDtypeStruct((nb, POOL_STATE, POOL_WIDTH), F32)),
        grid=(nb // bt,),
        in_specs=[_SMEM, per_b(ATTN_WIDTH), per_b(KV_WIDTH), per_b(KV_WIDTH), per_b(POOL_WIDTH), cache, cache,
                  hist, _full((N_HEADS, 2 * WINDOW))],
        out_specs=(per_b(ATTN_WIDTH), per_b(POOL_WIDTH), new_cache, new_cache, new_hist),
        scratch_shapes=[pltpu.VMEM((rows, KV_WIDTH), F32), pltpu.VMEM((rows, KV_WIDTH), F32),
                        pltpu.VMEM((rows, KV_WIDTH), F32), pltpu.VMEM((rows, WINDOW), F32),
                        pltpu.VMEM((rows, WINDOW), F32), pltpu.VMEM((rows, KV_WIDTH), F32)],
        compiler_params=pltpu.CompilerParams(dimension_semantics=("arbitrary",), vmem_limit_bytes=VMEM_LIMIT),
        name="sample_mixers",
    )(sinks, q, kn, vn, u, cache_k, cache_v, state_pool, bias_s)


def kernel(x_prompt, x_sample, c_prompt, c_sample, cache_k, cache_v, state_pool, w_ada, b_ada, norm_gain,
           w_in, w_out, sinks, rel_bias, pool_w, pool_scale, ffn1_wg, ffn1_wu, ffn1_wd, ffn2_wg, ffn2_wu,
           ffn2_wd, final_gain):
    n_p, seq, _ = x_prompt.shape
    n_s = x_sample.shape[0]

    mod_all, mods_tail = _modulation(jnp.concatenate([c_sample, c_prompt], axis=0), w_ada, b_ada, n_s)
    bias = _bias_table(rel_bias)
    bias_s = bias.reshape(N_HEADS, WINDOW, 2 * WINDOW)[:, 0, :]
    fgain = final_gain.reshape(1, D_MODEL)

    wq = (w_in[:, :, :ATTN_WIDTH].reshape(DEPTH, D_MODEL, 2, 4, HEAD_DIM).transpose(0, 1, 3, 2, 4)
          .reshape(DEPTH, D_MODEL, ATTN_WIDTH)) * (HEAD_DIM ** -0.5)
    win = jnp.concatenate([wq, w_in[:, :, ATTN_WIDTH:]], axis=2).astype(BF16)
    wo_attn = (w_out[:, :ATTN_WIDTH].reshape(DEPTH, 2, 4, HEAD_DIM, D_MODEL).transpose(0, 2, 1, 3, 4)
               .reshape(DEPTH, ATTN_WIDTH, D_MODEL))
    wout = jnp.concatenate([wo_attn, w_out[:, ATTN_WIDTH:]], axis=1).astype(BF16)
    wg1, wu1, wd1 = ffn1_wg.astype(BF16), ffn1_wu.astype(BF16), ffn1_wd.astype(BF16)
    wg2, wu2, wd2 = ffn2_wg.astype(BF16), ffn2_wu.astype(BF16), ffn2_wd.astype(BF16)
    poolw = pool_w.astype(BF16)

    xp = x_prompt
    xs = x_sample.reshape(n_s, D_MODEL)
    ck = cache_k.reshape(DEPTH, n_s, WINDOW, KV_WIDTH)
    cv = cache_v.reshape(DEPTH, n_s, WINDOW, KV_WIDTH)

    new_kp, new_vp, new_pp, new_ks, new_vs, new_ps = [], [], [], [], [], []
    for l in range(DEPTH):
        final = l == DEPTH - 1
        mod_p = pl.BlockSpec((None, n_p, MOD_WIDTH), lambda *_, l=l: (l, n_s // n_p, 0))
        mod_s = pl.BlockSpec((None, n_s, MOD_WIDTH), lambda *_, l=l: (l, 0, 0), pipeline_mode=pl.Buffered(1))

        x1, q, k, v, u, k_last, v_last, u_last, x1s, qs, ks, vs, us = _ka_call(
            xp, xs, mod_all, mod_p, mod_s, norm_gain, wg1, wu1, wd1, win, l, f"ffn1_inproj_l{l}")
        attn_s, pooled_s, ck_next, cv_next, pool_next = _ks_call(sinks, qs, ks, vs, us, ck, cv, state_pool, l, bias_s)
        xp, xs = _kb_call(sinks, x1, q, k, v, u, x1s, attn_s, pooled_s, mod_all, mod_p, mods_tail, norm_gain, bias,
                          poolw, pool_scale, wout, wg2, wu2, wd2, fgain, l, final, f"mixer_ffn2_l{l}")
        new_kp.append(k_last)
        new_vp.append(v_last)
        new_pp.append(u_last)
        new_ks.append(ck_next)
        new_vs.append(cv_next)
        new_ps.append(pool_next)

    kv_prompt = (DEPTH, n_p, WINDOW, N_KV_HEADS, HEAD_DIM)
    kv_sample = (DEPTH, n_s, WINDOW, N_KV_HEADS, HEAD_DIM)
    return (xp, xs.reshape(n_s, 1, D_MODEL), jnp.stack(new_kp).reshape(kv_prompt), jnp.stack(new_vp).reshape(kv_prompt),
            jnp.stack(new_pp)[:, :, HIST - POOL_STATE:], jnp.stack(new_ks).reshape(kv_sample),
            jnp.stack(new_vs).reshape(kv_sample), jnp.stack(new_ps))
```

```python
import functools
import math

import numpy as np
import jax
import jax.numpy as jnp
from jax import lax
from jax.experimental import pallas as pl
from jax.experimental.pallas import tpu as pltpu

D_MODEL = 1024
DEPTH = 2
HEAD_DIM = 64
N_HEADS = 8
N_KV_HEADS = 2
ATTN_WIDTH = N_HEADS * HEAD_DIM
KV_WIDTH = N_KV_HEADS * HEAD_DIM
WINDOW = 128
POOL_SIZES = (2, 4, 8, 16)
POOL_GROUP_WIDTH = 128
POOL_WIDTH = len(POOL_SIZES) * POOL_GROUP_WIDTH
POOL_STATE = max(POOL_SIZES) - 1
IN_WIDTH = ATTN_WIDTH + 2 * KV_WIDTH + POOL_WIDTH
D_FF = 2816
N_BUCKETS = 32
MAX_DISTANCE = 128
N_MOD = 9
MOD_WIDTH = N_MOD * D_MODEL
TAIL_MOD_FIRST = 5
MOD_CHUNK = 3 * D_MODEL
EPS = 1e-6

LANES = 128
SUBLANES = 8
MXU_COLS = 256
FF_CHUNK = 2 * MXU_COLS
FF_CHUNKS = tuple((c, min(FF_CHUNK, D_FF - c)) for c in range(0, D_FF, FF_CHUNK))
TM_PROMPT = 512
SAMPLE_SEQS_PER_STEP = 32
HIST = 16
VMEM_LIMIT = 56 * 1024 * 1024

F32 = jnp.float32
BF16 = jnp.bfloat16
NEG_INF = float("-inf")


def _head_of_row(r):
    return (r // 2) + 4 * (r % 2)


def _t5_bucket_np(dist):
    n = np.maximum(dist, 0)
    max_exact = N_BUCKETS // 2
    nf = np.maximum(n, 1).astype(np.float32)
    large = max_exact + (np.log(nf / max_exact) / math.log(MAX_DISTANCE / max_exact)
                         * (N_BUCKETS - max_exact)).astype(np.int32)
    large = np.minimum(large, N_BUCKETS - 1)
    return np.where(n < max_exact, n, large).astype(np.int32)


def _bucket_table():
    dist = np.arange(WINDOW)[:, None] + WINDOW - np.arange(2 * WINDOW)[None, :]
    return _t5_bucket_np(dist)


def _full(shape):
    return pl.BlockSpec(shape, lambda *_: (0,) * len(shape))


def _layer_resident(shape, layer):
    return pl.BlockSpec((None,) + tuple(shape), lambda *_: (layer,) + (0,) * len(shape),
                        pipeline_mode=pl.Buffered(1))


_SMEM = pl.BlockSpec(memory_space=pltpu.SMEM)


def _rms_mod(x, gain, shift, scale):
    y = x * lax.rsqrt(jnp.mean(x * x, axis=-1, keepdims=True) + EPS)
    return (y * gain) * (1.0 + scale) + shift


def _dot_by_row_halves(a_ref, w):
    half = a_ref.shape[0] // 2
    if half % (2 * SUBLANES) or half < MXU_COLS:
        return jnp.dot(a_ref[...], w, preferred_element_type=F32)
    return jnp.concatenate([jnp.dot(a_ref[:half, :], w, preferred_element_type=F32),
                            jnp.dot(a_ref[half:, :], w, preferred_element_type=F32)], axis=0)


def _swiglu(h_ref, act_ref, wg_ref, wu_ref, wd_ref, side_work=(), before_down=None):
    n = len(FF_CHUNKS)
    assert len(side_work) <= n
    after_chunk = {((j + 1) * n) // (len(side_work) + 1) - 1: work for j, work in enumerate(side_work)}
    chunk_dot = ((lambda a_ref, w: jnp.dot(a_ref[...], w, preferred_element_type=F32)) if side_work
                 else _dot_by_row_halves)
    for i, (c0, cw) in enumerate(FF_CHUNKS):
        g = chunk_dot(h_ref, wg_ref[:, c0:c0 + cw])
        u = chunk_dot(h_ref, wu_ref[:, c0:c0 + cw])
        half_g = 0.5 * g
        act_ref[:, c0:c0 + cw] = ((half_g * (1.0 + jnp.tanh(half_g))) * u).astype(BF16)
        if i in after_chunk:
            after_chunk[i]()
    if before_down is not None:
        before_down()
    return _dot_by_row_halves(act_ref, wd_ref[...])


def _mod_getter(mod_ref, row=None, first=0):
    rows = slice(None) if row is None else pl.ds(row, 1)
    return lambda k: mod_ref[rows, (k - first) * D_MODEL:(k - first + 1) * D_MODEL]


def _mod_kernel(c_ref, w_ref, b_ref, o_ref, tail_ref):
    c = c_ref[...]
    a = (c / (1.0 + jnp.exp(-c))).astype(BF16)
    m = jnp.dot(a, w_ref[...].astype(BF16), preferred_element_type=F32) + b_ref[...]
    o_ref[...] = m
    rs = tail_ref.shape[0]
    j = pl.program_id(1)
    first_chunk, first_col = divmod(TAIL_MOD_FIRST * D_MODEL, MOD_CHUNK)
    assert first_chunk == MOD_WIDTH // MOD_CHUNK - 2

    @pl.when(j == first_chunk)
    def _():
        tail_ref[:, 0:MOD_CHUNK - first_col] = m[0:rs, first_col:]

    @pl.when(j == first_chunk + 1)
    def _():
        tail_ref[:, MOD_CHUNK - first_col:] = m[0:rs, :]


def _modulation(c_all, w_ada, b_ada, tail_rows):
    rows = c_all.shape[0]
    tail_width = MOD_WIDTH - TAIL_MOD_FIRST * D_MODEL
    return pl.pallas_call(
        _mod_kernel,
        out_shape=(jax.ShapeDtypeStruct((DEPTH, rows, MOD_WIDTH), F32),
                   jax.ShapeDtypeStruct((DEPTH, tail_rows, tail_width), F32)),
        grid=(DEPTH, MOD_WIDTH // MOD_CHUNK),
        in_specs=[
            pl.BlockSpec((rows, D_MODEL), lambda l, j: (0, 0)),
            pl.BlockSpec((None, D_MODEL, MOD_CHUNK), lambda l, j: (l, 0, j)),
            pl.BlockSpec((None, 1, MOD_CHUNK), lambda l, j: (l, 0, j)),
        ],
        out_specs=(pl.BlockSpec((None, rows, MOD_CHUNK), lambda l, j: (l, 0, j)),
                   pl.BlockSpec((None, tail_rows, tail_width), lambda l, j: (l, 0, 0))),
        compiler_params=pltpu.CompilerParams(
            dimension_semantics=("arbitrary", "arbitrary"), vmem_limit_bytes=VMEM_LIMIT),
        name="adaln_modulation",
    )(c_all, w_ada, b_ada.reshape(DEPTH, 1, MOD_WIDTH))


def _bias_kernel(rb_ref, bucket_ref, o_ref):
    bucket = bucket_ref[...]
    eq = [bucket == b for b in range(N_BUCKETS)]
    for r in range(N_HEADS):
        acc = jnp.zeros((WINDOW, 2 * WINDOW), F32)
        for b in range(N_BUCKETS):
            acc = jnp.where(eq[b], rb_ref[b, _head_of_row(r)], acc)
        o_ref[r * WINDOW:(r + 1) * WINDOW, :] = acc


def _bias_table(rel_bias):
    return pl.pallas_call(
        _bias_kernel,
        out_shape=jax.ShapeDtypeStruct((N_HEADS * WINDOW, 2 * WINDOW), F32),
        in_specs=[_SMEM, pl.BlockSpec(memory_space=pltpu.VMEM)],
        out_specs=pl.BlockSpec(memory_space=pltpu.VMEM),
        name="rel_bias_table",
    )(rel_bias, jnp.asarray(_bucket_table()))


def _project_q(hm_ref, win_ref, q_ref):
    q_ref[...] = jnp.dot(hm_ref[...], win_ref[:, :ATTN_WIDTH], preferred_element_type=F32).astype(BF16)


def _project_kvu(hm_ref, win_ref, k_ref, v_ref, u_ref):
    z = jnp.dot(hm_ref[...], win_ref[:, ATTN_WIDTH:], preferred_element_type=F32)
    k_ref[...] = z[:, :KV_WIDTH]
    v_ref[...] = z[:, KV_WIDTH:2 * KV_WIDTH]
    u_ref[...] = z[:, 2 * KV_WIDTH:]
    return z


def _ffn1(x_ref, mod, gain_ref, wg_ref, wu_ref, wd_ref, x1_ref, h_ref, act_ref, layer, before_norm=None):
    x = x_ref[...]
    h_ref[...] = _rms_mod(x, gain_ref[layer, 0:1, :], mod(0), mod(1)).astype(BF16)
    x1 = x + 0.5 * mod(2) * _swiglu(h_ref, act_ref, wg_ref, wu_ref, wd_ref)
    x1_ref[...] = x1
    if before_norm is not None:
        before_norm()
    return _rms_mod(x1, gain_ref[layer, 1:2, :], mod(3), mod(4)).astype(BF16)


def _ka_kernel(x_ref, mod_ref, xs_ref, mods_ref, gain_ref, wg_ref, wu_ref, wd_ref, win_ref,
               x1_ref, q_ref, k_ref, v_ref, u_ref, kl_ref, vl_ref, ul_ref,
               x1s_ref, qs_ref, ks_ref, vs_ref, us_ref,
               h_ref, act_ref, hm_ref, hs_ref, acts_ref, *, layer, tiles_per_seq):
    tm = x_ref.shape[0]
    s = pl.program_id(0)
    n_tiles = pl.num_programs(0) - 2

    @pl.when(s == n_tiles + 1)
    def _():
        hs_ref[...] = _ffn1(xs_ref, _mod_getter(mods_ref), gain_ref, wg_ref, wu_ref, wd_ref, x1s_ref, hs_ref,
                            acts_ref, layer)
        _project_q(hs_ref, win_ref, qs_ref)
        _project_kvu(hs_ref, win_ref, ks_ref, vs_ref, us_ref)

    @pl.when(s == 0)
    def _():
        hm_ref[...] = jnp.zeros_like(hm_ref)

    def project_rest():
        z = _project_kvu(hm_ref, win_ref, k_ref, v_ref, u_ref)
        kl_ref[...] = z[tm - WINDOW:, :KV_WIDTH]
        vl_ref[...] = z[tm - WINDOW:, KV_WIDTH:2 * KV_WIDTH]
        ul_ref[...] = z[tm - HIST:, 2 * KV_WIDTH:]

    @pl.when(s < n_tiles)
    def _():
        _project_q(hm_ref, win_ref, q_ref)
        mod = _mod_getter(mod_ref, s // tiles_per_seq)
        hm = _ffn1(x_ref, mod, gain_ref, wg_ref, wu_ref, wd_ref, x1_ref, h_ref, act_ref, layer,
                   before_norm=project_rest)
        hm_ref[...] = hm

    @pl.when(s == n_tiles)
    def _():
        _project_q(hm_ref, win_ref, q_ref)
        project_rest()


def _ka_out_shapes(g, r):
    return (jax.ShapeDtypeStruct((g, r, D_MODEL), F32),
            jax.ShapeDtypeStruct((g, r, ATTN_WIDTH), BF16),
            jax.ShapeDtypeStruct((g, r, KV_WIDTH), F32),
            jax.ShapeDtypeStruct((g, r, KV_WIDTH), F32),
            jax.ShapeDtypeStruct((g, r, POOL_WIDTH), F32))


def _ka_weight_specs(layer):
    return [_full((DEPTH, 3, D_MODEL)),
            _layer_resident((D_MODEL, D_FF), layer), _layer_resident((D_MODEL, D_FF), layer),
            _layer_resident((D_FF, D_MODEL), layer), _layer_resident((D_MODEL, IN_WIDTH), layer)]


def _ka_call(x, xs, mod_all, mod_spec, mods_spec, gain, wg, wu, wd, win, layer, name):
    g, r, _ = x.shape
    rs = xs.shape[0]
    tm = TM_PROMPT
    tps = r // tm
    n_tiles = g * tps
    main = lambda s: jnp.minimum(s, n_tiles - 1)
    prev = lambda s: jnp.clip(s - 1, 0, n_tiles - 1)
    main_row = lambda w: pl.BlockSpec((None, tm, w), lambda s: (main(s) // tps, main(s) % tps, 0))
    prev_row = lambda w: pl.BlockSpec((None, tm, w), lambda s: (prev(s) // tps, prev(s) % tps, 0))
    last = lambda rows, w: pl.BlockSpec((None, rows, w), lambda s: (prev(s) // tps, 0, 0))
    srow = lambda w: _full((rs, w))
    sample_shapes = tuple(jax.ShapeDtypeStruct(sh.shape[1:], sh.dtype) for sh in _ka_out_shapes(1, rs))
    return pl.pallas_call(
        functools.partial(_ka_kernel, layer=layer, tiles_per_seq=tps),
        out_shape=_ka_out_shapes(g, r) + (jax.ShapeDtypeStruct((g, WINDOW, KV_WIDTH), F32),
                                          jax.ShapeDtypeStruct((g, WINDOW, KV_WIDTH), F32),
                                          jax.ShapeDtypeStruct((g, HIST, POOL_WIDTH), F32)) + sample_shapes,
        grid=(n_tiles + 2,),
        in_specs=[main_row(D_MODEL), mod_spec, srow(D_MODEL), mods_spec] + _ka_weight_specs(layer),
        out_specs=(main_row(D_MODEL), prev_row(ATTN_WIDTH), prev_row(KV_WIDTH), prev_row(KV_WIDTH),
                   prev_row(POOL_WIDTH), last(WINDOW, KV_WIDTH), last(WINDOW, KV_WIDTH), last(HIST, POOL_WIDTH),
                   srow(D_MODEL), srow(ATTN_WIDTH), srow(KV_WIDTH), srow(KV_WIDTH), srow(POOL_WIDTH)),
        scratch_shapes=[pltpu.VMEM((tm, D_MODEL), BF16), pltpu.VMEM((tm, D_FF), BF16),
                        pltpu.VMEM((tm, D_MODEL), BF16),
                        pltpu.VMEM((rs, D_MODEL), BF16), pltpu.VMEM((rs, D_FF), BF16)],
        compiler_params=pltpu.CompilerParams(dimension_semantics=("arbitrary",), vmem_limit_bytes=VMEM_LIMIT),
        name=name,
    )(x, mod_all, xs, mod_all, gain, wg, wu, wd, win)


def _low_half():
    return lax.broadcasted_iota(jnp.int32, (WINDOW, LANES), 1) < HEAD_DIM


def _attn_probs(qb, kk, bias_ref, sinks_ref, layer, valid, p_ref):
    lo = _low_half()
    zero = jnp.zeros((WINDOW, LANES), BF16)
    for c in range(4):
        qc = qb[:, c * LANES:(c + 1) * LANES]
        qs = jnp.concatenate([jnp.where(lo, qc, zero), jnp.where(lo, zero, qc)], axis=0)
        s = lax.dot_general(qs, kk, (((1,), (1,)), ((), ())), preferred_element_type=F32)
        for half in range(2):
            r = 2 * c + half
            rows = slice(r * WINDOW, (r + 1) * WINDOW)
            sr = jnp.where(valid, s[half * WINDOW:(half + 1) * WINDOW] + bias_ref[rows, :], NEG_INF)
            sink = sinks_ref[layer, _head_of_row(r)]
            m = jnp.maximum(jnp.max(sr, axis=-1, keepdims=True), sink)
            p = jnp.exp(sr - m)
            denom = jnp.sum(p, axis=-1, keepdims=True) + jnp.exp(sink - m)
            p_ref[rows, :] = (p / denom).astype(BF16)


def _attn_values(p_ref, vv):
    lo = _low_half()
    o = jnp.dot(p_ref[...], vv, preferred_element_type=F32)
    cols = [jnp.where(lo, o[(2 * c) * WINDOW:(2 * c + 1) * WINDOW], o[(2 * c + 1) * WINDOW:(2 * c + 2) * WINDOW])
            for c in range(4)]
    return jnp.concatenate(cols, axis=1)


def _pool_sums(u_ref, up_ref, uext_ref, t1_ref, t2_ref, pooled_ref, seq_tile):
    tm = u_ref.shape[0]
    g_w = POOL_GROUP_WIDTH
    top = 2 * HIST
    end = top + tm
    s1, s2, s3 = SUBLANES, 2 * SUBLANES, 3 * SUBLANES
    assert POOL_SIZES == (2, 4, 8, 16) and HIST == 2 * SUBLANES and top == 4 * SUBLANES
    uext_ref[0:HIST, :] = jnp.zeros((HIST, POOL_WIDTH), F32)
    uext_ref[HIST:top, :] = jnp.where(seq_tile == 0, 0.0, up_ref[...])
    uext_ref[top:, :] = u_ref[...]
    t1_ref[s1:end, :] = uext_ref[s1:end, :] + uext_ref[s1 - 1:end - 1, :]
    t2_ref[s2:end, :] = t1_ref[s2:end, g_w:] + t1_ref[s2 - 2:end - 2, g_w:]
    t1_ref[s3:end, 2 * g_w:] = t2_ref[s3:end, g_w:] + t2_ref[s3 - 4:end - 4, g_w:]
    sums = [t1_ref[top:, 0:g_w], t2_ref[top:, 0:g_w], t1_ref[top:, 2 * g_w:3 * g_w],
            t1_ref[top:, 3 * g_w:] + t1_ref[top - 8:end - 8, 3 * g_w:]]
    pos = seq_tile * tm + lax.broadcasted_iota(jnp.int32, (tm, g_w), 0)
    for g, w in enumerate(POOL_SIZES):
        cols = slice(g * g_w, (g + 1) * g_w)
        count = jnp.minimum(pos + 1, w).astype(F32)
        pooled_ref[:, cols] = (sums[g] / count - uext_ref[top:, cols]).astype(BF16)


def _mix_prompt_steps(q_ref, k_ref, v_ref, kp_ref, vp_ref, bias_ref, sinks_ref, poolw_ref,
                      pscale_ref, mixed_ref, pooled_ref, p_ref, layer, seq_tile):
    tm = q_ref.shape[0]
    nb = tm // WINDOW
    first = seq_tile == 0

    def keys_values(ref, prev_ref, j):
        prev = prev_ref[...] if j == 0 else ref[(j - 1) * WINDOW:j * WINDOW, :]
        return jnp.concatenate([prev, ref[j * WINDOW:(j + 1) * WINDOW, :]], axis=0).astype(BF16)

    def probs(j):
        qi = lax.broadcasted_iota(jnp.int32, (WINDOW, 2 * WINDOW), 0)
        kj = lax.broadcasted_iota(jnp.int32, (WINDOW, 2 * WINDOW), 1)
        dist = qi + WINDOW - kj
        valid = (dist >= 0) & (dist <= WINDOW)
        if j == 0:
            valid = valid & (kj >= jnp.where(first, WINDOW, 0))
        _attn_probs(q_ref[j * WINDOW:(j + 1) * WINDOW, :], keys_values(k_ref, kp_ref, j), bias_ref, sinks_ref,
                    layer, valid, p_ref)

    def values(j):
        a = _attn_values(p_ref, keys_values(v_ref, vp_ref, j))
        mixed_ref[j * WINDOW:(j + 1) * WINDOW, 0:ATTN_WIDTH] = a.astype(BF16)

    def pool_map(g):
        cols = slice(g * POOL_GROUP_WIDTH, (g + 1) * POOL_GROUP_WIDTH)
        pg = jnp.dot(pooled_ref[:, cols], poolw_ref[g], preferred_element_type=F32) * pscale_ref[layer:layer + 1, cols]
        mixed_ref[:, ATTN_WIDTH + g * POOL_GROUP_WIDTH:ATTN_WIDTH + (g + 1) * POOL_GROUP_WIDTH] = pg.astype(BF16)

    assert nb == len(POOL_SIZES)

    def stage(i):
        def run():
            if i > 0:
                values(i - 1)
                pool_map(i - 1)
            if i < nb:
                probs(i)
        return run

    return [stage(i) for i in range(nb + 1)]


def _out_projection(x1, mixed_ref, wout_ref, mod, gain_ref, layer):
    x2 = x1 + mod(5) * _dot_by_row_halves(mixed_ref, wout_ref[...])
    return x2, _rms_mod(x2, gain_ref[layer, 2:3, :], mod(6), mod(7)).astype(BF16)


def _ffn2_out(x2, y, mod, fgain_ref, final):
    x3 = x2 + 0.5 * mod(8) * y
    if final:
        x3 = x3 * lax.rsqrt(jnp.mean(x3 * x3, axis=-1, keepdims=True) + EPS) * fgain_ref[...]
    return x3


def _kb_kernel(sinks_ref, x1_ref, q_ref, k_ref, v_ref, kp_ref, vp_ref, u_ref, up_ref, un_ref, upn_ref,
               mod_ref, x1s_ref, attns_ref, pooleds_ref, mods_ref,
               gain_ref, bias_ref, poolw_ref, pscale_ref, wout_ref, wg_ref, wu_ref, wd_ref, fgain_ref,
               o_ref, os_ref, mixed_ref, h_ref, x2_ref, act_ref, uext_ref, t1_ref, t2_ref, pooled_ref, p_ref,
               *, layer, final, tiles_per_seq):
    s = pl.program_id(0)
    n_tiles = pl.num_programs(0) - 2

    @pl.when(s == n_tiles + 1)
    def _():
        rs = x1s_ref.shape[0]
        mixed_s, h_s, act_s = (ref.at[pl.ds(0, rs), :] for ref in (mixed_ref, h_ref, act_ref))
        mixed_s[:, 0:ATTN_WIDTH] = attns_ref[...]
        for g in range(len(POOL_SIZES)):
            cols = slice(g * POOL_GROUP_WIDTH, (g + 1) * POOL_GROUP_WIDTH)
            pg = (jnp.dot(pooleds_ref[:, cols], poolw_ref[g], preferred_element_type=F32)
                  * pscale_ref[layer:layer + 1, cols])
            mixed_s[:, ATTN_WIDTH + g * POOL_GROUP_WIDTH:ATTN_WIDTH + (g + 1) * POOL_GROUP_WIDTH] = pg.astype(BF16)
        mod = _mod_getter(mods_ref, first=TAIL_MOD_FIRST)
        x2, h = _out_projection(x1s_ref[...], mixed_s, wout_ref, mod, gain_ref, layer)
        h_s[...] = h
        os_ref[...] = _ffn2_out(x2, _swiglu(h_s, act_s, wg_ref, wu_ref, wd_ref), mod, fgain_ref, final)
    mix_tile = jnp.minimum(s, n_tiles - 1)
    next_tile = jnp.minimum(s + 1, n_tiles - 1)
    ffn_tile = jnp.maximum(s - 1, 0)
    slot = s % 2

    mix_steps = _mix_prompt_steps(q_ref, k_ref, v_ref, kp_ref, vp_ref, bias_ref, sinks_ref, poolw_ref, pscale_ref,
                                  mixed_ref, pooled_ref, p_ref, layer, mix_tile % tiles_per_seq)

    mix_mod = _mod_getter(mod_ref, mix_tile // tiles_per_seq)
    last_mix_step = mix_steps[-1]

    def finish_mix():
        last_mix_step()
        x2_ref[slot] = x1_ref[...] + mix_mod(5) * _dot_by_row_halves(mixed_ref, wout_ref[...])

    def norm_and_pool_ahead():
        h_ref[...] = _rms_mod(x2_ref[slot], gain_ref[layer, 2:3, :], mix_mod(6), mix_mod(7)).astype(BF16)
        _pool_sums(un_ref, upn_ref, uext_ref, t1_ref, t2_ref, pooled_ref, next_tile % tiles_per_seq)

    def ffn2(**mixing):
        y = _swiglu(h_ref, act_ref, wg_ref, wu_ref, wd_ref, **mixing)
        o_ref[...] = _ffn2_out(x2_ref[1 - slot], y, _mod_getter(mod_ref, ffn_tile // tiles_per_seq), fgain_ref, final)

    @pl.when(s == 0)
    def _():
        _pool_sums(u_ref, up_ref, uext_ref, t1_ref, t2_ref, pooled_ref, 0)
        for step in mix_steps[:-1]:
            step()
        finish_mix()
        norm_and_pool_ahead()

    @pl.when(jnp.logical_and(s > 0, s < n_tiles))
    def _():
        ffn2(side_work=mix_steps[:-1] + [finish_mix], before_down=norm_and_pool_ahead)

    @pl.when(s == n_tiles)
    def _():
        ffn2()


def _kb_call(sinks, x1, q, k, v, u, x1s, attn_s, pooled_s, mod_all, mod_spec, mods_tail, gain, bias, poolw, pscale,
             wout, wg, wu, wd, fgain, layer, final, name):
    mods_spec = pl.BlockSpec((None,) + mods_tail.shape[1:], lambda s: (layer, 0, 0), pipeline_mode=pl.Buffered(1))
    g, r, _ = x1.shape
    rs = x1s.shape[0]
    tm = TM_PROMPT
    tps = r // tm
    n_tiles = g * tps
    nb = tm // WINDOW
    nh = tm // HIST
    mix = lambda s: jnp.minimum(s, n_tiles - 1)
    tail = lambda s: jnp.clip(s - 1, 0, n_tiles - 1)
    mix_row = lambda w: pl.BlockSpec((None, tm, w), lambda s: (mix(s) // tps, mix(s) % tps, 0))
    tail_row = lambda w: pl.BlockSpec((None, tm, w), lambda s: (tail(s) // tps, tail(s) % tps, 0))
    prev_kv = pl.BlockSpec((None, WINDOW, KV_WIDTH),
                           lambda s: (mix(s) // tps, jnp.maximum((mix(s) % tps) * nb - 1, 0), 0))
    nxt = lambda s: jnp.minimum(s + 1, n_tiles - 1)
    first_u = pl.BlockSpec((None, tm, POOL_WIDTH), lambda s: (0, 0, 0))
    first_hist = pl.BlockSpec((None, HIST, POOL_WIDTH), lambda s: (0, 0, 0))
    next_u = pl.BlockSpec((None, tm, POOL_WIDTH), lambda s: (nxt(s) // tps, nxt(s) % tps, 0))
    next_hist = pl.BlockSpec((None, HIST, POOL_WIDTH),
                             lambda s: (nxt(s) // tps, jnp.maximum((nxt(s) % tps) * nh - 1, 0), 0))
    return pl.pallas_call(
        functools.partial(_kb_kernel, layer=layer, final=final, tiles_per_seq=tps),
        out_shape=(jax.ShapeDtypeStruct((g, r, D_MODEL), F32), jax.ShapeDtypeStruct((rs, D_MODEL), F32)),
        grid=(n_tiles + 2,),
        in_specs=[_SMEM, mix_row(D_MODEL), mix_row(ATTN_WIDTH), mix_row(KV_WIDTH), mix_row(KV_WIDTH),
                  prev_kv, prev_kv, first_u, first_hist, next_u, next_hist, mod_spec,
                  _full((rs, D_MODEL)), _full((rs, ATTN_WIDTH)), _full((rs, POOL_WIDTH)), mods_spec,
                  _full((DEPTH, 3, D_MODEL)),
                  _full((N_HEADS * WINDOW, 2 * WINDOW)),
                  _layer_resident((len(POOL_SIZES), POOL_GROUP_WIDTH, POOL_GROUP_WIDTH), layer),
                  _full((DEPTH, POOL_WIDTH)),
                  _layer_resident((D_MODEL, D_MODEL), layer), _layer_resident((D_MODEL, D_FF), layer),
                  _layer_resident((D_MODEL, D_FF), layer), _layer_resident((D_FF, D_MODEL), layer),
                  _full((1, D_MODEL))],
        out_specs=(tail_row(D_MODEL), _full((rs, D_MODEL))),
        scratch_shapes=[pltpu.VMEM((tm, D_MODEL), BF16), pltpu.VMEM((tm, D_MODEL), BF16),
                        pltpu.VMEM((2, tm, D_MODEL), F32), pltpu.VMEM((tm, D_FF), BF16),
                        pltpu.VMEM((2 * HIST + tm, POOL_WIDTH), F32), pltpu.VMEM((2 * HIST + tm, POOL_WIDTH), F32),
                        pltpu.VMEM((2 * HIST + tm, POOL_WIDTH - POOL_GROUP_WIDTH), F32),
                        pltpu.VMEM((tm, POOL_WIDTH), BF16),
                        pltpu.VMEM((N_HEADS * WINDOW, 2 * WINDOW), BF16)],
        compiler_params=pltpu.CompilerParams(
            dimension_semantics=("arbitrary",), vmem_limit_bytes=VMEM_LIMIT),
        name=name,
    )(sinks, x1, q, k, v, k, v, u, u, u, u, mod_all, x1s, attn_s, pooled_s, mods_tail,
      gain, bias, poolw, pscale, wout, wg, wu, wd, fgain)


def _ks_kernel(sinks_ref, q_ref, kn_ref, vn_ref, u_ref, ck_ref, cv_ref, hist_ref, bias_ref,
               o_ref, pooled_ref, nk_ref, nv_ref, newhist_ref,
               qs_ref, knr_ref, vnr_ref, s_ref, p_ref, acc_ref, *, layer):
    bt = q_ref.shape[0]
    for b in range(bt):
        for cache_ref, new_ref, out_ref in ((ck_ref, kn_ref, nk_ref), (cv_ref, vn_ref, nv_ref)):
            out_ref[b, 0:WINDOW - 1, :] = cache_ref[b, 1:WINDOW, :]
            out_ref[b, WINDOW - 1:WINDOW, :] = new_ref[b:b + 1, :]

    for r in range(POOL_STATE - 1):
        newhist_ref[:, r, :] = hist_ref[:, r + 1, :]
    newhist_ref[:, POOL_STATE - 1, :] = u_ref[...]
    for g, w in enumerate(POOL_SIZES):
        cols = slice(g * POOL_GROUP_WIDTH, (g + 1) * POOL_GROUP_WIDTH)
        ug = u_ref[:, cols]
        acc = ug
        for d in range(1, w):
            acc = acc + hist_ref[:, POOL_STATE - d, cols]
        pooled_ref[:, cols] = (acc / float(w) - ug).astype(BF16)

    rows = bt * N_HEADS
    nt = (((1,), (1,)), ((), ()))
    head_rows = lambda r: pl.ds(r, bt, stride=N_HEADS)
    lo = lax.broadcasted_iota(jnp.int32, (bt, LANES), 1) < HEAD_DIM
    kn = kn_ref[...].astype(BF16).astype(F32)
    vn = vn_ref[...].astype(BF16).astype(F32)
    for c in range(4):
        qc = q_ref[:, c * LANES:(c + 1) * LANES].astype(F32)
        qs_ref[head_rows(2 * c), :] = jnp.where(lo, qc, 0.0)
        qs_ref[head_rows(2 * c + 1), :] = jnp.where(lo, 0.0, qc)
    for r in range(N_HEADS):
        knr_ref[head_rows(r), :] = kn
        vnr_ref[head_rows(r), :] = vn

    for b in range(bt):
        rb = slice(b * N_HEADS, (b + 1) * N_HEADS)
        s_ref[rb, :] = lax.dot_general(qs_ref[rb, :].astype(BF16), ck_ref[b].astype(BF16), nt,
                                       preferred_element_type=F32)

    head = jnp.bitwise_and(lax.broadcasted_iota(jnp.int32, (rows, 1), 0), N_HEADS - 1)
    sink = jnp.zeros((rows, 1), F32)
    for r in range(N_HEADS):
        sink = jnp.where(head == r, sinks_ref[layer, _head_of_row(r)], sink)
    bias = jnp.concatenate([bias_ref[...]] * bt, axis=0)
    s_c = s_ref[...] + bias[:, 0:WINDOW]
    s_n = jnp.sum(qs_ref[...] * knr_ref[...], axis=-1, keepdims=True) + bias[:, WINDOW:WINDOW + 1]
    m = jnp.maximum(jnp.maximum(jnp.max(s_c, axis=-1, keepdims=True), s_n), sink)
    p_c = jnp.exp(s_c - m)
    p_n = jnp.exp(s_n - m)
    denom = jnp.sum(p_c, axis=-1, keepdims=True) + p_n + jnp.exp(sink - m)
    p_ref[...] = p_c / denom

    for b in range(bt):
        rb = slice(b * N_HEADS, (b + 1) * N_HEADS)
        acc_ref[rb, :] = jnp.dot(p_ref[rb, :].astype(BF16), cv_ref[b].astype(BF16), preferred_element_type=F32)
    acc_ref[...] = acc_ref[...] + (p_n / denom).astype(BF16).astype(F32) * vnr_ref[...]

    for c in range(4):
        o_ref[:, c * LANES:(c + 1) * LANES] = jnp.where(
            lo, acc_ref[head_rows(2 * c), :], acc_ref[head_rows(2 * c + 1), :]).astype(BF16)


def _ks_call(sinks, q, kn, vn, u, cache_k, cache_v, state_pool, layer, bias_s):
    nb = q.shape[0]
    bt = SAMPLE_SEQS_PER_STEP
    rows = bt * N_HEADS
    per_b = lambda w: pl.BlockSpec((bt, w), lambda i: (i, 0))
    cache = pl.BlockSpec((None, bt, WINDOW, KV_WIDTH), lambda i: (layer, i, 0, 0))
    new_cache = pl.BlockSpec((bt, WINDOW, KV_WIDTH), lambda i: (i, 0, 0))
    hist = pl.BlockSpec((None, bt, POOL_STATE, POOL_WIDTH), lambda i: (layer, i, 0, 0))
    new_hist = pl.BlockSpec((bt, POOL_STATE, POOL_WIDTH), lambda i: (i, 0, 0))
    return pl.pallas_call(
        functools.partial(_ks_kernel, layer=layer),
        out_shape=(jax.ShapeDtypeStruct((nb, ATTN_WIDTH), BF16),
                   jax.ShapeDtypeStruct((nb, POOL_WIDTH), BF16),
                   jax.ShapeDtypeStruct((nb, WINDOW, KV_WIDTH), F32),
                   jax.ShapeDtypeStruct((nb, WINDOW, KV_WIDTH), F32),
                   jax.ShapeDtypeStruct((nb, POOL_STATE, POOL_WIDTH), F32)),
        grid=(nb // bt,),
        in_specs=[_SMEM, per_b(ATTN_WIDTH), per_b(KV_WIDTH), per_b(KV_WIDTH), per_b(POOL_WIDTH), cache, cache,
                  hist, _full((N_HEADS, 2 * WINDOW))],
        out_specs=(per_b(ATTN_WIDTH), per_b(POOL_WIDTH), new_cache, new_cache, new_hist),
        scratch_shapes=[pltpu.VMEM((rows, KV_WIDTH), F32), pltpu.VMEM((rows, KV_WIDTH), F32),
                        pltpu.VMEM((rows, KV_WIDTH), F32), pltpu.VMEM((rows, WINDOW), F32),
                        pltpu.VMEM((rows, WINDOW), F32), pltpu.VMEM((rows, KV_WIDTH), F32)],
        compiler_params=pltpu.CompilerParams(dimension_semantics=("arbitrary",), vmem_limit_bytes=VMEM_LIMIT),
        name="sample_mixers",
    )(sinks, q, kn, vn, u, cache_k, cache_v, state_pool, bias_s)


def kernel(x_prompt, x_sample, c_prompt, c_sample, cache_k, cache_v, state_pool, w_ada, b_ada, norm_gain,
           w_in, w_out, sinks, rel_bias, pool_w, pool_scale, ffn1_wg, ffn1_wu, ffn1_wd, ffn2_wg, ffn2_wu,
           ffn2_wd, final_gain):
    n_p, seq, _ = x_prompt.shape
    n_s = x_sample.shape[0]

    mod_all, mods_tail = _modulation(jnp.concatenate([c_sample, c_prompt], axis=0), w_ada, b_ada, n_s)
    bias = _bias_table(rel_bias)
    bias_s = bias.reshape(N_HEADS, WINDOW, 2 * WINDOW)[:, 0, :]
    fgain = final_gain.reshape(1, D_MODEL)

    wq = (w_in[:, :, :ATTN_WIDTH].reshape(DEPTH, D_MODEL, 2, 4, HEAD_DIM).transpose(0, 1, 3, 2, 4)
          .reshape(DEPTH, D_MODEL, ATTN_WIDTH)) * (HEAD_DIM ** -0.5)
    win = jnp.concatenate([wq, w_in[:, :, ATTN_WIDTH:]], axis=2).astype(BF16)
    wo_attn = (w_out[:, :ATTN_WIDTH].reshape(DEPTH, 2, 4, HEAD_DIM, D_MODEL).transpose(0, 2, 1, 3, 4)
               .reshape(DEPTH, ATTN_WIDTH, D_MODEL))
    wout = jnp.concatenate([wo_attn, w_out[:, ATTN_WIDTH:]], axis=1).astype(BF16)
    wg1, wu1, wd1 = ffn1_wg.astype(BF16), ffn1_wu.astype(BF16), ffn1_wd.astype(BF16)
    wg2, wu2, wd2 = ffn2_wg.astype(BF16), ffn2_wu.astype(BF16), ffn2_wd.astype(BF16)
    poolw = pool_w.astype(BF16)

    xp = x_prompt
    xs = x_sample.reshape(n_s, D_MODEL)
    ck = cache_k.reshape(DEPTH, n_s, WINDOW, KV_WIDTH)
    cv = cache_v.reshape(DEPTH, n_s, WINDOW, KV_WIDTH)

    new_kp, new_vp, new_pp, new_ks, new_vs, new_ps = [], [], [], [], [], []
    for l in range(DEPTH):
        final = l == DEPTH - 1
        mod_p = pl.BlockSpec((None, n_p, MOD_WIDTH), lambda *_, l=l: (l, n_s // n_p, 0))
        mod_s = pl.BlockSpec((None, n_s, MOD_WIDTH), lambda *_, l=l: (l, 0, 0), pipeline_mode=pl.Buffered(1))

        x1, q, k, v, u, k_last, v_last, u_last, x1s, qs, ks, vs, us = _ka_call(
            xp, xs, mod_all, mod_p, mod_s, norm_gain, wg1, wu1, wd1, win, l, f"ffn1_inproj_l{l}")
        attn_s, pooled_s, ck_next, cv_next, pool_next = _ks_call(sinks, qs, ks, vs, us, ck, cv, state_pool, l, bias_s)
        xp, xs = _kb_call(sinks, x1, q, k, v, u, x1s, attn_s, pooled_s, mod_all, mod_p, mods_tail, norm_gain, bias,
                          poolw, pool_scale, wout, wg2, wu2, wd2, fgain, l, final, f"mixer_ffn2_l{l}")
        new_kp.append(k_last)
        new_vp.append(v_last)
        new_pp.append(u_last)
        new_ks.append(ck_next)
        new_vs.append(cv_next)
        new_ps.append(pool_next)

    kv_prompt = (DEPTH, n_p, WINDOW, N_KV_HEADS, HEAD_DIM)
    kv_sample = (DEPTH, n_s, WINDOW, N_KV_HEADS, HEAD_DIM)
    return (xp, xs.reshape(n_s, 1, D_MODEL), jnp.stack(new_kp).reshape(kv_prompt), jnp.stack(new_vp).reshape(kv_prompt),
            jnp.stack(new_pp)[:, :, HIST - POOL_STATE:], jnp.stack(new_ks).reshape(kv_sample),
            jnp.stack(new_vs).reshape(kv_sample), jnp.stack(new_ps))
```
